```python
import functools
import jax, jax.numpy as jnp
from jax import lax
import numpy as np


D_MODEL = 1024
BATCH = 8
SEQ = 2048
DEPTH = 1
DEC_BATCH = 128
DEC_SEQ = 8
PAST_LEN = 2048
PAGE_SIZE = 128

N_HEADS = 8
HEAD_DIM = 64
D_ATTN = N_HEADS * HEAD_DIM
D_CONV = D_MODEL - D_ATTN
D_MIX = D_ATTN + D_CONV
D_IN = 3 * D_ATTN + 2 * D_CONV
PATTERNS = ((128, 1), (512, 4), (2048, 16))
WIN_MAX = max(w for w, _ in PATTERNS)
CONV_WIDTH = 31
D_FF = ((-(-8 * D_MODEL // 3) + 255) // 256) * 256
BLK = 128
RMS_EPS = 1e-6
LN_EPS = 1e-5
ATTN_SCALE = HEAD_DIM ** -0.5

kernel_name = 'hymba_dilated_conformer_decode_step'


def _rms_norm(x, g):
    xf = x.astype(jnp.float32)
    y = xf * lax.rsqrt(jnp.mean(xf * xf, axis=-1, keepdims=True) + RMS_EPS)
    return (y * g.astype(jnp.float32)).astype(x.dtype)


def _layer_norm(x, g, b):
    xf = x.astype(jnp.float32)
    xc = xf - jnp.mean(xf, axis=-1, keepdims=True)
    y = xc * lax.rsqrt(jnp.mean(xc * xc, axis=-1, keepdims=True) + LN_EPS)
    return (y * g.astype(jnp.float32) + b.astype(jnp.float32)).astype(x.dtype)


def _dilated_band_prompt(q, k, v, window, dilation):
    B, S, H, Dh = q.shape
    n_sub = window // dilation
    span = dilation * BLK
    s_pad = -(-S // span) * span
    L = s_pad // dilation
    nb = L // BLK

    def to_blocks(t):
        t = jnp.pad(t, ((0, 0), (0, s_pad - S), (0, 0), (0, 0)))
        t = t.reshape(B, L, dilation, H, Dh).transpose(0, 2, 1, 3, 4)
        return t.reshape(B, dilation, nb, BLK, H, Dh)

    def with_prev(t):
        prev = jnp.pad(t, ((0, 0), (0, 0), (1, 0), (0, 0), (0, 0), (0, 0)))[:, :, :-1]
        return jnp.concatenate([prev, t], axis=3)

    qb = to_blocks(q)
    kk = with_prev(to_blocks(k))
    vv = with_prev(to_blocks(v))
    s = jnp.einsum('brnqhd,brnkhd->brnhqk', qb, kk).astype(jnp.float32) * ATTN_SCALE
    qi = jnp.arange(BLK)[:, None]
    ki = jnp.arange(2 * BLK)[None, :]
    dist = BLK + qi - ki
    band = (dist >= 0) & (dist <= n_sub)
    blk = jnp.arange(nb)[:, None, None]
    valid = band[None] & (blk * BLK + ki[None] - BLK >= 0)
    s = jnp.where(valid[None, None, :, None], s, -jnp.inf)
    m = jnp.max(s, axis=-1, keepdims=True)
    p = jnp.exp(s - m)
    l = jnp.sum(p, axis=-1, keepdims=True)
    o = jnp.einsum('brnhqk,brnkhd->brnqhd', p / l, vv.astype(jnp.float32))
    lse = jnp.swapaxes((m + jnp.log(l))[..., 0], 3, 4)
    o = o.reshape(B, dilation, L, H, Dh).transpose(0, 2, 1, 3, 4).reshape(B, s_pad, H, Dh)[:, :S]
    lse = lse.reshape(B, dilation, L, H).transpose(0, 2, 1, 3).reshape(B, s_pad, H)[:, :S]
    return o, lse


def _dilated_gather_sample(q, k_all, v_all, window, dilation):
    T = q.shape[1]
    lb = k_all.shape[1] - T
    n_sub = window // dilation
    idx = lb + jnp.arange(T)[:, None] - dilation * jnp.arange(n_sub + 1)[None, :]
    valid = idx >= 0
    idx = jnp.maximum(idx, 0)
    kg = k_all[:, idx]
    vg = v_all[:, idx]
    s = jnp.einsum('bthd,btjhd->bthj', q, kg).astype(jnp.float32) * ATTN_SCALE
    s = jnp.where(valid[None, :, None, :], s, -jnp.inf)
    m = jnp.max(s, axis=-1, keepdims=True)
    p = jnp.exp(s - m)
    l = jnp.sum(p, axis=-1, keepdims=True)
    o = jnp.einsum('bthj,btjhd->bthd', p / l, vg.astype(jnp.float32))
    lse = (m + jnp.log(l))[..., 0]
    return o, lse


def _combine(results):
    o = jnp.stack([r[0] for r in results])
    lse = jnp.stack([r[1] for r in results])
    w = jax.nn.softmax(lse, axis=0)
    return jnp.sum(w[..., None] * o, axis=0)


def _attend_prompt(q, k, v):
    S = q.shape[1]
    out = _combine([_dilated_band_prompt(q, k, v, w, r) for w, r in PATTERNS])
    keep = min(WIN_MAX, S)
    return out, k[:, S - keep:], v[:, S - keep:]


def _attend_sample(q, k, v, k_buf, v_buf):
    k_all = jnp.concatenate([k_buf.astype(k.dtype), k], axis=1)
    v_all = jnp.concatenate([v_buf.astype(v.dtype), v], axis=1)
    out = _combine([_dilated_gather_sample(q, k_all, v_all, w, r) for w, r in PATTERNS])
    keep = min(WIN_MAX, k_all.shape[1])
    return out, k_all[:, -keep:], v_all[:, -keep:]


def _causal_depthwise(u_ext, w, b):
    out = lax.conv_general_dilated(u_ext, w[:, None, :].astype(u_ext.dtype), window_strides=(1,),
                                   padding='VALID', dimension_numbers=('NWC', 'WIO', 'NWC'),
                                   feature_group_count=u_ext.shape[-1])
    return out + b.astype(out.dtype)


def _layer(x, conv_left, attend, attn_norm_g, w_in, q_norm_g, k_norm_g, conv_w, conv_b,
           conv_ln_g, conv_ln_b, w_out, ffn_norm_g, w_gate, w_up, w_down):
    N, T, _ = x.shape
    h = _rms_norm(x, attn_norm_g)
    z = h @ w_in
    q, k, v, a, g = jnp.split(z, [D_ATTN, 2 * D_ATTN, 3 * D_ATTN, 3 * D_ATTN + D_CONV], axis=-1)
    q = _rms_norm(q.reshape(N, T, N_HEADS, HEAD_DIM), q_norm_g)
    k = _rms_norm(k.reshape(N, T, N_HEADS, HEAD_DIM), k_norm_g)
    v = v.reshape(N, T, N_HEADS, HEAD_DIM)
    attn, k_state, v_state = attend(q, k, v)
    u = a * jax.nn.sigmoid(g)
    u_ext = jnp.concatenate([conv_left.astype(u.dtype), u], axis=1)
    c = _causal_depthwise(u_ext, conv_w, conv_b)
    c = jax.nn.silu(_layer_norm(c, conv_ln_g, conv_ln_b))
    mix = jnp.concatenate([attn.reshape(N, T, D_ATTN).astype(x.dtype), c.astype(x.dtype)], axis=-1)
    x = x + mix @ w_out
    h = _rms_norm(x, ffn_norm_g)
    x = x + (jax.nn.silu(h @ w_gate) * (h @ w_up)) @ w_down
    return x, k_state, v_state, u_ext[:, -(CONV_WIDTH - 1):]


def setup_inputs(seed: int = 0) -> dict:
    key = jax.random.key(seed)
    ks = jax.random.split(key, 18)
    f32 = jnp.float32
    l_buf = min(WIN_MAX, PAST_LEN)

    def nrm(k, shape, scale):
        return scale * jax.random.normal(k, shape, f32)

    return {
        'x_prompt': nrm(ks[0], (BATCH, SEQ, D_MODEL), 1.0),
        'x_sample': nrm(ks[1], (DEC_BATCH, DEC_SEQ, D_MODEL), 1.0),
        'cache_k': nrm(ks[2], (DEPTH, DEC_BATCH, l_buf, N_HEADS, HEAD_DIM), 1.0),
        'cache_v': nrm(ks[3], (DEPTH, DEC_BATCH, l_buf, N_HEADS, HEAD_DIM), 1.0),
        'state_conv': nrm(ks[4], (DEPTH, DEC_BATCH, CONV_WIDTH - 1, D_CONV), 0.5),
        'attn_norm_g': 1.0 + nrm(ks[5], (DEPTH, D_MODEL), 0.02),
        'w_in': nrm(ks[6], (DEPTH, D_MODEL, D_IN), D_MODEL ** -0.5),
        'q_norm_g': 1.0 + nrm(ks[7], (DEPTH, HEAD_DIM), 0.02),
        'k_norm_g': 1.0 + nrm(ks[8], (DEPTH, HEAD_DIM), 0.02),
        'conv_w': nrm(ks[9], (DEPTH, CONV_WIDTH, D_CONV), CONV_WIDTH ** -0.5),
        'conv_b': nrm(ks[10], (DEPTH, D_CONV), 0.02),
        'conv_ln_g': 1.0 + nrm(ks[11], (DEPTH, D_CONV), 0.02),
        'conv_ln_b': nrm(ks[12], (DEPTH, D_CONV), 0.02),
        'w_out': nrm(ks[13], (DEPTH, D_MIX, D_MODEL), D_MIX ** -0.5),
        'ffn_norm_g': 1.0 + nrm(ks[14], (DEPTH, D_MODEL), 0.02),
        'w_gate': nrm(ks[15], (DEPTH, D_MODEL, D_FF), D_MODEL ** -0.5),
        'w_up': nrm(ks[16], (DEPTH, D_MODEL, D_FF), D_MODEL ** -0.5),
        'w_down': nrm(ks[17], (DEPTH, D_FF, D_MODEL), D_FF ** -0.5),
    }


def reference(x_prompt, x_sample, cache_k, cache_v, state_conv, attn_norm_g, w_in, q_norm_g,
              k_norm_g, conv_w, conv_b, conv_ln_g, conv_ln_b, w_out, ffn_norm_g, w_gate, w_up, w_down):
    y_p, y_s = x_prompt, x_sample
    nk_p, nv_p, nc_p, nk_s, nv_s, nc_s = [], [], [], [], [], []
    for l in range(DEPTH):
        w = (attn_norm_g[l], w_in[l], q_norm_g[l], k_norm_g[l], conv_w[l], conv_b[l],
             conv_ln_g[l], conv_ln_b[l], w_out[l], ffn_norm_g[l], w_gate[l], w_up[l], w_down[l])
        zeros_left = jnp.zeros((y_p.shape[0], CONV_WIDTH - 1, D_CONV), y_p.dtype)
        y_p, kp, vp, cp = _layer(y_p, zeros_left, _attend_prompt, *w)
        attend_s = functools.partial(_attend_sample, k_buf=cache_k[l], v_buf=cache_v[l])
        y_s, ks_, vs_, cs_ = _layer(y_s, state_conv[l], attend_s, *w)
        nk_p.append(kp)
        nv_p.append(vp)
        nc_p.append(cp)
        nk_s.append(ks_)
        nv_s.append(vs_)
        nc_s.append(cs_)
    return (y_p, y_s, jnp.stack(nk_p), jnp.stack(nv_p), jnp.stack(nc_p),
            jnp.stack(nk_s), jnp.stack(nv_s), jnp.stack(nc_s))
```

```python
import functools

import numpy as np
import jax
import jax.numpy as jnp
from jax import lax
from jax.experimental import pallas as pl
from jax.experimental.pallas import tpu as pltpu

N_HEADS = 8
HEAD_DIM = 64
D_ATTN = N_HEADS * HEAD_DIM
PATTERNS = ((128, 1), (512, 4), (2048, 16))
WIN_MAX = max(w for w, _ in PATTERNS)
CONV_WIDTH = 31
CONV_LEFT = CONV_WIDTH - 1
BLK = 128
RMS_EPS = 1e-6
LN_EPS = 1e-5
ATTN_SCALE = HEAD_DIM ** -0.5
NEG_BIG = -1e30

LANES = 128
HEADS_PER_LANE_TILE = LANES // HEAD_DIM
VMEM_LIMIT_BYTES = 56 * 1024 * 1024

F32 = jnp.float32
BF16 = jnp.bfloat16


def _params(n_axes):
    return pltpu.CompilerParams(dimension_semantics=("arbitrary",) * n_axes,
                                vmem_limit_bytes=VMEM_LIMIT_BYTES)


def _const_spec(shape):
    return pl.BlockSpec(shape, lambda *_: (0,) * len(shape), pipeline_mode=pl.Buffered(1))


def _sigmoid(x):
    return 1.0 / (1.0 + jnp.exp(-x))


def _rms_rows(x, g):
    ms = jnp.mean(x * x, axis=-1, keepdims=True)
    return x * lax.rsqrt(ms + RMS_EPS) * g


def _inproj_kernel(x_ref, g_ref, w_ref, qg_ref, kg_ref, seg_ref,
                   qb_ref, k_ref, kb_ref, v_ref, vb_ref, u_ref):
    h = _rms_rows(x_ref[...], g_ref[...]).astype(BF16)

    def proj(col):
        return jnp.dot(h, w_ref[:, col * D_ATTN:(col + 1) * D_ATTN], preferred_element_type=F32)

    def head_rms(t, gain):
        t2 = t * t
        hi = t2.astype(BF16)
        lo = (t2 - hi.astype(F32)).astype(BF16)
        ss = (jnp.dot(hi, seg_ref[...], preferred_element_type=F32)
              + jnp.dot(lo, seg_ref[...], preferred_element_type=F32))
        return t * lax.rsqrt(ss * (1.0 / HEAD_DIM) + RMS_EPS) * gain

    q = head_rms(proj(0), qg_ref[...]) * ATTN_SCALE
    qb_ref[...] = q.astype(BF16)
    k = head_rms(proj(1), kg_ref[...])
    k_ref[...] = k
    kb_ref[...] = k.astype(BF16)
    v = proj(2)
    v_ref[...] = v
    vb_ref[...] = v.astype(BF16)
    u_ref[...] = proj(3) * _sigmoid(proj(4))


def _inproj(x2d, g, w_in_b, qg, kg, seg, tm):
    m, d = x2d.shape
    tm = min(tm, m)
    d_in = w_in_b.shape[1]
    row = lambda i: (i, 0)
    out_f = jax.ShapeDtypeStruct((m, D_ATTN), F32)
    out_b = jax.ShapeDtypeStruct((m, D_ATTN), BF16)
    tile = pl.BlockSpec((tm, D_ATTN), row)
    return pl.pallas_call(
        _inproj_kernel,
        grid=(m // tm,),
        in_specs=[pl.BlockSpec((tm, d), row), _const_spec((1, d)), _const_spec((d, d_in)),
                  _const_spec((1, D_ATTN)), _const_spec((1, D_ATTN)), _const_spec((D_ATTN, D_ATTN))],
        out_specs=[tile] * 6,
        out_shape=[out_b, out_f, out_b, out_f, out_b, out_f],
        compiler_params=_params(1),
        name="inproj",
    )(x2d, g, w_in_b, qg, kg, seg)


def _band_attn_kernel(q_ref, k_ref, v_ref, bias0_ref, bias_ref, o_ref, lse_ref, *, n_blocks):
    lane = lax.broadcasted_iota(jnp.int32, (BLK, LANES), 1)
    upper = lane >= HEAD_DIM

    def block(r0, k0, n_keys, bias):
        for hp in range(N_HEADS // HEADS_PER_LANE_TILE):
            lanes = slice(hp * LANES, (hp + 1) * LANES)
            qp = q_ref[0, pl.ds(r0, BLK), lanes]
            kp = k_ref[0, pl.ds(k0, n_keys), lanes]
            vp = v_ref[0, pl.ds(k0, n_keys), lanes]
            o_pair = lse_pair = None
            for hh in range(HEADS_PER_LANE_TILE):
                qm = jnp.where(upper if hh else ~upper, qp, jnp.zeros_like(qp))
                s = lax.dot_general(qm, kp, (((1,), (1,)), ((), ())),
                                    preferred_element_type=F32) + bias
                m = jnp.max(s, axis=-1, keepdims=True)
                p = jnp.exp(s - m)
                l = jnp.sum(p, axis=-1, keepdims=True)
                o = jnp.dot(p.astype(BF16), vp, preferred_element_type=F32) * (1.0 / l)
                lse = jnp.broadcast_to(m + jnp.log(l), (BLK, LANES))
                if hh == 0:
                    o_pair, lse_pair = o, lse
                else:
                    o_pair = jnp.where(upper, o, o_pair)
                    lse_pair = jnp.where(upper, lse, lse_pair)
            o_ref[0, pl.ds(r0, BLK), lanes] = o_pair.astype(o_ref.dtype)
            lse_ref[0, pl.ds(r0, BLK), lanes] = lse_pair

    block(0, 0, BLK, bias0_ref[...])
    if n_blocks > 1:
        bias = bias_ref[...]

        def body(i, carry):
            r0 = pl.multiple_of(i * BLK, BLK)
            block(r0, pl.multiple_of(r0 - BLK, BLK), 2 * BLK, bias)
            return carry

        lax.fori_loop(1, n_blocks, body, 0)


def _band_biases(n_sub):
    qi = np.arange(BLK)[:, None]
    ki = np.arange(2 * BLK)[None, :]
    dist = BLK + qi - ki
    band = (dist >= 0) & (dist <= n_sub)
    first = qi >= np.arange(BLK)[None, :]
    to_bias = lambda ok: np.where(ok, 0.0, NEG_BIG).astype(np.float32)
    return to_bias(first & (qi - np.arange(BLK)[None, :] <= n_sub)), to_bias(band)


def _band_attention(qb, kb, vb, batch, seq, window, dilation):
    n_sub = window // dilation
    length = seq // dilation
    n_blocks = length // BLK
    bias0, bias = _band_biases(n_sub)
    view = lambda t: t.reshape(batch, length, dilation * D_ATTN)
    blk = pl.BlockSpec((1, length, D_ATTN), lambda b, c: (b, 0, c))
    o, lse = pl.pallas_call(
        functools.partial(_band_attn_kernel, n_blocks=n_blocks),
        grid=(batch, dilation),
        in_specs=[blk, blk, blk, _const_spec((BLK, BLK)), _const_spec((BLK, 2 * BLK))],
        out_specs=[blk, blk],
        out_shape=[jax.ShapeDtypeStruct((batch, length, dilation * D_ATTN), BF16),
                   jax.ShapeDtypeStruct((batch, length, dilation * D_ATTN), F32)],
        compiler_params=_params(2),
        name=f"band_attn_d{dilation}",
    )(view(qb), view(kb), view(vb), jnp.asarray(bias0), jnp.asarray(bias))
    return o.reshape(batch * seq, D_ATTN), lse.reshape(batch * seq, D_ATTN)


CONV_PAD = 32
CONV_CHUNK = 64


def _conv_ln_swish(ext_ref, row0, n_rows, w_ref, b_ref, lg_ref, lb_ref):
    acc = None
    for j in range(CONV_WIDTH):
        term = ext_ref[pl.ds(row0 + j, n_rows), :] * w_ref[j:j + 1, :]
        acc = term if acc is None else acc + term
    c = acc + b_ref[...]
    mu = jnp.mean(c, axis=-1, keepdims=True)
    cc = c - mu
    var = jnp.mean(cc * cc, axis=-1, keepdims=True)
    y = cc * lax.rsqrt(var + LN_EPS) * lg_ref[...] + lb_ref[...]
    return y * _sigmoid(y)


def _mix_kernel(u_ref, up_ref, w_ref, b_ref, lg_ref, lb_ref,
                o1_ref, o2_ref, o3_ref, l1_ref, l2_ref, l3_ref, mix_ref, ext_ref, *, tt):
    t = pl.program_id(1)
    prev = up_ref[0]
    ext_ref[0:CONV_PAD, :] = jnp.where(t > 0, prev, jnp.zeros_like(prev))
    ext_ref[CONV_PAD:CONV_PAD + tt, :] = u_ref[0]
    l1, l2, l3 = l1_ref[...], l2_ref[...], l3_ref[...]
    mx = jnp.maximum(jnp.maximum(l1, l2), l3)
    e1, e2, e3 = jnp.exp(l1 - mx), jnp.exp(l2 - mx), jnp.exp(l3 - mx)
    inv = 1.0 / (e1 + e2 + e3)
    attn = (e1 * o1_ref[...].astype(F32) + e2 * o2_ref[...].astype(F32)
            + e3 * o3_ref[...].astype(F32)) * inv
    mix_ref[:, 0:D_ATTN] = attn.astype(mix_ref.dtype)
    d_conv = u_ref.shape[-1]
    for ch in range(tt // CONV_CHUNK):
        c = _conv_ln_swish(ext_ref, ch * CONV_CHUNK + CONV_PAD - CONV_LEFT, CONV_CHUNK,
                           w_ref, b_ref, lg_ref, lb_ref)
        mix_ref[ch * CONV_CHUNK:(ch + 1) * CONV_CHUNK, D_ATTN:D_ATTN + d_conv] = c.astype(mix_ref.dtype)


def _mix_prompt(u, conv_w, conv_b, ln_g, ln_b, outs, lses, batch, seq, tt):
    d_conv = u.shape[-1]
    u3 = u.reshape(batch, seq, d_conv)
    n_t = seq // tt
    per = tt // CONV_PAD
    cur = pl.BlockSpec((1, tt, d_conv), lambda b, t: (b, t, 0))
    prev = pl.BlockSpec((1, CONV_PAD, d_conv), lambda b, t: (b, jnp.maximum(t * per - 1, 0), 0))
    flat = pl.BlockSpec((tt, D_ATTN), lambda b, t: (b * n_t + t, 0))
    return pl.pallas_call(
        functools.partial(_mix_kernel, tt=tt),
        grid=(batch, n_t),
        in_specs=[cur, prev, _const_spec(conv_w.shape), _const_spec((1, d_conv)),
                  _const_spec((1, d_conv)), _const_spec((1, d_conv))] + [flat] * 6,
        out_specs=pl.BlockSpec((tt, D_ATTN + d_conv), lambda b, t: (b * n_t + t, 0)),
        out_shape=jax.ShapeDtypeStruct((batch * seq, D_ATTN + d_conv), BF16),
        scratch_shapes=[pltpu.VMEM((CONV_PAD + tt, d_conv), F32)],
        compiler_params=_params(2),
        name="mix_prompt",
    )(u3, u3, conv_w, conv_b, ln_g, ln_b, *outs, *lses)


def _sample_conv_kernel(st_ref, u_ref, w_ref, b_ref, lg_ref, lb_ref, c_ref, ns_ref, ext_ref, *, nb, t_new):
    for n in range(nb):
        ext_ref[0:CONV_LEFT, :] = st_ref[n]
        ext_ref[CONV_LEFT:CONV_LEFT + t_new, :] = u_ref[n]
        c = _conv_ln_swish(ext_ref, 0, t_new, w_ref, b_ref, lg_ref, lb_ref)
        c_ref[n] = c.astype(c_ref.dtype)
        ns_ref[n] = ext_ref[t_new:t_new + CONV_LEFT, :]


def _sample_conv(state, u_s, conv_w, conv_b, ln_g, ln_b, nb):
    n, left, d_conv = state.shape
    t_new = u_s.shape[1]
    blk = lambda rows: pl.BlockSpec((nb, rows, d_conv), lambda i: (i, 0, 0))
    return pl.pallas_call(
        functools.partial(_sample_conv_kernel, nb=nb, t_new=t_new),
        grid=(n // nb,),
        in_specs=[blk(left), blk(t_new), _const_spec(conv_w.shape), _const_spec((1, d_conv)),
                  _const_spec((1, d_conv)), _const_spec((1, d_conv))],
        out_specs=[blk(t_new), blk(left)],
        out_shape=[jax.ShapeDtypeStruct((n, t_new, d_conv), BF16),
                   jax.ShapeDtypeStruct((n, left, d_conv), F32)],
        scratch_shapes=[pltpu.VMEM((left + t_new + 2, d_conv), F32)],
        compiler_params=_params(1),
        name="sample_conv",
    )(state, u_s, conv_w, conv_b, ln_g, ln_b)


def _multiplicity(dist):
    dist = np.asarray(dist)
    c = np.zeros(dist.shape, np.float32)
    for window, dilation in PATTERNS:
        c += ((dist >= 0) & (dist <= window) & (dist % dilation == 0)).astype(np.float32)
    return c


def _sample_attn_kernel(q_ref, kn_ref, vn_ref, ck_ref, cv_ref, cnt_ref, attn_ref, ok_ref, ov_ref,
                        *, t_new, l_buf):
    rows = N_HEADS * t_new
    keep = l_buf - t_new
    ck = ck_ref[0]
    cv = cv_ref[0]
    kn = kn_ref[0]
    vn = vn_ref[0]
    ok_ref[0, 0:keep, :] = ck[t_new:, :]
    ok_ref[0, keep:l_buf, :] = kn
    ov_ref[0, 0:keep, :] = cv[t_new:, :]
    ov_ref[0, keep:l_buf, :] = vn

    row = lax.broadcasted_iota(jnp.int32, (rows, D_ATTN), 0)
    lane = lax.broadcasted_iota(jnp.int32, (rows, D_ATTN), 1)
    own_head = (row // t_new) == (lane // HEAD_DIM)
    q_rep = jnp.concatenate([q_ref[0].astype(F32)] * N_HEADS, axis=0)
    q_bd = jnp.where(own_head, q_rep, 0.0)

    cnt = cnt_ref[...]
    s_c = lax.dot_general(q_bd.astype(BF16), ck.astype(BF16), (((1,), (1,)), ((), ())),
                          preferred_element_type=F32)
    s_c = jnp.where(cnt > 0.0, s_c, NEG_BIG)
    m = jnp.max(s_c, axis=-1, keepdims=True)

    t_of_row = lax.broadcasted_iota(jnp.int32, (rows, 1), 0) % t_new
    s_n, c_n = [], []
    for tp in range(t_new):
        d = t_of_row - tp
        c = jnp.zeros((rows, 1), F32)
        for _, dilation in PATTERNS:
            c = c + jnp.where((d >= 0) & (d % dilation == 0), 1.0, 0.0)
        s = jnp.sum(q_bd * kn[tp:tp + 1, :], axis=-1, keepdims=True)
        s = jnp.where(c > 0.0, s, NEG_BIG)
        m = jnp.maximum(m, s)
        s_n.append(s)
        c_n.append(c)

    p_c = cnt * jnp.exp(s_c - m)
    l = jnp.sum(p_c, axis=-1, keepdims=True)
    acc = jnp.dot(p_c.astype(BF16), cv.astype(BF16), preferred_element_type=F32)
    for tp in range(t_new):
        p = c_n[tp] * jnp.exp(s_n[tp] - m)
        l = l + p
        acc = acc + p * vn[tp:tp + 1, :]
    acc = jnp.where(own_head, acc * (1.0 / l), 0.0)
    out = acc[0:t_new, :]
    for h in range(1, N_HEADS):
        out = out + acc[h * t_new:(h + 1) * t_new, :]
    attn_ref[0] = out.astype(attn_ref.dtype)


def _sample_attention(qb_s, k_s, v_s, cache_k, cache_v, layer):
    n, t_new, _ = qb_s.shape
    l_buf = cache_k.shape[1]
    assert l_buf == WIN_MAX and t_new % 8 == 0
    dist = l_buf + np.arange(t_new)[:, None] - np.arange(l_buf)[None, :]
    cnt = np.tile(_multiplicity(dist), (N_HEADS, 1))
    new = pl.BlockSpec((1, t_new, D_ATTN), lambda i: (i, 0, 0))
    big = pl.BlockSpec((1, l_buf, D_ATTN), lambda i: (i, 0, 0))
    big_in = pl.BlockSpec((1, l_buf, D_ATTN), lambda i: (layer * n + i, 0, 0))
    return pl.pallas_call(
        functools.partial(_sample_attn_kernel, t_new=t_new, l_buf=l_buf),
        grid=(n,),
        in_specs=[new, new, new, big_in, big_in, _const_spec(cnt.shape)],
        out_specs=[new, big, big],
        out_shape=[jax.ShapeDtypeStruct((n, t_new, D_ATTN), BF16),
                   jax.ShapeDtypeStruct((n, l_buf, D_ATTN), F32),
                   jax.ShapeDtypeStruct((n, l_buf, D_ATTN), F32)],
        compiler_params=_params(1),
        name="sample_attn",
    )(qb_s, k_s, v_s, cache_k, cache_v, jnp.asarray(cnt))


def _ffn_kernel(x_ref, mix_ref, wo_ref, g_ref, wg_ref, wu_ref, wd_ref, y_ref):
    x1 = x_ref[...] + jnp.dot(mix_ref[...], wo_ref[...], preferred_element_type=F32)
    h = _rms_rows(x1, g_ref[...]).astype(BF16)
    gate = jnp.dot(h, wg_ref[...], preferred_element_type=F32)
    up = jnp.dot(h, wu_ref[...], preferred_element_type=F32)
    act = (gate * _sigmoid(gate) * up).astype(BF16)
    y_ref[...] = x1 + jnp.dot(act, wd_ref[...], preferred_element_type=F32)


def _outproj_ffn(x2d, mix, wo_b, g, wg_b, wu_b, wd_b, tm):
    m, d = x2d.shape
    tm = min(tm, m)
    d_ff = wg_b.shape[1]
    row = lambda i: (i, 0)
    return pl.pallas_call(
        _ffn_kernel,
        grid=(m // tm,),
        in_specs=[pl.BlockSpec((tm, d), row), pl.BlockSpec((tm, mix.shape[1]), row),
                  _const_spec(wo_b.shape), _const_spec((1, d)),
                  _const_spec((d, d_ff)), _const_spec((d, d_ff)), _const_spec((d_ff, d))],
        out_specs=pl.BlockSpec((tm, d), row),
        out_shape=jax.ShapeDtypeStruct((m, d), F32),
        compiler_params=_params(1),
        name="outproj_ffn",
    )(x2d, mix, wo_b, g, wg_b, wu_b, wd_b)


TOKEN_TILE = 512
MIX_TIME_TILE = 256
SAMPLE_CONV_BATCH = 8


def kernel(x_prompt, x_sample, cache_k, cache_v, state_conv, attn_norm_g, w_in, q_norm_g, k_norm_g,
           conv_w, conv_b, conv_ln_g, conv_ln_b, w_out, ffn_norm_g, w_gate, w_up, w_down):
    batch, seq, d_model = x_prompt.shape
    n_dec, t_new, _ = x_sample.shape
    depth = w_in.shape[0]
    d_conv = conv_w.shape[-1]
    l_buf = cache_k.shape[2]
    assert seq == WIN_MAX and l_buf == WIN_MAX and seq % (BLK * PATTERNS[-1][1]) == 0
    assert w_in.shape[-1] == 3 * D_ATTN + 2 * d_conv and d_conv == D_ATTN

    seg = jnp.asarray(np.kron(np.eye(N_HEADS), np.ones((HEAD_DIM, HEAD_DIM))), BF16)
    row = lambda v: v.reshape(1, -1)
    y_p = x_prompt.reshape(batch * seq, d_model)
    y_s = x_sample.reshape(n_dec * t_new, d_model)
    outs = [[] for _ in range(6)]
    for l in range(depth):
        w_in_b = w_in[l].astype(BF16)
        wo_b, wg_b, wu_b, wd_b = (w[l].astype(BF16) for w in (w_out, w_gate, w_up, w_down))
        g_attn, g_ffn = row(attn_norm_g[l]), row(ffn_norm_g[l])
        qg = row(jnp.tile(q_norm_g[l], N_HEADS))
        kg = row(jnp.tile(k_norm_g[l], N_HEADS))
        cb, lg, lb = row(conv_b[l]), row(conv_ln_g[l]), row(conv_ln_b[l])

        qb, k, kb, v, vb, u = _inproj(y_p, g_attn, w_in_b, qg, kg, seg, TOKEN_TILE)
        res = [_band_attention(qb, kb, vb, batch, seq, w, d) for w, d in PATTERNS]
        mix = _mix_prompt(u, conv_w[l], cb, lg, lb, [r[0] for r in res], [r[1] for r in res],
                          batch, seq, MIX_TIME_TILE)
        y_p = _outproj_ffn(y_p, mix, wo_b, g_ffn, wg_b, wu_b, wd_b, TOKEN_TILE)
        outs[0].append(k.reshape(batch, seq, N_HEADS, HEAD_DIM))
        outs[1].append(v.reshape(batch, seq, N_HEADS, HEAD_DIM))
        outs[2].append(u.reshape(batch, seq, d_conv)[:, seq - CONV_LEFT:])

        qb, k, kb, v, vb, u = _inproj(y_s, g_attn, w_in_b, qg, kg, seg, TOKEN_TILE)
        as3 = lambda t: t.reshape(n_dec, t_new, D_ATTN)
        attn_s, nk, nv = _sample_attention(as3(qb), as3(k), as3(v),
                                           cache_k.reshape(depth * n_dec, l_buf, D_ATTN),
                                           cache_v.reshape(depth * n_dec, l_buf, D_ATTN), l)
        c_s, ns = _sample_conv(state_conv[l], as3(u), conv_w[l], cb, lg, lb, SAMPLE_CONV_BATCH)
        mix = jnp.concatenate([attn_s, c_s], axis=-1).reshape(n_dec * t_new, D_ATTN + d_conv)
        y_s = _outproj_ffn(y_s, mix, wo_b, g_ffn, wg_b, wu_b, wd_b, TOKEN_TILE)
        outs[3].append(nk.reshape(n_dec, l_buf, N_HEADS, HEAD_DIM))
        outs[4].append(nv.reshape(n_dec, l_buf, N_HEADS, HEAD_DIM))
        outs[5].append(ns)

    stack = lambda xs: xs[0][None] if len(xs) == 1 else jnp.stack(xs)
    return (y_p.reshape(batch, seq, d_model), y_s.reshape(n_dec, t_new, d_model)) + tuple(
        stack(o) for o in outs)
```

```python
import functools

import numpy as np
import jax
import jax.numpy as jnp
from jax import lax
from jax.experimental import pallas as pl
from jax.experimental.pallas import tpu as pltpu

N_HEADS = 8
HEAD_DIM = 64
D_ATTN = N_HEADS * HEAD_DIM
PATTERNS = ((128, 1), (512, 4), (2048, 16))
WIN_MAX = max(w for w, _ in PATTERNS)
CONV_WIDTH = 31
CONV_LEFT = CONV_WIDTH - 1
BLK = 128
RMS_EPS = 1e-6
LN_EPS = 1e-5
ATTN_SCALE = HEAD_DIM ** -0.5
NEG_BIG = -1e30

LANES = 128
HEADS_PER_LANE_TILE = LANES // HEAD_DIM
VMEM_LIMIT_BYTES = 56 * 1024 * 1024

F32 = jnp.float32
BF16 = jnp.bfloat16


def _params(n_axes):
    return pltpu.CompilerParams(dimension_semantics=("arbitrary",) * n_axes,
                                vmem_limit_bytes=VMEM_LIMIT_BYTES)


def _const_spec(shape):
    return pl.BlockSpec(shape, lambda *_: (0,) * len(shape), pipeline_mode=pl.Buffered(1))


def _sigmoid(x):
    return 1.0 / (1.0 + jnp.exp(-x))


def _rms_rows(x, g):
    ms = jnp.mean(x * x, axis=-1, keepdims=True)
    return x * lax.rsqrt(ms + RMS_EPS) * g


def _class_lanes(c):
    return slice(c * D_ATTN, (c + 1) * D_ATTN)


def _tiles_scratch(rows, width):
    return pltpu.VMEM((width // LANES, rows, LANES), F32)


def _put_tiles(scr, x):
    for lt in range(scr.shape[0]):
        scr[lt] = x[:, lt * LANES:(lt + 1) * LANES]


def _get_tiles(scr):
    return jnp.concatenate([scr[lt] for lt in range(scr.shape[0])], axis=1)


def _get_rows(scr, start, n, stride):
    return jnp.concatenate([scr[lt, pl.ds(start, n, stride=stride), :] for lt in range(scr.shape[0])],
                           axis=1)


def _put_rows(scr, start, stride, x):
    for lt in range(scr.shape[0]):
        scr[lt, pl.ds(start, x.shape[0], stride=stride), :] = x[:, lt * LANES:(lt + 1) * LANES]


def _inproj_kernel(x_ref, g_ref, w_ref, qg_ref, kg_ref, seg_ref, *refs, dilations):
    h = _rms_rows(x_ref[...], g_ref[...]).astype(BF16)

    def proj(col):
        return jnp.dot(h, w_ref[:, col * D_ATTN:(col + 1) * D_ATTN], preferred_element_type=F32)

    def head_rms(t, gain):
        t2 = t * t
        hi = t2.astype(BF16)
        lo = (t2 - hi.astype(F32)).astype(BF16)
        ss = (jnp.dot(hi, seg_ref[...], preferred_element_type=F32)
              + jnp.dot(lo, seg_ref[...], preferred_element_type=F32))
        return t * lax.rsqrt(ss * (1.0 / HEAD_DIM) + RMS_EPS) * gain

    q = head_rms(proj(0), qg_ref[...]) * ATTN_SCALE
    k = head_rms(proj(1), kg_ref[...])
    v = proj(2)
    u = proj(3) * _sigmoid(proj(4))
    if not dilations:
        q_ref, k_ref, v_ref, u_ref = refs
        q_ref[...] = q
        k_ref[...] = k
        v_ref[...] = v
        u_ref[...] = u
        return

    n_views = 3 * len(dilations)
    nat = refs[0:3]
    views = refs[3:3 + n_views]
    kt_ref, vt_ref, u_ref = refs[3 + n_views:6 + n_views]
    scratch = refs[6 + n_views:]
    tm = x_ref.shape[0]
    u_ref[...] = u
    kt_ref[0] = k.T
    vt_ref[0] = v.T
    for i, t in enumerate((q, k, v)):
        nat[i][...] = t.astype(BF16)
        _put_tiles(scratch[i], t)
        for j, dil in enumerate(dilations):
            for c in range(dil):
                views[3 * j + i][0, :, _class_lanes(c)] = (
                    _get_rows(scratch[i], c, tm // dil, dil).astype(BF16))


def _inproj_sample(x2d, g, w_in_b, qg, kg, seg, tm):
    m, d = x2d.shape
    tm = min(tm, m)
    row = lambda i: (i, 0)
    out_f = jax.ShapeDtypeStruct((m, D_ATTN), F32)
    tile = pl.BlockSpec((tm, D_ATTN), row)
    return pl.pallas_call(
        functools.partial(_inproj_kernel, dilations=()),
        grid=(m // tm,),
        in_specs=[pl.BlockSpec((tm, d), row), _const_spec((1, d)), _const_spec(w_in_b.shape),
                  _const_spec((1, D_ATTN)), _const_spec((1, D_ATTN)), _const_spec((D_ATTN, D_ATTN))],
        out_specs=[tile] * 4,
        out_shape=[out_f] * 4,
        compiler_params=_params(1),
        name="inproj_sample",
    )(x2d, g, w_in_b, qg, kg, seg)


def _inproj_prompt(x2d, g, w_in_b, qg, kg, seg, batch, seq, tm):
    m, d = x2d.shape
    n_t = seq // tm
    dilations = tuple(dil for _, dil in PATTERNS if dil > 1)
    flat = lambda b, t: (b * n_t + t, 0)
    nat_spec = pl.BlockSpec((tm, D_ATTN), flat)
    nat_b = jax.ShapeDtypeStruct((m, D_ATTN), BF16)
    view_specs, view_shapes = [], []
    for dil in dilations:
        view_specs += [pl.BlockSpec((1, tm // dil, dil * D_ATTN), lambda b, t: (b, t, 0))] * 3
        view_shapes += [jax.ShapeDtypeStruct((batch, seq // dil, dil * D_ATTN), BF16)] * 3
    t_spec = pl.BlockSpec((1, D_ATTN, tm), lambda b, t: (b, 0, t))
    t_shape = jax.ShapeDtypeStruct((batch, D_ATTN, seq), F32)
    res = pl.pallas_call(
        functools.partial(_inproj_kernel, dilations=dilations),
        grid=(batch, n_t),
        in_specs=[pl.BlockSpec((tm, d), flat), _const_spec((1, d)), _const_spec(w_in_b.shape),
                  _const_spec((1, D_ATTN)), _const_spec((1, D_ATTN)), _const_spec((D_ATTN, D_ATTN))],
        out_specs=[nat_spec] * 3 + view_specs + [t_spec, t_spec, nat_spec],
        out_shape=[nat_b] * 3 + view_shapes + [t_shape, t_shape, jax.ShapeDtypeStruct((m, D_ATTN), F32)],
        scratch_shapes=[_tiles_scratch(tm, D_ATTN)] * 3,
        compiler_params=_params(2),
        name="inproj_prompt",
    )(x2d, g, w_in_b, qg, kg, seg)
    n_views = 3 * len(dilations)
    qkv = {1: tuple(t.reshape(batch, seq, D_ATTN) for t in res[0:3])}
    for j, dil in enumerate(dilations):
        qkv[dil] = tuple(res[3 + 3 * j:6 + 3 * j])
    kt, vt, u = res[3 + n_views:]
    return qkv, kt, vt, u


def _band_attn_kernel(q_ref, k_ref, v_ref, bias0_ref, bias_ref, o_ref, lse_ref, *, n_blocks):
    lane = lax.broadcasted_iota(jnp.int32, (BLK, LANES), 1)
    upper = lane >= HEAD_DIM

    def block(r0, k0, n_keys, bias):
        for hp in range(N_HEADS // HEADS_PER_LANE_TILE):
            lanes = slice(hp * LANES, (hp + 1) * LANES)
            qp = q_ref[0, pl.ds(r0, BLK), lanes]
            kp = k_ref[0, pl.ds(k0, n_keys), lanes]
            vp = v_ref[0, pl.ds(k0, n_keys), lanes]
            o_pair = lse_pair = None
            for hh in range(HEADS_PER_LANE_TILE):
                qm = jnp.where(upper if hh else ~upper, qp, jnp.zeros_like(qp))
                s = lax.dot_general(qm, kp, (((1,), (1,)), ((), ())),
                                    preferred_element_type=F32) + bias
                m = jnp.max(s, axis=-1, keepdims=True)
                p = jnp.exp(s - m)
                l = jnp.sum(p, axis=-1, keepdims=True)
                o = jnp.dot(p.astype(BF16), vp, preferred_element_type=F32) * (1.0 / l)
                lse = jnp.broadcast_to(m + jnp.log(l), (BLK, LANES))
                if hh == 0:
                    o_pair, lse_pair = o, lse
                else:
                    o_pair = jnp.where(upper, o, o_pair)
                    lse_pair = jnp.where(upper, lse, lse_pair)
            o_ref[0, pl.ds(r0, BLK), lanes] = o_pair.astype(o_ref.dtype)
            lse_ref[0, pl.ds(r0, BLK), lanes] = lse_pair

    block(0, 0, BLK, bias0_ref[...])
    if n_blocks > 1:
        bias = bias_ref[...]

        def body(i, carry):
            r0 = pl.multiple_of(i * BLK, BLK)
            block(r0, pl.multiple_of(r0 - BLK, BLK), 2 * BLK, bias)
            return carry

        lax.fori_loop(1, n_blocks, body, 0)


def _band_biases(n_sub):
    qi = np.arange(BLK)[:, None]
    ki = np.arange(2 * BLK)[None, :]
    dist = BLK + qi - ki
    band = (dist >= 0) & (dist <= n_sub)
    d0 = qi - np.arange(BLK)[None, :]
    to_bias = lambda ok: np.where(ok, 0.0, NEG_BIG).astype(np.float32)
    return to_bias((d0 >= 0) & (d0 <= n_sub)), to_bias(band)


def _band_attention(q, k, v, window, dilation):
    batch, length, _ = q.shape
    bias0, bias = _band_biases(window // dilation)
    blk = pl.BlockSpec((1, length, D_ATTN), lambda b, c: (b, 0, c))
    return pl.pallas_call(
        functools.partial(_band_attn_kernel, n_blocks=length // BLK),
        grid=(batch, dilation),
        in_specs=[blk, blk, blk, _const_spec((BLK, BLK)), _const_spec((BLK, 2 * BLK))],
        out_specs=[blk, blk],
        out_shape=[jax.ShapeDtypeStruct(q.shape, BF16), jax.ShapeDtypeStruct(q.shape, F32)],
        compiler_params=_params(2),
        name=f"band_attn_d{dilation}",
    )(q, k, v, jnp.asarray(bias0), jnp.asarray(bias))


CONV_PAD = 32
CONV_CHUNK = 64


def _mix_kernel(u_ref, up_ref, w_ref, b_ref, lg_ref, lb_ref, *refs, tt, dilations):
    n_pat = 1 + len(dilations)
    ins = refs[0:2 * n_pat]
    mix_ref, ext_ref = refs[2 * n_pat:2 * n_pat + 2]
    nat = refs[2 * n_pat + 2:]
    t = pl.program_id(1)
    prev = up_ref[0]
    ext_ref[0:CONV_PAD, :] = jnp.where(t > 0, prev, jnp.zeros_like(prev))
    ext_ref[CONV_PAD:CONV_PAD + tt, :] = u_ref[0]

    for j, dil in enumerate(dilations):
        for i in range(2):
            src, dst = ins[2 + 2 * j + i], nat[2 * j + i]
            for c in range(dil):
                _put_rows(dst, c, dil, src[0, :, _class_lanes(c)].astype(F32))
    os_ = [ins[0][...].astype(F32)] + [_get_tiles(nat[2 * j]) for j in range(len(dilations))]
    ls_ = [ins[1][...]] + [_get_tiles(nat[2 * j + 1]) for j in range(len(dilations))]
    mx = functools.reduce(jnp.maximum, ls_)
    es = [jnp.exp(l - mx) for l in ls_]
    attn = sum(e * o for e, o in zip(es, os_)) * (1.0 / sum(es))
    mix_ref[:, 0:D_ATTN] = attn.astype(mix_ref.dtype)

    d_conv = u_ref.shape[-1]
    for ch in range(tt // CONV_CHUNK):
        row0 = ch * CONV_CHUNK + CONV_PAD - CONV_LEFT
        acc = None
        for j in range(CONV_WIDTH):
            term = ext_ref[pl.ds(row0 + j, CONV_CHUNK), :] * w_ref[j:j + 1, :]
            acc = term if acc is None else acc + term
        c = _ln_swish(acc + b_ref[...], lg_ref[...], lb_ref[...])
        mix_ref[ch * CONV_CHUNK:(ch + 1) * CONV_CHUNK, D_ATTN:D_ATTN + d_conv] = c.astype(mix_ref.dtype)


def _ln_swish(c, g, b):
    mu = jnp.mean(c, axis=-1, keepdims=True)
    cc = c - mu
    var = jnp.mean(cc * cc, axis=-1, keepdims=True)
    y = cc * lax.rsqrt(var + LN_EPS) * g + b
    return y * _sigmoid(y)


def _mix_prompt(u, conv_w, conv_b, ln_g, ln_b, attn, batch, seq, tt):
    d_conv = u.shape[-1]
    u3 = u.reshape(batch, seq, d_conv)
    n_t = seq // tt
    per = tt // CONV_PAD
    dilations = tuple(dil for _, dil in PATTERNS if dil > 1)
    cur = pl.BlockSpec((1, tt, d_conv), lambda b, t: (b, t, 0))
    prev = pl.BlockSpec((1, CONV_PAD, d_conv), lambda b, t: (b, jnp.maximum(t * per - 1, 0), 0))
    flat = pl.BlockSpec((tt, D_ATTN), lambda b, t: (b * n_t + t, 0))
    ins = [a.reshape(batch * seq, D_ATTN) for a in attn[1]]
    specs = [flat, flat]
    for dil in dilations:
        ins += list(attn[dil])
        specs += [pl.BlockSpec((1, tt // dil, dil * D_ATTN), lambda b, t: (b, t, 0))] * 2
    return pl.pallas_call(
        functools.partial(_mix_kernel, tt=tt, dilations=dilations),
        grid=(batch, n_t),
        in_specs=[cur, prev, _const_spec(conv_w.shape), _const_spec((1, d_conv)),
                  _const_spec((1, d_conv)), _const_spec((1, d_conv))] + specs,
        out_specs=pl.BlockSpec((tt, D_ATTN + d_conv), lambda b, t: (b * n_t + t, 0)),
        out_shape=jax.ShapeDtypeStruct((batch * seq, D_ATTN + d_conv), BF16),
        scratch_shapes=[pltpu.VMEM((CONV_PAD + tt, d_conv), F32)]
        + [_tiles_scratch(tt, D_ATTN)] * (2 * len(dilations)),
        compiler_params=_params(2),
        name="mix_prompt",
    )(u3, u3, conv_w, conv_b, ln_g, ln_b, *ins)


def _sample_conv_kernel(st_ref, u_ref, w_ref, b_ref, lg_ref, lb_ref, c_ref, ns_ref, us_ref, cs_ref,
                        *, nb, t_new):
    _put_tiles(us_ref, u_ref[...])
    new = [_get_rows(us_ref, t, nb, t_new) for t in range(t_new)]

    def ext(tau):
        return st_ref[tau] if tau < CONV_LEFT else new[tau - CONV_LEFT]

    for t in range(t_new):
        acc = None
        for j in range(CONV_WIDTH):
            term = ext(t + j) * w_ref[j:j + 1, :]
            acc = term if acc is None else acc + term
        c = _ln_swish(acc + b_ref[...], lg_ref[...], lb_ref[...])
        _put_rows(cs_ref, t, t_new, c)
    c_ref[...] = _get_tiles(cs_ref)
    for tau in range(CONV_LEFT):
        ns_ref[tau] = ext(tau + t_new)


def _sample_conv(state_t, u_s, conv_w, conv_b, ln_g, ln_b, layer, t_new, nb):
    _, n, d_conv = state_t.shape
    return pl.pallas_call(
        functools.partial(_sample_conv_kernel, nb=nb, t_new=t_new),
        grid=(n // nb,),
        in_specs=[pl.BlockSpec((CONV_LEFT, nb, d_conv), lambda i: (layer, i, 0)),
                  pl.BlockSpec((nb * t_new, d_conv), lambda i: (i, 0)),
                  _const_spec(conv_w.shape), _const_spec((1, d_conv)),
                  _const_spec((1, d_conv)), _const_spec((1, d_conv))],
        out_specs=[pl.BlockSpec((nb * t_new, d_conv), lambda i: (i, 0)),
                   pl.BlockSpec((CONV_LEFT, nb, d_conv), lambda i: (0, i, 0))],
        out_shape=[jax.ShapeDtypeStruct((n * t_new, d_conv), F32),
                   jax.ShapeDtypeStruct((CONV_LEFT, n, d_conv), F32)],
        scratch_shapes=[_tiles_scratch(nb * t_new, d_conv)] * 2,
        compiler_params=_params(1),
        name="sample_conv",
    )(state_t, u_s, conv_w, conv_b, ln_g, ln_b)


def _multiplicity(dist):
    dist = np.asarray(dist)
    c = np.zeros(dist.shape, np.float32)
    for window, dilation in PATTERNS:
        c += ((dist >= 0) & (dist <= window) & (dist % dilation == 0)).astype(np.float32)
    return c


def _shift_append(old, new_rows, out_ref, t_new):
    n_ch, l_buf = old.shape
    n_tiles = l_buf // LANES
    lane = lax.broadcasted_iota(jnp.int32, (n_ch, LANES), 1)
    keep = lane < LANES - t_new
    pad = jnp.concatenate([jnp.zeros((LANES - t_new, n_ch), F32), new_rows], axis=0)
    nxt = pad.T
    for j in reversed(range(n_tiles)):
        cur = pltpu.roll(old[:, j * LANES:(j + 1) * LANES], LANES - t_new, axis=1)
        out_ref[0, :, j * LANES:(j + 1) * LANES] = jnp.where(keep, cur, nxt)
        nxt = cur


def _sample_attn_kernel(q_ref, kn_ref, vn_ref, ck_ref, cv_ref, cnt_ref, attn_ref, ok_ref, ov_ref,
                        *, t_new, l_buf):
    rows = N_HEADS * t_new
    ck = ck_ref[0]
    cv = cv_ref[0]
    kn = kn_ref[0]
    vn = vn_ref[0]
    _shift_append(ck, kn, ok_ref, t_new)
    _shift_append(cv, vn, ov_ref, t_new)

    row = lax.broadcasted_iota(jnp.int32, (rows, D_ATTN), 0)
    lane = lax.broadcasted_iota(jnp.int32, (rows, D_ATTN), 1)
    own_head = (row // t_new) == (lane // HEAD_DIM)
    q_rep = jnp.concatenate([q_ref[0]] * N_HEADS, axis=0)
    q_bd = jnp.where(own_head, q_rep, 0.0)

    cnt = cnt_ref[...]
    s_c = jnp.dot(q_bd.astype(BF16), ck.astype(BF16), preferred_element_type=F32)
    s_c = jnp.where(cnt > 0.0, s_c, NEG_BIG)
    m = jnp.max(s_c, axis=-1, keepdims=True)

    t_of_row = lax.broadcasted_iota(jnp.int32, (rows, 1), 0) % t_new
    s_n, c_n = [], []
    for tp in range(t_new):
        d = t_of_row - tp
        c = jnp.zeros((rows, 1), F32)
        for _, dilation in PATTERNS:
            c = c + jnp.where((d >= 0) & (d % dilation == 0), 1.0, 0.0)
        s = jnp.sum(q_bd * kn[tp:tp + 1, :], axis=-1, keepdims=True)
        s = jnp.where(c > 0.0, s, NEG_BIG)
        m = jnp.maximum(m, s)
        s_n.append(s)
        c_n.append(c)

    p_c = cnt * jnp.exp(s_c - m)
    l = jnp.sum(p_c, axis=-1, keepdims=True)
    acc = lax.dot_general(p_c.astype(BF16), cv.astype(BF16), (((1,), (1,)), ((), ())),
                          preferred_element_type=F32)
    for tp in range(t_new):
        p = c_n[tp] * jnp.exp(s_n[tp] - m)
        l = l + p
        acc = acc + p * vn[tp:tp + 1, :]
    acc = jnp.where(own_head, acc * (1.0 / l), 0.0)
    out = acc[0:t_new, :]
    for h in range(1, N_HEADS):
        out = out + acc[h * t_new:(h + 1) * t_new, :]
    attn_ref[0] = out.astype(attn_ref.dtype)


def _sample_attention(q_s, k_s, v_s, cache_kt, cache_vt, layer):
    n, t_new, _ = q_s.shape
    l_buf = cache_kt.shape[2]
    assert l_buf == WIN_MAX and t_new % 8 == 0
    dist = l_buf + np.arange(t_new)[:, None] - np.arange(l_buf)[None, :]
    cnt = np.tile(_multiplicity(dist), (N_HEADS, 1))
    new = pl.BlockSpec((1, t_new, D_ATTN), lambda i: (i, 0, 0))
    big = pl.BlockSpec((1, D_ATTN, l_buf), lambda i: (i, 0, 0))
    big_in = pl.BlockSpec((1, D_ATTN, l_buf), lambda i: (layer * n + i, 0, 0))
    return pl.pallas_call(
        functools.partial(_sample_attn_kernel, t_new=t_new, l_buf=l_buf),
        grid=(n,),
        in_specs=[new, new, new, big_in, big_in, _const_spec(cnt.shape)],
        out_specs=[new, big, big],
        out_shape=[jax.ShapeDtypeStruct((n, t_new, D_ATTN), F32),
                   jax.ShapeDtypeStruct((n, D_ATTN, l_buf), F32),
                   jax.ShapeDtypeStruct((n, D_ATTN, l_buf), F32)],
        compiler_params=_params(1),
        name="sample_attn",
    )(q_s, k_s, v_s, cache_kt, cache_vt, jnp.asarray(cnt))


def _ffn_kernel(x_ref, *refs, n_mix):
    mix_refs = refs[0:n_mix]
    wo_ref, g_ref, wg_ref, wu_ref, wd_ref, y_ref = refs[n_mix:]
    x1 = x_ref[...]
    row = 0
    for mr in mix_refs:
        width = mr.shape[1]
        x1 = x1 + jnp.dot(mr[...].astype(BF16), wo_ref[row:row + width, :], preferred_element_type=F32)
        row += width
    h = _rms_rows(x1, g_ref[...]).astype(BF16)
    gate = jnp.dot(h, wg_ref[...], preferred_element_type=F32)
    up = jnp.dot(h, wu_ref[...], preferred_element_type=F32)
    act = (gate * _sigmoid(gate) * up).astype(BF16)
    y_ref[...] = x1 + jnp.dot(act, wd_ref[...], preferred_element_type=F32)


def _outproj_ffn(x2d, mix_parts, wo_b, g, wg_b, wu_b, wd_b, tm):
    m, d = x2d.shape
    tm = min(tm, m)
    d_ff = wg_b.shape[1]
    row = lambda i: (i, 0)
    return pl.pallas_call(
        functools.partial(_ffn_kernel, n_mix=len(mix_parts)),
        grid=(m // tm,),
        in_specs=[pl.BlockSpec((tm, d), row)] + [pl.BlockSpec((tm, p.shape[1]), row) for p in mix_parts]
        + [_const_spec(wo_b.shape), _const_spec((1, d)),
           _const_spec((d, d_ff)), _const_spec((d, d_ff)), _const_spec((d_ff, d))],
        out_specs=pl.BlockSpec((tm, d), row),
        out_shape=jax.ShapeDtypeStruct((m, d), F32),
        compiler_params=_params(1),
        name="outproj_ffn",
    )(x2d, *mix_parts, wo_b, g, wg_b, wu_b, wd_b)


TOKEN_TILE = 512
MIX_TIME_TILE = 256
SAMPLE_CONV_BATCH = 32


def kernel(x_prompt, x_sample, cache_k, cache_v, state_conv, attn_norm_g, w_in, q_norm_g, k_norm_g,
           conv_w, conv_b, conv_ln_g, conv_ln_b, w_out, ffn_norm_g, w_gate, w_up, w_down):
    batch, seq, d_model = x_prompt.shape
    n_dec, t_new, _ = x_sample.shape
    depth = w_in.shape[0]
    d_conv = conv_w.shape[-1]
    l_buf = cache_k.shape[2]
    assert seq == WIN_MAX and l_buf == WIN_MAX and seq % (BLK * PATTERNS[-1][1]) == 0
    assert w_in.shape[-1] == 3 * D_ATTN + 2 * d_conv and d_conv == D_ATTN

    seg = jnp.asarray(np.kron(np.eye(N_HEADS), np.ones((HEAD_DIM, HEAD_DIM))), BF16)
    row = lambda v: v.reshape(1, -1)
    to_t = lambda c: jnp.transpose(c, (0, 1, 3, 4, 2)).reshape(depth * n_dec, D_ATTN, l_buf)
    from_t = lambda c, n: jnp.transpose(c.reshape(n, N_HEADS, HEAD_DIM, -1), (0, 3, 1, 2))
    cache_kt, cache_vt = to_t(cache_k), to_t(cache_v)
    state_t = jnp.transpose(state_conv, (0, 2, 1, 3)).reshape(depth * CONV_LEFT, n_dec, d_conv)

    y_p = x_prompt.reshape(batch * seq, d_model)
    y_s = x_sample.reshape(n_dec * t_new, d_model)
    outs = [[] for _ in range(6)]
    for l in range(depth):
        w_in_b = w_in[l].astype(BF16)
        wo_b, wg_b, wu_b, wd_b = (w[l].astype(BF16) for w in (w_out, w_gate, w_up, w_down))
        g_attn, g_ffn = row(attn_norm_g[l]), row(ffn_norm_g[l])
        qg = row(jnp.tile(q_norm_g[l], N_HEADS))
        kg = row(jnp.tile(k_norm_g[l], N_HEADS))
        cb, lg, lb = row(conv_b[l]), row(conv_ln_g[l]), row(conv_ln_b[l])

        q, k, v, u = _inproj_sample(y_s, g_attn, w_in_b, qg, kg, seg, TOKEN_TILE)
        as3 = lambda t: t.reshape(n_dec, t_new, D_ATTN)
        attn_s, nk, nv = _sample_attention(as3(q), as3(k), as3(v), cache_kt, cache_vt, l)
        c_s, ns = _sample_conv(state_t, u, conv_w[l], cb, lg, lb, l, t_new, min(SAMPLE_CONV_BATCH, n_dec))
        y_s = _outproj_ffn(y_s, [attn_s.reshape(n_dec * t_new, D_ATTN), c_s],
                           wo_b, g_ffn, wg_b, wu_b, wd_b, TOKEN_TILE)
        outs[3].append(from_t(nk, n_dec))
        outs[4].append(from_t(nv, n_dec))
        outs[5].append(jnp.transpose(ns, (1, 0, 2)))

        qkv, kt, vt, u = _inproj_prompt(y_p, g_attn, w_in_b, qg, kg, seg, batch, seq, TOKEN_TILE)
        attn = {dil: _band_attention(*qkv[dil], w, dil) for w, dil in PATTERNS}
        mix = _mix_prompt(u, conv_w[l], cb, lg, lb, attn, batch, seq, MIX_TIME_TILE)
        y_p = _outproj_ffn(y_p, [mix], wo_b, g_ffn, wg_b, wu_b, wd_b, TOKEN_TILE)
        outs[0].append(from_t(kt, batch))
        outs[1].append(from_t(vt, batch))
        outs[2].append(u.reshape(batch, seq, d_conv)[:, seq - CONV_LEFT:])

    stack = lambda xs: xs[0][None] if len(xs) == 1 else jnp.stack(xs)
    return (y_p.reshape(batch, seq, d_model), y_s.reshape(n_dec, t_new, d_model)) + tuple(
        stack(o) for o in outs)
```

```python
import functools

import numpy as np
import jax
import jax.numpy as jnp
from jax import lax
from jax.experimental import pallas as pl
from jax.experimental.pallas import tpu as pltpu

N_HEADS = 8
HEAD_DIM = 64
D_ATTN = N_HEADS * HEAD_DIM
PATTERNS = ((128, 1), (512, 4), (2048, 16))
WIN_MAX = max(w for w, _ in PATTERNS)
CONV_WIDTH = 31
CONV_LEFT = CONV_WIDTH - 1
BLK = 128
RMS_EPS = 1e-6
LN_EPS = 1e-5
ATTN_SCALE = HEAD_DIM ** -0.5
LOG2E = 1.4426950408889634
NEG_BIG = -1e30

LANES = 128
HEADS_PER_LANE_TILE = LANES // HEAD_DIM
VMEM_LIMIT_BYTES = 56 * 1024 * 1024

F32 = jnp.float32
BF16 = jnp.bfloat16


def _params(n_axes):
    return pltpu.CompilerParams(dimension_semantics=("arbitrary",) * n_axes,
                                vmem_limit_bytes=VMEM_LIMIT_BYTES)


def _const_spec(shape):
    return pl.BlockSpec(shape, lambda *_: (0,) * len(shape), pipeline_mode=pl.Buffered(1))


def _sigmoid(x):
    return 1.0 / (1.0 + jnp.exp(-x))


def _rms_rows(x, g):
    ms = jnp.mean(x * x, axis=-1, keepdims=True)
    return x * lax.rsqrt(ms + RMS_EPS) * g


def _class_lanes(c):
    return slice(c * D_ATTN, (c + 1) * D_ATTN)


def _tiles_scratch(rows, width):
    return pltpu.VMEM((width // LANES, rows, LANES), F32)


def _put_tiles(scr, x):
    for lt in range(scr.shape[0]):
        scr[lt] = x[:, lt * LANES:(lt + 1) * LANES]


def _get_tiles(scr):
    return jnp.concatenate([scr[lt] for lt in range(scr.shape[0])], axis=1)


def _get_rows(scr, start, n, stride):
    return jnp.concatenate([scr[lt, pl.ds(start, n, stride=stride), :] for lt in range(scr.shape[0])],
                           axis=1)


def _put_rows(scr, start, stride, x):
    for lt in range(scr.shape[0]):
        scr[lt, pl.ds(start, x.shape[0], stride=stride), :] = x[:, lt * LANES:(lt + 1) * LANES]


def _inproj_kernel(x_ref, g_ref, w_ref, qg_ref, kg_ref, seg_ref, *refs, dilations):
    h = _rms_rows(x_ref[...], g_ref[...]).astype(BF16)

    def proj(col):
        return jnp.dot(h, w_ref[:, col * D_ATTN:(col + 1) * D_ATTN], preferred_element_type=F32)

    def head_rms(t, gain):
        ss = jnp.dot((t * t).astype(BF16), seg_ref[...], preferred_element_type=F32)
        return t * lax.rsqrt(ss * (1.0 / HEAD_DIM) + RMS_EPS) * gain

    q = head_rms(proj(0), qg_ref[...]) * (ATTN_SCALE * LOG2E)
    k = head_rms(proj(1), kg_ref[...])
    v = proj(2)
    u = proj(3) * _sigmoid(proj(4))
    if not dilations:
        q_ref, k_ref, v_ref, u_ref = refs
        q_ref[...] = q
        k_ref[...] = k
        v_ref[...] = v
        u_ref[...] = u
        return

    n_views = 3 * len(dilations)
    nat = refs[0:3]
    views = refs[3:3 + n_views]
    kt_ref, vt_ref, u_ref = refs[3 + n_views:6 + n_views]
    scratch = refs[6 + n_views:]
    tm = x_ref.shape[0]
    u_ref[...] = u
    kt_ref[0] = k.T
    vt_ref[0] = v.T
    for i, t in enumerate((q, k, v)):
        nat[i][...] = t.astype(BF16)
        _put_tiles(scratch[i], t)
        for j, dil in enumerate(dilations):
            for c in range(dil):
                views[3 * j + i][0, :, _class_lanes(c)] = (
                    _get_rows(scratch[i], c, tm // dil, dil).astype(BF16))


def _inproj_sample(x2d, g, w_in_b, qg, kg, seg, tm):
    m, d = x2d.shape
    tm = min(tm, m)
    row = lambda i: (i, 0)
    out_f = jax.ShapeDtypeStruct((m, D_ATTN), F32)
    tile = pl.BlockSpec((tm, D_ATTN), row)
    return pl.pallas_call(
        functools.partial(_inproj_kernel, dilations=()),
        grid=(m // tm,),
        in_specs=[pl.BlockSpec((tm, d), row), _const_spec((1, d)), _const_spec(w_in_b.shape),
                  _const_spec((1, D_ATTN)), _const_spec((1, D_ATTN)), _const_spec((D_ATTN, D_ATTN))],
        out_specs=[tile] * 4,
        out_shape=[out_f] * 4,
        compiler_params=_params(1),
        name="inproj_sample",
    )(x2d, g, w_in_b, qg, kg, seg)


def _inproj_prompt(x2d, g, w_in_b, qg, kg, seg, batch, seq, tm):
    m, d = x2d.shape
    n_t = seq // tm
    dilations = tuple(dil for _, dil in PATTERNS if dil > 1)
    flat = lambda b, t: (b * n_t + t, 0)
    nat_spec = pl.BlockSpec((tm, D_ATTN), flat)
    nat_b = jax.ShapeDtypeStruct((m, D_ATTN), BF16)
    view_specs, view_shapes = [], []
    for dil in dilations:
        view_specs += [pl.BlockSpec((1, tm // dil, dil * D_ATTN), lambda b, t: (b, t, 0))] * 3
        view_shapes += [jax.ShapeDtypeStruct((batch, seq // dil, dil * D_ATTN), BF16)] * 3
    t_spec = pl.BlockSpec((1, D_ATTN, tm), lambda b, t: (b, 0, t))
    t_shape = jax.ShapeDtypeStruct((batch, D_ATTN, seq), F32)
    res = pl.pallas_call(
        functools.partial(_inproj_kernel, dilations=dilations),
        grid=(batch, n_t),
        in_specs=[pl.BlockSpec((tm, d), flat), _const_spec((1, d)), _const_spec(w_in_b.shape),
                  _const_spec((1, D_ATTN)), _const_spec((1, D_ATTN)), _const_spec((D_ATTN, D_ATTN))],
        out_specs=[nat_spec] * 3 + view_specs + [t_spec, t_spec, nat_spec],
        out_shape=[nat_b] * 3 + view_shapes + [t_shape, t_shape, jax.ShapeDtypeStruct((m, D_ATTN), F32)],
        scratch_shapes=[_tiles_scratch(tm, D_ATTN)] * 3,
        compiler_params=_params(2),
        name="inproj_prompt",
    )(x2d, g, w_in_b, qg, kg, seg)
    n_views = 3 * len(dilations)
    qkv = {1: tuple(t.reshape(batch, seq, D_ATTN) for t in res[0:3])}
    for j, dil in enumerate(dilations):
        qkv[dil] = tuple(res[3 + 3 * j:6 + 3 * j])
    kt, vt, u = res[3 + n_views:]
    return qkv, kt, vt, u


def _band_attn_kernel(q_ref, k_ref, v_ref, bias0_ref, bias_ref, o_ref, lse_ref, *, n_blocks, n_classes):
    lane = lax.broadcasted_iota(jnp.int32, (BLK, LANES), 1)
    upper = lane >= HEAD_DIM
    n_pairs = N_HEADS // HEADS_PER_LANE_TILE

    def block(cl, r0, k0, n_keys, bias):
        vps, scores = [], []
        for hp in range(n_pairs):
            lanes = slice(cl * D_ATTN + hp * LANES, cl * D_ATTN + (hp + 1) * LANES)
            qp = q_ref[0, pl.ds(r0, BLK), lanes]
            kp = k_ref[0, pl.ds(k0, n_keys), lanes]
            vps.append(v_ref[0, pl.ds(k0, n_keys), lanes])
            for hh in range(HEADS_PER_LANE_TILE):
                qm = jnp.where(upper if hh else ~upper, qp, jnp.zeros_like(qp))
                scores.append(lax.dot_general(qm, kp, (((1,), (1,)), ((), ())),
                                              preferred_element_type=F32) + bias)
        probs, inv_l = [], []
        lse_tile = jnp.zeros((BLK, LANES), F32)
        for h, s in enumerate(scores):
            m = jnp.max(s, axis=-1, keepdims=True)
            p = jnp.exp2(s - m)
            l = jnp.sum(p, axis=-1, keepdims=True)
            probs.append(p.astype(BF16))
            inv_l.append(1.0 / l)
            lse_tile = jnp.where(lane == h, m + jnp.log(l) * LOG2E, lse_tile)
        lse_ref[0, pl.ds(r0, BLK), cl * LANES:(cl + 1) * LANES] = lse_tile
        for hp in range(n_pairs):
            lanes = slice(cl * D_ATTN + hp * LANES, cl * D_ATTN + (hp + 1) * LANES)
            o_lo, o_hi = (jnp.dot(probs[2 * hp + hh], vps[hp], preferred_element_type=F32)
                          * inv_l[2 * hp + hh] for hh in range(HEADS_PER_LANE_TILE))
            o_ref[0, pl.ds(r0, BLK), lanes] = jnp.where(upper, o_hi, o_lo).astype(o_ref.dtype)

    for cl in range(n_classes):
        block(cl, 0, 0, BLK, bias0_ref[...])
        if n_blocks > 1:
            def body(i, carry, cl=cl):
                r0 = pl.multiple_of(i * BLK, BLK)
                block(cl, r0, pl.multiple_of(r0 - BLK, BLK), 2 * BLK, bias_ref[...])
                return carry

            lax.fori_loop(1, n_blocks, body, 0)


def _band_biases(n_sub):
    qi = np.arange(BLK)[:, None]
    ki = np.arange(2 * BLK)[None, :]
    dist = BLK + qi - ki
    band = (dist >= 0) & (dist <= n_sub)
    d0 = qi - np.arange(BLK)[None, :]
    to_bias = lambda ok: np.where(ok, 0.0, NEG_BIG).astype(np.float32)
    return to_bias((d0 >= 0) & (d0 <= n_sub)), to_bias(band)


BAND_CLASSES_PER_STEP = 4


def _band_attention(q, k, v, window, dilation):
    batch, length, _ = q.shape
    cps = min(BAND_CLASSES_PER_STEP, dilation)
    bias0, bias = _band_biases(window // dilation)
    blk = pl.BlockSpec((1, length, cps * D_ATTN), lambda b, c: (b, 0, c))
    return pl.pallas_call(
        functools.partial(_band_attn_kernel, n_blocks=length // BLK, n_classes=cps),
        grid=(batch, dilation // cps),
        in_specs=[blk, blk, blk, _const_spec((BLK, BLK)), _const_spec((BLK, 2 * BLK))],
        out_specs=[blk, pl.BlockSpec((1, length, cps * LANES), lambda b, c: (b, 0, c))],
        out_shape=[jax.ShapeDtypeStruct(q.shape, BF16),
                   jax.ShapeDtypeStruct((batch, length, dilation * LANES), F32)],
        compiler_params=_params(2),
        name=f"band_attn_d{dilation}",
    )(q, k, v, jnp.asarray(bias0), jnp.asarray(bias))


CONV_PAD = 32
CONV_CHUNK = 64
SUBLANES = 8


def _mix_kernel(u_ref, up_ref, w_ref, b_ref, lg_ref, lb_ref, expand_ref, *refs, tt, dilations):
    n_pat = 1 + len(dilations)
    ins = refs[0:2 * n_pat]
    mix_ref, ext_ref = refs[2 * n_pat:2 * n_pat + 2]
    nat = refs[2 * n_pat + 2:]
    t = pl.program_id(1)
    rows = CONV_PAD + tt
    prev = up_ref[0]
    ext_ref[0, 0:CONV_PAD, :] = jnp.where(t > 0, prev, jnp.zeros_like(prev))
    ext_ref[0, CONV_PAD:rows, :] = u_ref[0]
    for s in range(1, SUBLANES):
        ext_ref[s, 0:rows - SUBLANES, :] = ext_ref[0, pl.ds(s, rows - SUBLANES), :]

    for j, dil in enumerate(dilations):
        for i, width in enumerate((D_ATTN, LANES)):
            src, dst = ins[2 + 2 * j + i], nat[2 * j + i]
            for c in range(dil):
                _put_rows(dst, c, dil, src[0, :, c * width:(c + 1) * width].astype(F32))
    os_ = [ins[0][...].astype(F32)] + [_get_tiles(nat[2 * j]) for j in range(len(dilations))]
    ls_ = [ins[1][...]] + [_get_tiles(nat[2 * j + 1]) for j in range(len(dilations))]
    mx = functools.reduce(jnp.maximum, ls_)
    es = [jnp.exp2(l - mx) for l in ls_]
    inv = 1.0 / sum(es)
    attn = None
    for e, o in zip(es, os_):
        w = e * inv
        hi = w.astype(BF16)
        lo = (w - hi.astype(F32)).astype(BF16)
        wide = (jnp.dot(hi, expand_ref[...], preferred_element_type=F32)
                + jnp.dot(lo, expand_ref[...], preferred_element_type=F32))
        attn = wide * o if attn is None else attn + wide * o
    mix_ref[:, 0:D_ATTN] = attn.astype(mix_ref.dtype)

    d_conv = u_ref.shape[-1]
    for ch in range(tt // CONV_CHUNK):
        acc = None
        for j in range(CONV_WIDTH):
            off = CONV_PAD - CONV_LEFT + j
            s = off % SUBLANES
            term = ext_ref[s, pl.ds(ch * CONV_CHUNK + off - s, CONV_CHUNK), :] * w_ref[j:j + 1, :]
            acc = term if acc is None else acc + term
        c = _ln_swish(acc + b_ref[...], lg_ref[...], lb_ref[...])
        mix_ref[ch * CONV_CHUNK:(ch + 1) * CONV_CHUNK, D_ATTN:D_ATTN + d_conv] = c.astype(mix_ref.dtype)


def _ln_swish(c, g, b):
    mu = jnp.mean(c, axis=-1, keepdims=True)
    cc = c - mu
    var = jnp.mean(cc * cc, axis=-1, keepdims=True)
    y = cc * lax.rsqrt(var + LN_EPS) * g + b
    return y * _sigmoid(y)


def _mix_prompt(u, conv_w, conv_b, ln_g, ln_b, attn, batch, seq, tt):
    d_conv = u.shape[-1]
    u3 = u.reshape(batch, seq, d_conv)
    n_t = seq // tt
    per = tt // CONV_PAD
    dilations = tuple(dil for _, dil in PATTERNS if dil > 1)
    cur = pl.BlockSpec((1, tt, d_conv), lambda b, t: (b, t, 0))
    prev = pl.BlockSpec((1, CONV_PAD, d_conv), lambda b, t: (b, jnp.maximum(t * per - 1, 0), 0))
    flat = lambda width: pl.BlockSpec((tt, width), lambda b, t: (b * n_t + t, 0))
    view = lambda dil, width: pl.BlockSpec((1, tt // dil, dil * width), lambda b, t: (b, t, 0))
    ins = [attn[1][0].reshape(batch * seq, D_ATTN), attn[1][1].reshape(batch * seq, LANES)]
    specs = [flat(D_ATTN), flat(LANES)]
    scratch = [pltpu.VMEM((SUBLANES, CONV_PAD + tt, d_conv), F32)]
    for dil in dilations:
        ins += list(attn[dil])
        specs += [view(dil, D_ATTN), view(dil, LANES)]
        scratch += [_tiles_scratch(tt, D_ATTN), _tiles_scratch(tt, LANES)]
    expand = np.zeros((LANES, D_ATTN), np.float32)
    expand[:N_HEADS] = np.kron(np.eye(N_HEADS), np.ones((1, HEAD_DIM)))
    return pl.pallas_call(
        functools.partial(_mix_kernel, tt=tt, dilations=dilations),
        grid=(batch, n_t),
        in_specs=[cur, prev, _const_spec(conv_w.shape), _const_spec((1, d_conv)),
                  _const_spec((1, d_conv)), _const_spec((1, d_conv)),
                  _const_spec((LANES, D_ATTN))] + specs,
        out_specs=pl.BlockSpec((tt, D_ATTN + d_conv), lambda b, t: (b * n_t + t, 0)),
        out_shape=jax.ShapeDtypeStruct((batch * seq, D_ATTN + d_conv), BF16),
        scratch_shapes=scratch,
        compiler_params=_params(2),
        name="mix_prompt",
    )(u3, u3, conv_w, conv_b, ln_g, ln_b, jnp.asarray(expand, BF16), *ins)


def _sample_conv_kernel(st_ref, u_ref, w_ref, b_ref, lg_ref, lb_ref, c_ref, ns_ref, us_ref, cs_ref,
                        *, nb, t_new):
    _put_tiles(us_ref, u_ref[...])
    new = [_get_rows(us_ref, t, nb, t_new) for t in range(t_new)]

    def ext(tau):
        return st_ref[tau] if tau < CONV_LEFT else new[tau - CONV_LEFT]

    for t in range(t_new):
        acc = None
        for j in range(CONV_WIDTH):
            term = ext(t + j) * w_ref[j:j + 1, :]
            acc = term if acc is None else acc + term
        c = _ln_swish(acc + b_ref[...], lg_ref[...], lb_ref[...])
        _put_rows(cs_ref, t, t_new, c)
    c_ref[...] = _get_tiles(cs_ref)
    for tau in range(CONV_LEFT):
        ns_ref[tau] = ext(tau + t_new)


def _sample_conv(state_t, u_s, conv_w, conv_b, ln_g, ln_b, layer, t_new, nb):
    _, n, d_conv = state_t.shape
    return pl.pallas_call(
        functools.partial(_sample_conv_kernel, nb=nb, t_new=t_new),
        grid=(n // nb,),
        in_specs=[pl.BlockSpec((CONV_LEFT, nb, d_conv), lambda i: (layer, i, 0)),
                  pl.BlockSpec((nb * t_new, d_conv), lambda i: (i, 0)),
                  _const_spec(conv_w.shape), _const_spec((1, d_conv)),
                  _const_spec((1, d_conv)), _const_spec((1, d_conv))],
        out_specs=[pl.BlockSpec((nb * t_new, d_conv), lambda i: (i, 0)),
                   pl.BlockSpec((CONV_LEFT, nb, d_conv), lambda i: (0, i, 0))],
        out_shape=[jax.ShapeDtypeStruct((n * t_new, d_conv), F32),
                   jax.ShapeDtypeStruct((CONV_LEFT, n, d_conv), F32)],
        scratch_shapes=[_tiles_scratch(nb * t_new, d_conv)] * 2,
        compiler_params=_params(1),
        name="sample_conv",
    )(state_t, u_s, conv_w, conv_b, ln_g, ln_b)


def _multiplicity(dist):
    dist = np.asarray(dist)
    c = np.zeros(dist.shape, np.float32)
    for window, dilation in PATTERNS:
        c += ((dist >= 0) & (dist <= window) & (dist % dilation == 0)).astype(np.float32)
    return c


def _shift_append(old, new_rows, out_ref, t_new):
    n_ch, l_buf = old.shape
    n_tiles = l_buf // LANES
    lane = lax.broadcasted_iota(jnp.int32, (n_ch, LANES), 1)
    keep = lane < LANES - t_new
    pad = jnp.concatenate([jnp.zeros((LANES - t_new, n_ch), F32), new_rows], axis=0)
    nxt = pad.T
    for j in reversed(range(n_tiles)):
        cur = pltpu.roll(old[:, j * LANES:(j + 1) * LANES], LANES - t_new, axis=1)
        out_ref[0, :, j * LANES:(j + 1) * LANES] = jnp.where(keep, cur, nxt)
        nxt = cur


def _sample_attn_kernel(q_ref, kn_ref, vn_ref, ck_ref, cv_ref, cnt_ref, attn_ref, ok_ref, ov_ref,
                        *, t_new, l_buf):
    rows = N_HEADS * t_new
    ck = ck_ref[0]
    cv = cv_ref[0]
    kn = kn_ref[0]
    vn = vn_ref[0]
    _shift_append(ck, kn, ok_ref, t_new)
    _shift_append(cv, vn, ov_ref, t_new)

    row = lax.broadcasted_iota(jnp.int32, (rows, D_ATTN), 0)
    lane = lax.broadcasted_iota(jnp.int32, (rows, D_ATTN), 1)
    own_head = (row // t_new) == (lane // HEAD_DIM)
    q_rep = jnp.concatenate([q_ref[0]] * N_HEADS, axis=0)
    q_bd = jnp.where(own_head, q_rep, 0.0)

    cnt = cnt_ref[...]
    s_c = jnp.dot(q_bd.astype(BF16), ck.astype(BF16), preferred_element_type=F32)
    s_c = jnp.where(cnt > 0.0, s_c, NEG_BIG)
    m = jnp.max(s_c, axis=-1, keepdims=True)

    t_of_row = lax.broadcasted_iota(jnp.int32, (rows, 1), 0) % t_new
    s_n, c_n = [], []
    for tp in range(t_new):
        d = t_of_row - tp
        c = jnp.zeros((rows, 1), F32)
        for _, dilation in PATTERNS:
            c = c + jnp.where((d >= 0) & (d % dilation == 0), 1.0, 0.0)
        s = jnp.sum(q_bd * kn[tp:tp + 1, :], axis=-1, keepdims=True)
        s = jnp.where(c > 0.0, s, NEG_BIG)
        m = jnp.maximum(m, s)
        s_n.append(s)
        c_n.append(c)

    p_c = cnt * jnp.exp2(s_c - m)
    l = jnp.sum(p_c, axis=-1, keepdims=True)
    acc = lax.dot_general(p_c.astype(BF16), cv.astype(BF16), (((1,), (1,)), ((), ())),
                          preferred_element_type=F32)
    for tp in range(t_new):
        p = c_n[tp] * jnp.exp2(s_n[tp] - m)
        l = l + p
        acc = acc + p * vn[tp:tp + 1, :]
    acc = jnp.where(own_head, acc * (1.0 / l), 0.0)
    out = acc[0:t_new, :]
    for h in range(1, N_HEADS):
        out = out + acc[h * t_new:(h + 1) * t_new, :]
    attn_ref[0] = out.astype(attn_ref.dtype)


def _sample_attention(q_s, k_s, v_s, cache_kt, cache_vt, layer):
    n, t_new, _ = q_s.shape
    l_buf = cache_kt.shape[2]
    assert l_buf == WIN_MAX and t_new % 8 == 0
    dist = l_buf + np.arange(t_new)[:, None] - np.arange(l_buf)[None, :]
    cnt = np.tile(_multiplicity(dist), (N_HEADS, 1))
    new = pl.BlockSpec((1, t_new, D_ATTN), lambda i: (i, 0, 0))
    big = pl.BlockSpec((1, D_ATTN, l_buf), lambda i: (i, 0, 0))
    big_in = pl.BlockSpec((1, D_ATTN, l_buf), lambda i: (layer * n + i, 0, 0))
    return pl.pallas_call(
        functools.partial(_sample_attn_kernel, t_new=t_new, l_buf=l_buf),
        grid=(n,),
        in_specs=[new, new, new, big_in, big_in, _const_spec(cnt.shape)],
        out_specs=[new, big, big],
        out_shape=[jax.ShapeDtypeStruct((n, t_new, D_ATTN), F32),
                   jax.ShapeDtypeStruct((n, D_ATTN, l_buf), F32),
                   jax.ShapeDtypeStruct((n, D_ATTN, l_buf), F32)],
        compiler_params=_params(1),
        name="sample_attn",
    )(q_s, k_s, v_s, cache_kt, cache_vt, jnp.asarray(cnt))


def _ffn_kernel(x_ref, *refs, n_mix):
    mix_refs = refs[0:n_mix]
    wo_ref, g_ref, wg_ref, wu_ref, wd_ref, y_ref = refs[n_mix:]
    x1 = x_ref[...]
    row = 0
    for mr in mix_refs:
        width = mr.shape[1]
        x1 = x1 + jnp.dot(mr[...].astype(BF16), wo_ref[row:row + width, :], preferred_element_type=F32)
        row += width
    h = _rms_rows(x1, g_ref[...]).astype(BF16)
    gate = jnp.dot(h, wg_ref[...], preferred_element_type=F32)
    up = jnp.dot(h, wu_ref[...], preferred_element_type=F32)
    act = (gate * _sigmoid(gate) * up).astype(BF16)
    y_ref[...] = x1 + jnp.dot(act, wd_ref[...], preferred_element_type=F32)


def _outproj_ffn(x2d, mix_parts, wo_b, g, wg_b, wu_b, wd_b, tm):
    m, d = x2d.shape
    tm = min(tm, m)
    d_ff = wg_b.shape[1]
    row = lambda i: (i, 0)
    return pl.pallas_call(
        functools.partial(_ffn_kernel, n_mix=len(mix_parts)),
        grid=(m // tm,),
        in_specs=[pl.BlockSpec((tm, d), row)] + [pl.BlockSpec((tm, p.shape[1]), row) for p in mix_parts]
        + [_const_spec(wo_b.shape), _const_spec((1, d)),
           _const_spec((d, d_ff)), _const_spec((d, d_ff)), _const_spec((d_ff, d))],
        out_specs=pl.BlockSpec((tm, d), row),
        out_shape=jax.ShapeDtypeStruct((m, d), F32),
        compiler_params=_params(1),
        name="outproj_ffn",
    )(x2d, *mix_parts, wo_b, g, wg_b, wu_b, wd_b)


TOKEN_TILE = 512
MIX_TIME_TILE = 256
SAMPLE_CONV_BATCH = 32


def kernel(x_prompt, x_sample, cache_k, cache_v, state_conv, attn_norm_g, w_in, q_norm_g, k_norm_g,
           conv_w, conv_b, conv_ln_g, conv_ln_b, w_out, ffn_norm_g, w_gate, w_up, w_down):
    batch, seq, d_model = x_prompt.shape
    n_dec, t_new, _ = x_sample.shape
    depth = w_in.shape[0]
    d_conv = conv_w.shape[-1]
    l_buf = cache_k.shape[2]
    assert seq == WIN_MAX and l_buf == WIN_MAX and seq % (BLK * PATTERNS[-1][1]) == 0
    assert w_in.shape[-1] == 3 * D_ATTN + 2 * d_conv and d_conv == D_ATTN

    seg = jnp.asarray(np.kron(np.eye(N_HEADS), np.ones((HEAD_DIM, HEAD_DIM))), BF16)
    row = lambda v: v.reshape(1, -1)
    to_t = lambda c: jnp.transpose(c, (0, 1, 3, 4, 2)).reshape(depth * n_dec, D_ATTN, l_buf)
    from_t = lambda c, n: jnp.transpose(c.reshape(n, N_HEADS, HEAD_DIM, -1), (0, 3, 1, 2))
    cache_kt, cache_vt = to_t(cache_k), to_t(cache_v)
    state_t = jnp.transpose(state_conv, (0, 2, 1, 3)).reshape(depth * CONV_LEFT, n_dec, d_conv)

    y_p = x_prompt.reshape(batch * seq, d_model)
    y_s = x_sample.reshape(n_dec * t_new, d_model)
    outs = [[] for _ in range(6)]
    for l in range(depth):
        w_in_b = w_in[l].astype(BF16)
        wo_b, wg_b, wu_b, wd_b = (w[l].astype(BF16) for w in (w_out, w_gate, w_up, w_down))
        g_attn, g_ffn = row(attn_norm_g[l]), row(ffn_norm_g[l])
        qg = row(jnp.tile(q_norm_g[l], N_HEADS))
        kg = row(jnp.tile(k_norm_g[l], N_HEADS))
        cb, lg, lb = row(conv_b[l]), row(conv_ln_g[l]), row(conv_ln_b[l])

        q, k, v, u = _inproj_sample(y_s, g_attn, w_in_b, qg, kg, seg, TOKEN_TILE)
        as3 = lambda t: t.reshape(n_dec, t_new, D_ATTN)
        attn_s, nk, nv = _sample_attention(as3(q), as3(k), as3(v), cache_kt, cache_vt, l)
        c_s, ns = _sample_conv(state_t, u, conv_w[l], cb, lg, lb, l, t_new, min(SAMPLE_CONV_BATCH, n_dec))
        y_s = _outproj_ffn(y_s, [attn_s.reshape(n_dec * t_new, D_ATTN), c_s],
                           wo_b, g_ffn, wg_b, wu_b, wd_b, TOKEN_TILE)
        outs[3].append(from_t(nk, n_dec))
        outs[4].append(from_t(nv, n_dec))
        outs[5].append(jnp.transpose(ns, (1, 0, 2)))

        qkv, kt, vt, u = _inproj_prompt(y_p, g_attn, w_in_b, qg, kg, seg, batch, seq, TOKEN_TILE)
        attn = {dil: _band_attention(*qkv[dil], w, dil) for w, dil in PATTERNS}
        mix = _mix_prompt(u, conv_w[l], cb, lg, lb, attn, batch, seq, MIX_TIME_TILE)
        y_p = _outproj_ffn(y_p, [mix], wo_b, g_ffn, wg_b, wu_b, wd_b, TOKEN_TILE)
        outs[0].append(from_t(kt, batch))
        outs[1].append(from_t(vt, batch))
        outs[2].append(u.reshape(batch, seq, d_conv)[:, seq - CONV_LEFT:])

    stack = lambda xs: xs[0][None] if len(xs) == 1 else jnp.stack(xs)
    return (y_p.reshape(batch, seq, d_model), y_s.reshape(n_dec, t_new, d_model)) + tuple(
        stack(o) for o in outs)
```

```python
import functools

import numpy as np
import jax
import jax.numpy as jnp
from jax import lax
from jax.experimental import pallas as pl
from jax.experimental.pallas import tpu as pltpu

N_HEADS = 8
HEAD_DIM = 64
D_ATTN = N_HEADS * HEAD_DIM
PATTERNS = ((128, 1), (512, 4), (2048, 16))
WIN_MAX = max(w for w, _ in PATTERNS)
CONV_WIDTH = 31
CONV_LEFT = CONV_WIDTH - 1
BLK = 128
RMS_EPS = 1e-6
LN_EPS = 1e-5
ATTN_SCALE = HEAD_DIM ** -0.5
LOG2E = 1.4426950408889634
NEG_BIG = -1e30

LANES = 128
HEADS_PER_LANE_TILE = LANES // HEAD_DIM
VMEM_LIMIT_BYTES = 56 * 1024 * 1024

F32 = jnp.float32
BF16 = jnp.bfloat16


def _params(n_axes):
    return pltpu.CompilerParams(dimension_semantics=("arbitrary",) * n_axes,
                                vmem_limit_bytes=VMEM_LIMIT_BYTES)


def _const_spec(shape):
    return pl.BlockSpec(shape, lambda *_: (0,) * len(shape), pipeline_mode=pl.Buffered(1))


def _sigmoid(x):
    return 1.0 / (1.0 + jnp.exp(-x))


def _rms_rows(x, g):
    ms = jnp.mean(x * x, axis=-1, keepdims=True)
    return x * lax.rsqrt(ms + RMS_EPS) * g


def _class_lanes(c):
    return slice(c * D_ATTN, (c + 1) * D_ATTN)


def _tiles_scratch(rows, width):
    return pltpu.VMEM((width // LANES, rows, LANES), F32)


def _put_tiles(scr, x):
    for lt in range(scr.shape[0]):
        scr[lt] = x[:, lt * LANES:(lt + 1) * LANES]


def _get_tiles(scr):
    return jnp.concatenate([scr[lt] for lt in range(scr.shape[0])], axis=1)


def _get_rows(scr, start, n, stride):
    return jnp.concatenate([scr[lt, pl.ds(start, n, stride=stride), :] for lt in range(scr.shape[0])],
                           axis=1)


def _put_rows(scr, start, stride, x):
    for lt in range(scr.shape[0]):
        scr[lt, pl.ds(start, x.shape[0], stride=stride), :] = x[:, lt * LANES:(lt + 1) * LANES]


def _inproj_kernel(x_ref, g_ref, w_ref, qg_ref, kg_ref, seg_ref, *refs, dilations):
    h = _rms_rows(x_ref[...], g_ref[...]).astype(BF16)

    def proj(col):
        return jnp.dot(h, w_ref[:, col * D_ATTN:(col + 1) * D_ATTN], preferred_element_type=F32)

    def head_rms(t, gain):
        ss = jnp.dot((t * t).astype(BF16), seg_ref[...], preferred_element_type=F32)
        return t * lax.rsqrt(ss * (1.0 / HEAD_DIM) + RMS_EPS) * gain

    q = head_rms(proj(0), qg_ref[...]) * (ATTN_SCALE * LOG2E)
    k = head_rms(proj(1), kg_ref[...])
    v = proj(2)
    u = proj(3) * _sigmoid(proj(4))
    if not dilations:
        q_ref, k_ref, v_ref, u_ref = refs
        q_ref[...] = q
        k_ref[...] = k
        v_ref[...] = v
        u_ref[...] = u
        return

    n_views = 3 * len(dilations)
    nat = refs[0:3]
    views = refs[3:3 + n_views]
    kt_ref, vt_ref, u_ref = refs[3 + n_views:6 + n_views]
    scratch = refs[6 + n_views:]
    tm = x_ref.shape[0]
    u_ref[...] = u
    kt_ref[0] = k.T
    vt_ref[0] = v.T
    for i, t in enumerate((q, k, v)):
        nat[i][...] = t.astype(BF16)
        _put_tiles(scratch[i], t)
        for j, dil in enumerate(dilations):
            for c in range(dil):
                views[3 * j + i][0, :, _class_lanes(c)] = (
                    _get_rows(scratch[i], c, tm // dil, dil).astype(BF16))


def _inproj_sample(x2d, g, w_in_b, qg, kg, seg, tm):
    m, d = x2d.shape
    tm = min(tm, m)
    row = lambda i: (i, 0)
    out_f = jax.ShapeDtypeStruct((m, D_ATTN), F32)
    tile = pl.BlockSpec((tm, D_ATTN), row)
    return pl.pallas_call(
        functools.partial(_inproj_kernel, dilations=()),
        grid=(m // tm,),
        in_specs=[pl.BlockSpec((tm, d), row), _const_spec((1, d)), _const_spec(w_in_b.shape),
                  _const_spec((1, D_ATTN)), _const_spec((1, D_ATTN)), _const_spec((D_ATTN, D_ATTN))],
        out_specs=[tile] * 4,
        out_shape=[out_f] * 4,
        compiler_params=_params(1),
        name="inproj_sample",
    )(x2d, g, w_in_b, qg, kg, seg)


def _inproj_prompt(x2d, g, w_in_b, qg, kg, seg, batch, seq, tm):
    m, d = x2d.shape
    n_t = seq // tm
    dilations = tuple(dil for _, dil in PATTERNS if dil > 1)
    flat = lambda b, t: (b * n_t + t, 0)
    nat_spec = pl.BlockSpec((tm, D_ATTN), flat)
    nat_b = jax.ShapeDtypeStruct((m, D_ATTN), BF16)
    view_specs, view_shapes = [], []
    for dil in dilations:
        view_specs += [pl.BlockSpec((1, tm // dil, dil * D_ATTN), lambda b, t: (b, t, 0))] * 3
        view_shapes += [jax.ShapeDtypeStruct((batch, seq // dil, dil * D_ATTN), BF16)] * 3
    t_spec = pl.BlockSpec((1, D_ATTN, tm), lambda b, t: (b, 0, t))
    t_shape = jax.ShapeDtypeStruct((batch, D_ATTN, seq), F32)
    res = pl.pallas_call(
        functools.partial(_inproj_kernel, dilations=dilations),
        grid=(batch, n_t),
        in_specs=[pl.BlockSpec((tm, d), flat), _const_spec((1, d)), _const_spec(w_in_b.shape),
                  _const_spec((1, D_ATTN)), _const_spec((1, D_ATTN)), _const_spec((D_ATTN, D_ATTN))],
        out_specs=[nat_spec] * 3 + view_specs + [t_spec, t_spec, nat_spec],
        out_shape=[nat_b] * 3 + view_shapes + [t_shape, t_shape, jax.ShapeDtypeStruct((m, D_ATTN), F32)],
        scratch_shapes=[_tiles_scratch(tm, D_ATTN)] * 3,
        compiler_params=_params(2),
        name="inproj_prompt",
    )(x2d, g, w_in_b, qg, kg, seg)
    n_views = 3 * len(dilations)
    qkv = {1: tuple(t.reshape(batch, seq, D_ATTN) for t in res[0:3])}
    for j, dil in enumerate(dilations):
        qkv[dil] = tuple(res[3 + 3 * j:6 + 3 * j])
    kt, vt, u = res[3 + n_views:]
    return qkv, kt, vt, u


def _band_attn_kernel(q_ref, k_ref, v_ref, bias0_ref, bias_ref, o_ref, lse_ref, *, n_blocks, n_classes):
    lane = lax.broadcasted_iota(jnp.int32, (BLK, LANES), 1)
    upper = lane >= HEAD_DIM
    n_pairs = N_HEADS // HEADS_PER_LANE_TILE

    def block(cl, r0, k0, n_keys, bias):
        vps, scores = [], []
        for hp in range(n_pairs):
            lanes = slice(cl * D_ATTN + hp * LANES, cl * D_ATTN + (hp + 1) * LANES)
            qp = q_ref[0, pl.ds(r0, BLK), lanes]
            kp = k_ref[0, pl.ds(k0, n_keys), lanes]
            vps.append(v_ref[0, pl.ds(k0, n_keys), lanes])
            for hh in range(HEADS_PER_LANE_TILE):
                qm = jnp.where(upper if hh else ~upper, qp, jnp.zeros_like(qp))
                scores.append(lax.dot_general(qm, kp, (((1,), (1,)), ((), ())),
                                              preferred_element_type=F32) + bias)
        probs, inv_l = [], []
        lse_tile = jnp.zeros((BLK, LANES), F32)
        for h, s in enumerate(scores):
            m = jnp.max(s, axis=-1, keepdims=True)
            p = jnp.exp2(s - m)
            l = jnp.sum(p, axis=-1, keepdims=True)
            probs.append(p.astype(BF16))
            inv_l.append(1.0 / l)
            lse_tile = jnp.where(lane == h, m + jnp.log(l) * LOG2E, lse_tile)
        lse_ref[0, pl.ds(r0, BLK), cl * LANES:(cl + 1) * LANES] = lse_tile
        for hp in range(n_pairs):
            lanes = slice(cl * D_ATTN + hp * LANES, cl * D_ATTN + (hp + 1) * LANES)
            o_lo, o_hi = (jnp.dot(probs[2 * hp + hh], vps[hp], preferred_element_type=F32)
                          * inv_l[2 * hp + hh] for hh in range(HEADS_PER_LANE_TILE))
            o_ref[0, pl.ds(r0, BLK), lanes] = jnp.where(upper, o_hi, o_lo).astype(o_ref.dtype)

    for cl in range(n_classes):
        block(cl, 0, 0, BLK, bias0_ref[...])
        if n_blocks > 1:
            def body(i, carry, cl=cl):
                r0 = pl.multiple_of(i * BLK, BLK)
                block(cl, r0, pl.multiple_of(r0 - BLK, BLK), 2 * BLK, bias_ref[...])
                return carry

            lax.fori_loop(1, n_blocks, body, 0)


def _band_biases(n_sub):
    qi = np.arange(BLK)[:, None]
    ki = np.arange(2 * BLK)[None, :]
    dist = BLK + qi - ki
    band = (dist >= 0) & (dist <= n_sub)
    d0 = qi - np.arange(BLK)[None, :]
    to_bias = lambda ok: np.where(ok, 0.0, NEG_BIG).astype(np.float32)
    return to_bias((d0 >= 0) & (d0 <= n_sub)), to_bias(band)


BAND_CLASSES_PER_STEP = 4


def _band_attention(q, k, v, window, dilation):
    batch, length, _ = q.shape
    cps = min(BAND_CLASSES_PER_STEP, dilation)
    bias0, bias = _band_biases(window // dilation)
    blk = pl.BlockSpec((1, length, cps * D_ATTN), lambda b, c: (b, 0, c))
    return pl.pallas_call(
        functools.partial(_band_attn_kernel, n_blocks=length // BLK, n_classes=cps),
        grid=(batch, dilation // cps),
        in_specs=[blk, blk, blk, _const_spec((BLK, BLK)), _const_spec((BLK, 2 * BLK))],
        out_specs=[blk, pl.BlockSpec((1, length, cps * LANES), lambda b, c: (b, 0, c))],
        out_shape=[jax.ShapeDtypeStruct(q.shape, BF16),
                   jax.ShapeDtypeStruct((batch, length, dilation * LANES), F32)],
        compiler_params=_params(2),
        name=f"band_attn_d{dilation}",
    )(q, k, v, jnp.asarray(bias0), jnp.asarray(bias))


CONV_PAD = 32
CONV_CHUNK = 64
SUBLANES = 8


def _mix_kernel(u_ref, up_ref, w_ref, b_ref, lg_ref, lb_ref, expand_ref, *refs, tt, dilations):
    n_pat = 1 + len(dilations)
    ins = refs[0:2 * n_pat]
    mix_ref, ext_ref = refs[2 * n_pat:2 * n_pat + 2]
    nat = refs[2 * n_pat + 2:]
    t = pl.program_id(1)
    rows = CONV_PAD + tt
    prev = up_ref[0]
    ext_ref[0, 0:CONV_PAD, :] = jnp.where(t > 0, prev, jnp.zeros_like(prev))
    ext_ref[0, CONV_PAD:rows, :] = u_ref[0]
    for s in range(1, SUBLANES):
        ext_ref[s, 0:rows - SUBLANES, :] = ext_ref[0, pl.ds(s, rows - SUBLANES), :]

    for j, dil in enumerate(dilations):
        for i, width in enumerate((D_ATTN, LANES)):
            src, dst = ins[2 + 2 * j + i], nat[2 * j + i]
            for c in range(dil):
                _put_rows(dst, c, dil, src[0, :, c * width:(c + 1) * width].astype(F32))
    os_ = [ins[0][...].astype(F32)] + [_get_tiles(nat[2 * j]) for j in range(len(dilations))]
    ls_ = [ins[1][...]] + [_get_tiles(nat[2 * j + 1]) for j in range(len(dilations))]
    mx = functools.reduce(jnp.maximum, ls_)
    es = [jnp.exp2(l - mx) for l in ls_]
    inv = 1.0 / sum(es)
    attn = None
    for e, o in zip(es, os_):
        w = e * inv
        hi = w.astype(BF16)
        lo = (w - hi.astype(F32)).astype(BF16)
        wide = (jnp.dot(hi, expand_ref[...], preferred_element_type=F32)
                + jnp.dot(lo, expand_ref[...], preferred_element_type=F32))
        attn = wide * o if attn is None else attn + wide * o
    mix_ref[:, 0:D_ATTN] = attn.astype(mix_ref.dtype)

    d_conv = u_ref.shape[-1]
    for ch in range(tt // CONV_CHUNK):
        acc = None
        for j in range(CONV_WIDTH):
            off = CONV_PAD - CONV_LEFT + j
            s = off % SUBLANES
            term = ext_ref[s, pl.ds(ch * CONV_CHUNK + off - s, CONV_CHUNK), :] * w_ref[j:j + 1, :]
            acc = term if acc is None else acc + term
        c = _ln_swish(acc + b_ref[...], lg_ref[...], lb_ref[...])
        mix_ref[ch * CONV_CHUNK:(ch + 1) * CONV_CHUNK, D_ATTN:D_ATTN + d_conv] = c.astype(mix_ref.dtype)


def _ln_swish(c, g, b):
    mu = jnp.mean(c, axis=-1, keepdims=True)
    cc = c - mu
    var = jnp.mean(cc * cc, axis=-1, keepdims=True)
    y = cc * lax.rsqrt(var + LN_EPS) * g + b
    return y * _sigmoid(y)


def _mix_prompt(u, conv_w, conv_b, ln_g, ln_b, attn, batch, seq, tt):
    d_conv = u.shape[-1]
    u3 = u.reshape(batch, seq, d_conv)
    n_t = seq // tt
    per = tt // CONV_PAD
    dilations = tuple(dil for _, dil in PATTERNS if dil > 1)
    cur = pl.BlockSpec((1, tt, d_conv), lambda b, t: (b, t, 0))
    prev = pl.BlockSpec((1, CONV_PAD, d_conv), lambda b, t: (b, jnp.maximum(t * per - 1, 0), 0))
    flat = lambda width: pl.BlockSpec((tt, width), lambda b, t: (b * n_t + t, 0))
    view = lambda dil, width: pl.BlockSpec((1, tt // dil, dil * width), lambda b, t: (b, t, 0))
    ins = [attn[1][0].reshape(batch * seq, D_ATTN), attn[1][1].reshape(batch * seq, LANES)]
    specs = [flat(D_ATTN), flat(LANES)]
    scratch = [pltpu.VMEM((SUBLANES, CONV_PAD + tt, d_conv), F32)]
    for dil in dilations:
        ins += list(attn[dil])
        specs += [view(dil, D_ATTN), view(dil, LANES)]
        scratch += [_tiles_scratch(tt, D_ATTN), _tiles_scratch(tt, LANES)]
    expand = np.zeros((LANES, D_ATTN), np.float32)
    expand[:N_HEADS] = np.kron(np.eye(N_HEADS), np.ones((1, HEAD_DIM)))
    return pl.pallas_call(
        functools.partial(_mix_kernel, tt=tt, dilations=dilations),
        grid=(batch, n_t),
        in_specs=[cur, prev, _const_spec(conv_w.shape), _const_spec((1, d_conv)),
                  _const_spec((1, d_conv)), _const_spec((1, d_conv)),
                  _const_spec((LANES, D_ATTN))] + specs,
        out_specs=pl.BlockSpec((tt, D_ATTN + d_conv), lambda b, t: (b * n_t + t, 0)),
        out_shape=jax.ShapeDtypeStruct((batch * seq, D_ATTN + d_conv), BF16),
        scratch_shapes=scratch,
        compiler_params=_params(2),
        name="mix_prompt",
    )(u3, u3, conv_w, conv_b, ln_g, ln_b, jnp.asarray(expand, BF16), *ins)


def _sample_conv_kernel(st_ref, u_ref, w_ref, b_ref, lg_ref, lb_ref, c_ref, ns_ref, us_ref, cs_ref,
                        *, nb, t_new):
    _put_tiles(us_ref, u_ref[...])
    new = [_get_rows(us_ref, t, nb, t_new) for t in range(t_new)]

    def ext(tau):
        return st_ref[tau] if tau < CONV_LEFT else new[tau - CONV_LEFT]

    for t in range(t_new):
        acc = None
        for j in range(CONV_WIDTH):
            term = ext(t + j) * w_ref[j:j + 1, :]
            acc = term if acc is None else acc + term
        c = _ln_swish(acc + b_ref[...], lg_ref[...], lb_ref[...])
        _put_rows(cs_ref, t, t_new, c)
    c_ref[...] = _get_tiles(cs_ref)
    for tau in range(CONV_LEFT):
        ns_ref[tau] = ext(tau + t_new)


def _sample_conv(state_t, u_s, conv_w, conv_b, ln_g, ln_b, layer, t_new, nb):
    _, n, d_conv = state_t.shape
    return pl.pallas_call(
        functools.partial(_sample_conv_kernel, nb=nb, t_new=t_new),
        grid=(n // nb,),
        in_specs=[pl.BlockSpec((CONV_LEFT, nb, d_conv), lambda i: (layer, i, 0)),
                  pl.BlockSpec((nb * t_new, d_conv), lambda i: (i, 0)),
                  _const_spec(conv_w.shape), _const_spec((1, d_conv)),
                  _const_spec((1, d_conv)), _const_spec((1, d_conv))],
        out_specs=[pl.BlockSpec((nb * t_new, d_conv), lambda i: (i, 0)),
                   pl.BlockSpec((CONV_LEFT, nb, d_conv), lambda i: (0, i, 0))],
        out_shape=[jax.ShapeDtypeStruct((n * t_new, d_conv), F32),
                   jax.ShapeDtypeStruct((CONV_LEFT, n, d_conv), F32)],
        scratch_shapes=[_tiles_scratch(nb * t_new, d_conv)] * 2,
        compiler_params=_params(1),
        name="sample_conv",
    )(state_t, u_s, conv_w, conv_b, ln_g, ln_b)


def _multiplicity(dist):
    dist = np.asarray(dist)
    c = np.zeros(dist.shape, np.float32)
    for window, dilation in PATTERNS:
        c += ((dist >= 0) & (dist <= window) & (dist % dilation == 0)).astype(np.float32)
    return c


def _shift_append(old, new_rows, out_ref, t_new):
    n_ch, l_buf = old.shape
    n_tiles = l_buf // LANES
    lane = lax.broadcasted_iota(jnp.int32, (n_ch, LANES), 1)
    keep = lane < LANES - t_new
    pad = jnp.concatenate([jnp.zeros((LANES - t_new, n_ch), F32), new_rows], axis=0)
    nxt = pad.T
    for j in reversed(range(n_tiles)):
        cur = pltpu.roll(old[:, j * LANES:(j + 1) * LANES], LANES - t_new, axis=1)
        out_ref[0, :, j * LANES:(j + 1) * LANES] = jnp.where(keep, cur, nxt)
        nxt = cur


CACHE_HEADS_PER_UNIT = 4


def _sample_unit(q_ref, kn_ref, vn_ref, ck_ref, cv_ref, cnt_ref, attn_ref, ok_ref, ov_ref, t_new):
    width = ck_ref.shape[1]
    n_heads = width // HEAD_DIM
    rows = n_heads * t_new
    ck = ck_ref[0]
    cv = cv_ref[0]
    kn = kn_ref[0]
    vn = vn_ref[0]
    _shift_append(ck, kn, ok_ref, t_new)
    _shift_append(cv, vn, ov_ref, t_new)

    row = lax.broadcasted_iota(jnp.int32, (rows, width), 0)
    lane = lax.broadcasted_iota(jnp.int32, (rows, width), 1)
    own_head = (row // t_new) == (lane // HEAD_DIM)
    q_rep = jnp.concatenate([q_ref[0]] * n_heads, axis=0)
    q_bd = jnp.where(own_head, q_rep, 0.0)

    cnt = cnt_ref[...]
    s_c = jnp.dot(q_bd.astype(BF16), ck.astype(BF16), preferred_element_type=F32)
    s_c = jnp.where(cnt > 0.0, s_c, NEG_BIG)
    m = jnp.max(s_c, axis=-1, keepdims=True)

    t_of_row = lax.broadcasted_iota(jnp.int32, (rows, 1), 0) % t_new
    s_n, c_n = [], []
    for tp in range(t_new):
        d = t_of_row - tp
        c = jnp.zeros((rows, 1), F32)
        for _, dilation in PATTERNS:
            c = c + jnp.where((d >= 0) & (d % dilation == 0), 1.0, 0.0)
        s = jnp.sum(q_bd * kn[tp:tp + 1, :], axis=-1, keepdims=True)
        s = jnp.where(c > 0.0, s, NEG_BIG)
        m = jnp.maximum(m, s)
        s_n.append(s)
        c_n.append(c)

    p_c = cnt * jnp.exp2(s_c - m)
    l = jnp.sum(p_c, axis=-1, keepdims=True)
    acc = lax.dot_general(p_c.astype(BF16), cv.astype(BF16), (((1,), (1,)), ((), ())),
                          preferred_element_type=F32)
    for tp in range(t_new):
        p = c_n[tp] * jnp.exp2(s_n[tp] - m)
        l = l + p
        acc = acc + p * vn[tp:tp + 1, :]
    acc = jnp.where(own_head, acc * (1.0 / l), 0.0)
    out = acc[0:t_new, :]
    for h in range(1, n_heads):
        out = out + acc[h * t_new:(h + 1) * t_new, :]
    attn_ref[0] = out.astype(attn_ref.dtype)


def _ffn_cache_kernel(x_ref, mix_ref, wo_ref, g_ref, wg_ref, wu_ref, wd_ref,
                      q_ref, kn_ref, vn_ref, ck_ref, cv_ref, cnt_ref,
                      y_ref, attn_ref, ok_ref, ov_ref, h_ref, *, n_chunks, t_new):
    c = pl.program_id(0) % n_chunks

    @pl.when(c == 0)
    def _():
        x1 = x_ref[...] + jnp.dot(mix_ref[...], wo_ref[...], preferred_element_type=F32)
        y_ref[...] = x1
        h_ref[...] = _rms_rows(x1, g_ref[...]).astype(BF16)

    h = h_ref[...]
    gate = jnp.dot(h, wg_ref[c], preferred_element_type=F32)
    up = jnp.dot(h, wu_ref[c], preferred_element_type=F32)
    act = (gate * _sigmoid(gate) * up).astype(BF16)
    y_ref[...] += jnp.dot(act, wd_ref[c], preferred_element_type=F32)
    _sample_unit(q_ref, kn_ref, vn_ref, ck_ref, cv_ref, cnt_ref, attn_ref, ok_ref, ov_ref, t_new)


def _ffn_with_cache(x2d, mix, wo_b, g, wg, wu, wd, q_s, k_s, v_s, cache_kt, cache_vt, layer, tm):
    m, d = x2d.shape
    n, t_new, _ = q_s.shape
    l_buf = cache_kt.shape[2]
    assert l_buf == WIN_MAX and t_new % SUBLANES == 0
    uw = CACHE_HEADS_PER_UNIT * HEAD_DIM
    per_n = D_ATTN // uw
    n_tiles, n_units = m // tm, n * per_n
    assert n_units % n_tiles == 0
    n_chunks = n_units // n_tiles
    d_ff = wg.shape[1]
    cw = -(-d_ff // (n_chunks * LANES)) * LANES
    pad = n_chunks * cw - d_ff
    by_cols = lambda w: jnp.pad(w, ((0, 0), (0, pad))).reshape(d, n_chunks, cw).transpose(1, 0, 2).astype(BF16)
    wg3, wu3 = by_cols(wg), by_cols(wu)
    wd3 = jnp.pad(wd, ((0, pad), (0, 0))).reshape(n_chunks, cw, d).astype(BF16)
    dist = l_buf + np.arange(t_new)[:, None] - np.arange(l_buf)[None, :]
    cnt = np.tile(_multiplicity(dist), (CACHE_HEADS_PER_UNIT, 1))

    tile = lambda width: pl.BlockSpec((tm, width), lambda u: (u // n_chunks, 0))
    new = pl.BlockSpec((1, t_new, uw), lambda u: (u // per_n, 0, u % per_n))
    big = pl.BlockSpec((1, uw, l_buf), lambda u: (u // per_n, u % per_n, 0))
    big_in = pl.BlockSpec((1, uw, l_buf), lambda u: (layer * n + u // per_n, u % per_n, 0))
    return pl.pallas_call(
        functools.partial(_ffn_cache_kernel, n_chunks=n_chunks, t_new=t_new),
        grid=(n_units,),
        in_specs=[tile(d), tile(mix.shape[1]), _const_spec(wo_b.shape), _const_spec((1, d)),
                  _const_spec(wg3.shape), _const_spec(wu3.shape), _const_spec(wd3.shape),
                  new, new, new, big_in, big_in, _const_spec(cnt.shape)],
        out_specs=[tile(d), new, big, big],
        out_shape=[jax.ShapeDtypeStruct((m, d), F32),
                   jax.ShapeDtypeStruct((n, t_new, D_ATTN), F32),
                   jax.ShapeDtypeStruct((n, D_ATTN, l_buf), F32),
                   jax.ShapeDtypeStruct((n, D_ATTN, l_buf), F32)],
        scratch_shapes=[pltpu.VMEM((tm, d), BF16)],
        compiler_params=_params(1),
        name="ffn_cache",
    )(x2d, mix, wo_b, g, wg3, wu3, wd3, q_s, k_s, v_s, cache_kt, cache_vt, jnp.asarray(cnt))


def _ffn_kernel(x_ref, *refs, n_mix):
    mix_refs = refs[0:n_mix]
    wo_ref, g_ref, wg_ref, wu_ref, wd_ref, y_ref = refs[n_mix:]
    x1 = x_ref[...]
    row = 0
    for mr in mix_refs:
        width = mr.shape[1]
        x1 = x1 + jnp.dot(mr[...].astype(BF16), wo_ref[row:row + width, :], preferred_element_type=F32)
        row += width
    h = _rms_rows(x1, g_ref[...]).astype(BF16)
    gate = jnp.dot(h, wg_ref[...], preferred_element_type=F32)
    up = jnp.dot(h, wu_ref[...], preferred_element_type=F32)
    act = (gate * _sigmoid(gate) * up).astype(BF16)
    y_ref[...] = x1 + jnp.dot(act, wd_ref[...], preferred_element_type=F32)


def _outproj_ffn(x2d, mix_parts, wo_b, g, wg_b, wu_b, wd_b, tm):
    m, d = x2d.shape
    tm = min(tm, m)
    d_ff = wg_b.shape[1]
    row = lambda i: (i, 0)
    return pl.pallas_call(
        functools.partial(_ffn_kernel, n_mix=len(mix_parts)),
        grid=(m // tm,),
        in_specs=[pl.BlockSpec((tm, d), row)] + [pl.BlockSpec((tm, p.shape[1]), row) for p in mix_parts]
        + [_const_spec(wo_b.shape), _const_spec((1, d)),
           _const_spec((d, d_ff)), _const_spec((d, d_ff)), _const_spec((d_ff, d))],
        out_specs=pl.BlockSpec((tm, d), row),
        out_shape=jax.ShapeDtypeStruct((m, d), F32),
        compiler_params=_params(1),
        name="outproj_ffn",
    )(x2d, *mix_parts, wo_b, g, wg_b, wu_b, wd_b)


TOKEN_TILE = 512
MIX_TIME_TILE = 256
SAMPLE_CONV_BATCH = 32


def kernel(x_prompt, x_sample, cache_k, cache_v, state_conv, attn_norm_g, w_in, q_norm_g, k_norm_g,
           conv_w, conv_b, conv_ln_g, conv_ln_b, w_out, ffn_norm_g, w_gate, w_up, w_down):
    batch, seq, d_model = x_prompt.shape
    n_dec, t_new, _ = x_sample.shape
    depth = w_in.shape[0]
    d_conv = conv_w.shape[-1]
    l_buf = cache_k.shape[2]
    assert seq == WIN_MAX and l_buf == WIN_MAX and seq % (BLK * PATTERNS[-1][1]) == 0
    assert w_in.shape[-1] == 3 * D_ATTN + 2 * d_conv and d_conv == D_ATTN

    seg = jnp.asarray(np.kron(np.eye(N_HEADS), np.ones((HEAD_DIM, HEAD_DIM))), BF16)
    row = lambda v: v.reshape(1, -1)
    to_t = lambda c: jnp.transpose(c, (0, 1, 3, 4, 2)).reshape(depth * n_dec, D_ATTN, l_buf)
    from_t = lambda c, n: jnp.transpose(c.reshape(n, N_HEADS, HEAD_DIM, -1), (0, 3, 1, 2))
    cache_kt, cache_vt = to_t(cache_k), to_t(cache_v)
    state_t = jnp.transpose(state_conv, (0, 2, 1, 3)).reshape(depth * CONV_LEFT, n_dec, d_conv)

    y_p = x_prompt.reshape(batch * seq, d_model)
    y_s = x_sample.reshape(n_dec * t_new, d_model)
    outs = [[] for _ in range(6)]
    for l in range(depth):
        w_in_b = w_in[l].astype(BF16)
        wo_b, wg_b, wu_b, wd_b = (w[l].astype(BF16) for w in (w_out, w_gate, w_up, w_down))
        g_attn, g_ffn = row(attn_norm_g[l]), row(ffn_norm_g[l])
        qg = row(jnp.tile(q_norm_g[l], N_HEADS))
        kg = row(jnp.tile(k_norm_g[l], N_HEADS))
        cb, lg, lb = row(conv_b[l]), row(conv_ln_g[l]), row(conv_ln_b[l])

        q, k, v, u_s = _inproj_sample(y_s, g_attn, w_in_b, qg, kg, seg, TOKEN_TILE)
        c_s, ns = _sample_conv(state_t, u_s, conv_w[l], cb, lg, lb, l, t_new, min(SAMPLE_CONV_BATCH, n_dec))
        qkv, kt, vt, u = _inproj_prompt(y_p, g_attn, w_in_b, qg, kg, seg, batch, seq, TOKEN_TILE)
        attn = {dil: _band_attention(*qkv[dil], w, dil) for w, dil in PATTERNS}
        mix = _mix_prompt(u, conv_w[l], cb, lg, lb, attn, batch, seq, MIX_TIME_TILE)

        as3 = lambda t: t.reshape(n_dec, t_new, D_ATTN)
        y_p, attn_s, nk, nv = _ffn_with_cache(y_p, mix, wo_b, g_ffn, w_gate[l], w_up[l], w_down[l],
                                              as3(q), as3(k), as3(v), cache_kt, cache_vt, l, TOKEN_TILE)
        y_s = _outproj_ffn(y_s, [attn_s.reshape(n_dec * t_new, D_ATTN), c_s],
                           wo_b, g_ffn, wg_b, wu_b, wd_b, TOKEN_TILE)
        outs[0].append(from_t(kt, batch))
        outs[1].append(from_t(vt, batch))
        outs[2].append(u.reshape(batch, seq, d_conv)[:, seq - CONV_LEFT:])
        outs[3].append(from_t(nk, n_dec))
        outs[4].append(from_t(nv, n_dec))
        outs[5].append(jnp.transpose(ns, (1, 0, 2)))

    stack = lambda xs: xs[0][None] if len(xs) == 1 else jnp.stack(xs)
    return (y_p.reshape(batch, seq, d_model), y_s.reshape(n_dec, t_new, d_model)) + tuple(
        stack(o) for o in outs)
```

```python
import functools

import numpy as np
import jax
import jax.numpy as jnp
from jax import lax
from jax.experimental import pallas as pl
from jax.experimental.pallas import tpu as pltpu

N_HEADS = 8
HEAD_DIM = 64
D_ATTN = N_HEADS * HEAD_DIM
PATTERNS = ((128, 1), (512, 4), (2048, 16))
WIN_MAX = max(w for w, _ in PATTERNS)
CONV_WIDTH = 31
CONV_LEFT = CONV_WIDTH - 1
BLK = 128
RMS_EPS = 1e-6
LN_EPS = 1e-5
ATTN_SCALE = HEAD_DIM ** -0.5
LOG2E = 1.4426950408889634
NEG_BIG = -1e30

LANES = 128
MXU_COLS = 256
HEADS_PER_LANE_TILE = LANES // HEAD_DIM
VMEM_LIMIT_BYTES = 56 * 1024 * 1024

F32 = jnp.float32
BF16 = jnp.bfloat16


def _params(n_axes):
    return pltpu.CompilerParams(dimension_semantics=("arbitrary",) * n_axes,
                                vmem_limit_bytes=VMEM_LIMIT_BYTES)


def _const_spec(shape):
    return pl.BlockSpec(shape, lambda *_: (0,) * len(shape), pipeline_mode=pl.Buffered(1))


def _sigmoid(x):
    return 1.0 / (1.0 + jnp.exp(-x))


def _rms_rows(x, g):
    ms = jnp.mean(x * x, axis=-1, keepdims=True)
    return x * lax.rsqrt(ms + RMS_EPS) * g


def _class_lanes(c):
    return slice(c * D_ATTN, (c + 1) * D_ATTN)


def _tiles_scratch(rows, width):
    return pltpu.VMEM((width // LANES, rows, LANES), F32)


def _put_tiles(scr, x):
    for lt in range(scr.shape[0]):
        scr[lt] = x[:, lt * LANES:(lt + 1) * LANES]


def _get_tiles(scr):
    return jnp.concatenate([scr[lt] for lt in range(scr.shape[0])], axis=1)


def _get_rows(scr, start, n, stride):
    return jnp.concatenate([scr[lt, pl.ds(start, n, stride=stride), :] for lt in range(scr.shape[0])],
                           axis=1)


def _put_rows(scr, start, stride, x):
    for lt in range(scr.shape[0]):
        scr[lt, pl.ds(start, x.shape[0], stride=stride), :] = x[:, lt * LANES:(lt + 1) * LANES]


def _inproj_kernel(x_ref, g_ref, w_ref, qg_ref, kg_ref, seg_ref, *refs, dilations):
    h = _rms_rows(x_ref[...], g_ref[...]).astype(BF16)

    def proj(col):
        return jnp.dot(h, w_ref[:, col * D_ATTN:(col + 1) * D_ATTN], preferred_element_type=F32)

    def head_rms(t, gain):
        ss = jnp.dot((t * t).astype(BF16), seg_ref[...], preferred_element_type=F32)
        return t * lax.rsqrt(ss * (1.0 / HEAD_DIM) + RMS_EPS) * gain

    q = head_rms(proj(0), qg_ref[...]) * (ATTN_SCALE * LOG2E)
    k = head_rms(proj(1), kg_ref[...])
    v = proj(2)
    u = proj(3) * _sigmoid(proj(4))
    if not dilations:
        q_ref, k_ref, v_ref, u_ref = refs
        q_ref[...] = q
        k_ref[...] = k
        v_ref[...] = v
        u_ref[...] = u
        return

    n_views = 3 * len(dilations)
    nat = refs[0:3]
    views = refs[3:3 + n_views]
    kt_ref, vt_ref, u_ref = refs[3 + n_views:6 + n_views]
    scratch = refs[6 + n_views:]
    tm = x_ref.shape[0]
    u_ref[...] = u
    kt_ref[0] = k.T
    vt_ref[0] = v.T
    for i, t in enumerate((q, k, v)):
        nat[i][...] = t.astype(BF16)
        _put_tiles(scratch[i], t)
        for j, dil in enumerate(dilations):
            for c in range(dil):
                views[3 * j + i][0, :, _class_lanes(c)] = (
                    _get_rows(scratch[i], c, tm // dil, dil).astype(BF16))


def _inproj_sample(x2d, g, w_in_b, qg, kg, seg, tm):
    m, d = x2d.shape
    tm = min(tm, m)
    row = lambda i: (i, 0)
    out_f = jax.ShapeDtypeStruct((m, D_ATTN), F32)
    tile = pl.BlockSpec((tm, D_ATTN), row)
    return pl.pallas_call(
        functools.partial(_inproj_kernel, dilations=()),
        grid=(m // tm,),
        in_specs=[pl.BlockSpec((tm, d), row), _const_spec((1, d)), _const_spec(w_in_b.shape),
                  _const_spec((1, D_ATTN)), _const_spec((1, D_ATTN)), _const_spec((D_ATTN, D_ATTN))],
        out_specs=[tile] * 4,
        out_shape=[out_f] * 4,
        compiler_params=_params(1),
        name="inproj_sample",
    )(x2d, g, w_in_b, qg, kg, seg)


def _inproj_prompt(x2d, g, w_in_b, qg, kg, seg, batch, seq, tm):
    m, d = x2d.shape
    n_t = seq // tm
    dilations = tuple(dil for _, dil in PATTERNS if dil > 1)
    flat = lambda b, t: (b * n_t + t, 0)
    nat_spec = pl.BlockSpec((tm, D_ATTN), flat)
    nat_b = jax.ShapeDtypeStruct((m, D_ATTN), BF16)
    view_specs, view_shapes = [], []
    for dil in dilations:
        view_specs += [pl.BlockSpec((1, tm // dil, dil * D_ATTN), lambda b, t: (b, t, 0))] * 3
        view_shapes += [jax.ShapeDtypeStruct((batch, seq // dil, dil * D_ATTN), BF16)] * 3
    t_spec = pl.BlockSpec((1, D_ATTN, tm), lambda b, t: (b, 0, t))
    t_shape = jax.ShapeDtypeStruct((batch, D_ATTN, seq), F32)
    res = pl.pallas_call(
        functools.partial(_inproj_kernel, dilations=dilations),
        grid=(batch, n_t),
        in_specs=[pl.BlockSpec((tm, d), flat), _const_spec((1, d)), _const_spec(w_in_b.shape),
                  _const_spec((1, D_ATTN)), _const_spec((1, D_ATTN)), _const_spec((D_ATTN, D_ATTN))],
        out_specs=[nat_spec] * 3 + view_specs + [t_spec, t_spec, nat_spec],
        out_shape=[nat_b] * 3 + view_shapes + [t_shape, t_shape, jax.ShapeDtypeStruct((m, D_ATTN), F32)],
        scratch_shapes=[_tiles_scratch(tm, D_ATTN)] * 3,
        compiler_params=_params(2),
        name="inproj_prompt",
    )(x2d, g, w_in_b, qg, kg, seg)
    n_views = 3 * len(dilations)
    qkv = {1: tuple(t.reshape(batch, seq, D_ATTN) for t in res[0:3])}
    for j, dil in enumerate(dilations):
        qkv[dil] = tuple(res[3 + 3 * j:6 + 3 * j])
    kt, vt, u = res[3 + n_views:]
    return qkv, kt, vt, u


def _band_attn_kernel(q_ref, k_ref, v_ref, bias0_ref, bias_ref, o_ref, lse_ref, *, n_blocks, n_classes):
    lane = lax.broadcasted_iota(jnp.int32, (BLK, LANES), 1)
    upper = lane >= HEAD_DIM
    n_pairs = N_HEADS // HEADS_PER_LANE_TILE

    def block(cl, r0, k0, n_keys, bias):
        vps, scores = [], []
        for hp in range(n_pairs):
            lanes = slice(cl * D_ATTN + hp * LANES, cl * D_ATTN + (hp + 1) * LANES)
            qp = q_ref[0, pl.ds(r0, BLK), lanes]
            kp = k_ref[0, pl.ds(k0, n_keys), lanes]
            vps.append(v_ref[0, pl.ds(k0, n_keys), lanes])
            for hh in range(HEADS_PER_LANE_TILE):
                qm = jnp.where(upper if hh else ~upper, qp, jnp.zeros_like(qp))
                scores.append(lax.dot_general(qm, kp, (((1,), (1,)), ((), ())),
                                              preferred_element_type=F32) + bias)
        probs, inv_l = [], []
        lse_tile = jnp.zeros((BLK, LANES), F32)
        for h, s in enumerate(scores):
            m = jnp.max(s, axis=-1, keepdims=True)
            p = jnp.exp2(s - m)
            l = jnp.sum(p, axis=-1, keepdims=True)
            probs.append(p.astype(BF16))
            inv_l.append(1.0 / l)
            lse_tile = jnp.where(lane == h, m + jnp.log(l) * LOG2E, lse_tile)
        lse_ref[0, pl.ds(r0, BLK), cl * LANES:(cl + 1) * LANES] = lse_tile
        for hp in range(n_pairs):
            lanes = slice(cl * D_ATTN + hp * LANES, cl * D_ATTN + (hp + 1) * LANES)
            o_lo, o_hi = (jnp.dot(probs[2 * hp + hh], vps[hp], preferred_element_type=F32)
                          * inv_l[2 * hp + hh] for hh in range(HEADS_PER_LANE_TILE))
            o_ref[0, pl.ds(r0, BLK), lanes] = jnp.where(upper, o_hi, o_lo).astype(o_ref.dtype)

    for cl in range(n_classes):
        block(cl, 0, 0, BLK, bias0_ref[...])
        if n_blocks > 1:
            def body(i, carry, cl=cl):
                r0 = pl.multiple_of(i * BLK, BLK)
                block(cl, r0, pl.multiple_of(r0 - BLK, BLK), 2 * BLK, bias_ref[...])
                return carry

            lax.fori_loop(1, n_blocks, body, 0)


def _band_biases(n_sub):
    qi = np.arange(BLK)[:, None]
    ki = np.arange(2 * BLK)[None, :]
    dist = BLK + qi - ki
    band = (dist >= 0) & (dist <= n_sub)
    d0 = qi - np.arange(BLK)[None, :]
    to_bias = lambda ok: np.where(ok, 0.0, NEG_BIG).astype(np.float32)
    return to_bias((d0 >= 0) & (d0 <= n_sub)), to_bias(band)


BAND_CLASSES_PER_STEP = 4


def _band_attention(q, k, v, window, dilation):
    batch, length, _ = q.shape
    cps = min(BAND_CLASSES_PER_STEP, dilation)
    bias0, bias = _band_biases(window // dilation)
    blk = pl.BlockSpec((1, length, cps * D_ATTN), lambda b, c: (b, 0, c))
    return pl.pallas_call(
        functools.partial(_band_attn_kernel, n_blocks=length // BLK, n_classes=cps),
        grid=(batch, dilation // cps),
        in_specs=[blk, blk, blk, _const_spec((BLK, BLK)), _const_spec((BLK, 2 * BLK))],
        out_specs=[blk, pl.BlockSpec((1, length, cps * LANES), lambda b, c: (b, 0, c))],
        out_shape=[jax.ShapeDtypeStruct(q.shape, BF16),
                   jax.ShapeDtypeStruct((batch, length, dilation * LANES), F32)],
        compiler_params=_params(2),
        name=f"band_attn_d{dilation}",
    )(q, k, v, jnp.asarray(bias0), jnp.asarray(bias))


CONV_PAD = 32
CONV_CHUNK = 64
SUBLANES = 8


def _mix_kernel(u_ref, up_ref, w_ref, b_ref, lg_ref, lb_ref, expand_ref, *refs, tt, dilations):
    n_pat = 1 + len(dilations)
    ins = refs[0:2 * n_pat]
    mix_ref, ext_ref = refs[2 * n_pat:2 * n_pat + 2]
    nat = refs[2 * n_pat + 2:]
    t = pl.program_id(1)
    rows = CONV_PAD + tt
    prev = up_ref[0]
    ext_ref[0, 0:CONV_PAD, :] = jnp.where(t > 0, prev, jnp.zeros_like(prev))
    ext_ref[0, CONV_PAD:rows, :] = u_ref[0]
    for s in range(1, SUBLANES):
        ext_ref[s, 0:rows - SUBLANES, :] = ext_ref[0, pl.ds(s, rows - SUBLANES), :]

    for j, dil in enumerate(dilations):
        for i, width in enumerate((D_ATTN, LANES)):
            src, dst = ins[2 + 2 * j + i], nat[2 * j + i]
            for c in range(dil):
                _put_rows(dst, c, dil, src[0, :, c * width:(c + 1) * width].astype(F32))
    os_ = [ins[0][...].astype(F32)] + [_get_tiles(nat[2 * j]) for j in range(len(dilations))]
    ls_ = [ins[1][...]] + [_get_tiles(nat[2 * j + 1]) for j in range(len(dilations))]
    mx = functools.reduce(jnp.maximum, ls_)
    es = [jnp.exp2(l - mx) for l in ls_]
    inv = 1.0 / sum(es)
    attn = None
    for e, o in zip(es, os_):
        w = e * inv
        hi = w.astype(BF16)
        lo = (w - hi.astype(F32)).astype(BF16)
        wide = (jnp.dot(hi, expand_ref[...], preferred_element_type=F32)
                + jnp.dot(lo, expand_ref[...], preferred_element_type=F32))
        attn = wide * o if attn is None else attn + wide * o
    mix_ref[:, 0:D_ATTN] = attn.astype(mix_ref.dtype)

    d_conv = u_ref.shape[-1]
    for ch in range(tt // CONV_CHUNK):
        acc = None
        for j in range(CONV_WIDTH):
            off = CONV_PAD - CONV_LEFT + j
            s = off % SUBLANES
            term = ext_ref[s, pl.ds(ch * CONV_CHUNK + off - s, CONV_CHUNK), :] * w_ref[j:j + 1, :]
            acc = term if acc is None else acc + term
        c = _ln_swish(acc + b_ref[...], lg_ref[...], lb_ref[...])
        mix_ref[ch * CONV_CHUNK:(ch + 1) * CONV_CHUNK, D_ATTN:D_ATTN + d_conv] = c.astype(mix_ref.dtype)


def _ln_swish(c, g, b):
    mu = jnp.mean(c, axis=-1, keepdims=True)
    cc = c - mu
    var = jnp.mean(cc * cc, axis=-1, keepdims=True)
    y = cc * lax.rsqrt(var + LN_EPS) * g + b
    return y * _sigmoid(y)


def _mix_prompt(u, conv_w, conv_b, ln_g, ln_b, attn, batch, seq, tt):
    d_conv = u.shape[-1]
    u3 = u.reshape(batch, seq, d_conv)
    n_t = seq // tt
    per = tt // CONV_PAD
    dilations = tuple(dil for _, dil in PATTERNS if dil > 1)
    cur = pl.BlockSpec((1, tt, d_conv), lambda b, t: (b, t, 0))
    prev = pl.BlockSpec((1, CONV_PAD, d_conv), lambda b, t: (b, jnp.maximum(t * per - 1, 0), 0))
    flat = lambda width: pl.BlockSpec((tt, width), lambda b, t: (b * n_t + t, 0))
    view = lambda dil, width: pl.BlockSpec((1, tt // dil, dil * width), lambda b, t: (b, t, 0))
    ins = [attn[1][0].reshape(batch * seq, D_ATTN), attn[1][1].reshape(batch * seq, LANES)]
    specs = [flat(D_ATTN), flat(LANES)]
    scratch = [pltpu.VMEM((SUBLANES, CONV_PAD + tt, d_conv), F32)]
    for dil in dilations:
        ins += list(attn[dil])
        specs += [view(dil, D_ATTN), view(dil, LANES)]
        scratch += [_tiles_scratch(tt, D_ATTN), _tiles_scratch(tt, LANES)]
    expand = np.zeros((LANES, D_ATTN), np.float32)
    expand[:N_HEADS] = np.kron(np.eye(N_HEADS), np.ones((1, HEAD_DIM)))
    return pl.pallas_call(
        functools.partial(_mix_kernel, tt=tt, dilations=dilations),
        grid=(batch, n_t),
        in_specs=[cur, prev, _const_spec(conv_w.shape), _const_spec((1, d_conv)),
                  _const_spec((1, d_conv)), _const_spec((1, d_conv)),
                  _const_spec((LANES, D_ATTN))] + specs,
        out_specs=pl.BlockSpec((tt, D_ATTN + d_conv), lambda b, t: (b * n_t + t, 0)),
        out_shape=jax.ShapeDtypeStruct((batch * seq, D_ATTN + d_conv), BF16),
        scratch_shapes=scratch,
        compiler_params=_params(2),
        name="mix_prompt",
    )(u3, u3, conv_w, conv_b, ln_g, ln_b, jnp.asarray(expand, BF16), *ins)


def _sample_conv_kernel(st_ref, u_ref, w_ref, b_ref, lg_ref, lb_ref, c_ref, ns_ref, us_ref, cs_ref,
                        *, nb, t_new):
    _put_tiles(us_ref, u_ref[...])
    new = [_get_rows(us_ref, t, nb, t_new) for t in range(t_new)]

    def ext(tau):
        return st_ref[tau] if tau < CONV_LEFT else new[tau - CONV_LEFT]

    for t in range(t_new):
        acc = None
        for j in range(CONV_WIDTH):
            term = ext(t + j) * w_ref[j:j + 1, :]
            acc = term if acc is None else acc + term
        c = _ln_swish(acc + b_ref[...], lg_ref[...], lb_ref[...])
        _put_rows(cs_ref, t, t_new, c)
    c_ref[...] = _get_tiles(cs_ref)
    for tau in range(CONV_LEFT):
        ns_ref[tau] = ext(tau + t_new)


def _sample_conv(state_t, u_s, conv_w, conv_b, ln_g, ln_b, layer, t_new, nb):
    _, n, d_conv = state_t.shape
    return pl.pallas_call(
        functools.partial(_sample_conv_kernel, nb=nb, t_new=t_new),
        grid=(n // nb,),
        in_specs=[pl.BlockSpec((CONV_LEFT, nb, d_conv), lambda i: (layer, i, 0)),
                  pl.BlockSpec((nb * t_new, d_conv), lambda i: (i, 0)),
                  _const_spec(conv_w.shape), _const_spec((1, d_conv)),
                  _const_spec((1, d_conv)), _const_spec((1, d_conv))],
        out_specs=[pl.BlockSpec((nb * t_new, d_conv), lambda i: (i, 0)),
                   pl.BlockSpec((CONV_LEFT, nb, d_conv), lambda i: (0, i, 0))],
        out_shape=[jax.ShapeDtypeStruct((n * t_new, d_conv), F32),
                   jax.ShapeDtypeStruct((CONV_LEFT, n, d_conv), F32)],
        scratch_shapes=[_tiles_scratch(nb * t_new, d_conv)] * 2,
        compiler_params=_params(1),
        name="sample_conv",
    )(state_t, u_s, conv_w, conv_b, ln_g, ln_b)


def _multiplicity(dist):
    dist = np.asarray(dist)
    c = np.zeros(dist.shape, np.float32)
    for window, dilation in PATTERNS:
        c += ((dist >= 0) & (dist <= window) & (dist % dilation == 0)).astype(np.float32)
    return c


def _shift_append(old, new_rows, out_ref, t_new):
    n_ch, l_buf = old.shape
    n_tiles = l_buf // LANES
    lane = lax.broadcasted_iota(jnp.int32, (n_ch, LANES), 1)
    keep = lane < LANES - t_new
    pad = jnp.concatenate([jnp.zeros((LANES - t_new, n_ch), F32), new_rows], axis=0)
    nxt = pad.T
    for j in reversed(range(n_tiles)):
        cur = pltpu.roll(old[:, j * LANES:(j + 1) * LANES], LANES - t_new, axis=1)
        out_ref[0, :, j * LANES:(j + 1) * LANES] = jnp.where(keep, cur, nxt)
        nxt = cur


CACHE_HEADS_PER_UNIT = 4


def _sample_unit(q_ref, kn_ref, vn_ref, ck_ref, cv_ref, cnt_ref, attn_ref, ok_ref, ov_ref, t_new):
    width = ck_ref.shape[1]
    n_heads = width // HEAD_DIM
    rows = n_heads * t_new
    ck = ck_ref[0]
    cv = cv_ref[0]
    kn = kn_ref[0]
    vn = vn_ref[0]
    _shift_append(ck, kn, ok_ref, t_new)
    _shift_append(cv, vn, ov_ref, t_new)

    row = lax.broadcasted_iota(jnp.int32, (rows, width), 0)
    lane = lax.broadcasted_iota(jnp.int32, (rows, width), 1)
    own_head = (row // t_new) == (lane // HEAD_DIM)
    q_rep = jnp.concatenate([q_ref[0]] * n_heads, axis=0)
    q_bd = jnp.where(own_head, q_rep, 0.0)

    cnt = cnt_ref[...]
    s_c = jnp.dot(q_bd.astype(BF16), ck.astype(BF16), preferred_element_type=F32)
    s_c = jnp.where(cnt > 0.0, s_c, NEG_BIG)
    m = jnp.max(s_c, axis=-1, keepdims=True)

    t_of_row = lax.broadcasted_iota(jnp.int32, (rows, 1), 0) % t_new
    s_n, c_n = [], []
    for tp in range(t_new):
        d = t_of_row - tp
        c = jnp.zeros((rows, 1), F32)
        for _, dilation in PATTERNS:
            c = c + jnp.where((d >= 0) & (d % dilation == 0), 1.0, 0.0)
        s = jnp.sum(q_bd * kn[tp:tp + 1, :], axis=-1, keepdims=True)
        s = jnp.where(c > 0.0, s, NEG_BIG)
        m = jnp.maximum(m, s)
        s_n.append(s)
        c_n.append(c)

    p_c = cnt * jnp.exp2(s_c - m)
    l = jnp.sum(p_c, axis=-1, keepdims=True)
    acc = lax.dot_general(p_c.astype(BF16), cv.astype(BF16), (((1,), (1,)), ((), ())),
                          preferred_element_type=F32)
    for tp in range(t_new):
        p = c_n[tp] * jnp.exp2(s_n[tp] - m)
        l = l + p
        acc = acc + p * vn[tp:tp + 1, :]
    acc = jnp.where(own_head, acc * (1.0 / l), 0.0)
    out = acc[0:t_new, :]
    for h in range(1, n_heads):
        out = out + acc[h * t_new:(h + 1) * t_new, :]
    attn_ref[0] = out.astype(attn_ref.dtype)


def _ffn_cache_kernel(x_ref, mix_ref, wo_ref, g_ref, wg_ref, wu_ref, wd_ref,
                      q_ref, kn_ref, vn_ref, ck_ref, cv_ref, cnt_ref,
                      y_ref, attn_ref, ok_ref, ov_ref, h_ref, *, steps_per_tile, t_new):
    c = pl.program_id(0) % steps_per_tile
    base, extra = divmod(wg_ref.shape[0], steps_per_tile)

    @pl.when(c == 0)
    def _():
        x1 = x_ref[...] + jnp.dot(mix_ref[...], wo_ref[...], preferred_element_type=F32)
        y_ref[...] = x1
        h_ref[...] = _rms_rows(x1, g_ref[...]).astype(BF16)

    def ffn_chunk(j):
        h = h_ref[...]
        gate = jnp.dot(h, wg_ref[j], preferred_element_type=F32)
        up = jnp.dot(h, wu_ref[j], preferred_element_type=F32)
        act = (gate * _sigmoid(gate) * up).astype(BF16)
        y_ref[...] += jnp.dot(act, wd_ref[j], preferred_element_type=F32)

    for i in range(base):
        ffn_chunk(c * base + i)
    if extra:
        first = steps_per_tile - extra

        @pl.when(c >= first)
        def _():
            ffn_chunk(steps_per_tile * base + c - first)

    _sample_unit(q_ref, kn_ref, vn_ref, ck_ref, cv_ref, cnt_ref, attn_ref, ok_ref, ov_ref, t_new)


def _ffn_with_cache(x2d, mix, wo_b, g, wg, wu, wd, q_s, k_s, v_s, cache_kt, cache_vt, layer, tm):
    m, d = x2d.shape
    n, t_new, _ = q_s.shape
    l_buf = cache_kt.shape[2]
    assert l_buf == WIN_MAX and t_new % SUBLANES == 0
    uw = CACHE_HEADS_PER_UNIT * HEAD_DIM
    per_n = D_ATTN // uw
    n_tiles, n_units = m // tm, n * per_n
    assert n_units % n_tiles == 0
    steps_per_tile = n_units // n_tiles
    d_ff = wg.shape[1]
    assert d_ff % MXU_COLS == 0 and d_ff // MXU_COLS >= steps_per_tile
    n_chunks = d_ff // MXU_COLS
    by_cols = lambda w: w.reshape(d, n_chunks, MXU_COLS).transpose(1, 0, 2).astype(BF16)
    wg3, wu3 = by_cols(wg), by_cols(wu)
    wd3 = wd.reshape(n_chunks, MXU_COLS, d).astype(BF16)
    dist = l_buf + np.arange(t_new)[:, None] - np.arange(l_buf)[None, :]
    cnt = np.tile(_multiplicity(dist), (CACHE_HEADS_PER_UNIT, 1))

    tile = lambda width: pl.BlockSpec((tm, width), lambda u: (u // steps_per_tile, 0))
    new = pl.BlockSpec((1, t_new, uw), lambda u: (u // per_n, 0, u % per_n))
    big = pl.BlockSpec((1, uw, l_buf), lambda u: (u // per_n, u % per_n, 0))
    big_in = pl.BlockSpec((1, uw, l_buf), lambda u: (layer * n + u // per_n, u % per_n, 0))
    return pl.pallas_call(
        functools.partial(_ffn_cache_kernel, steps_per_tile=steps_per_tile, t_new=t_new),
        grid=(n_units,),
        in_specs=[tile(d), tile(mix.shape[1]), _const_spec(wo_b.shape), _const_spec((1, d)),
                  _const_spec(wg3.shape), _const_spec(wu3.shape), _const_spec(wd3.shape),
                  new, new, new, big_in, big_in, _const_spec(cnt.shape)],
        out_specs=[tile(d), new, big, big],
        out_shape=[jax.ShapeDtypeStruct((m, d), F32),
                   jax.ShapeDtypeStruct((n, t_new, D_ATTN), F32),
                   jax.ShapeDtypeStruct((n, D_ATTN, l_buf), F32),
                   jax.ShapeDtypeStruct((n, D_ATTN, l_buf), F32)],
        scratch_shapes=[pltpu.VMEM((tm, d), BF16)],
        compiler_params=_params(1),
        name="ffn_cache",
    )(x2d, mix, wo_b, g, wg3, wu3, wd3, q_s, k_s, v_s, cache_kt, cache_vt, jnp.asarray(cnt))


def _ffn_kernel(x_ref, *refs, n_mix):
    mix_refs = refs[0:n_mix]
    wo_ref, g_ref, wg_ref, wu_ref, wd_ref, y_ref = refs[n_mix:]
    x1 = x_ref[...]
    row = 0
    for mr in mix_refs:
        width = mr.shape[1]
        x1 = x1 + jnp.dot(mr[...].astype(BF16), wo_ref[row:row + width, :], preferred_element_type=F32)
        row += width
    h = _rms_rows(x1, g_ref[...]).astype(BF16)
    gate = jnp.dot(h, wg_ref[...], preferred_element_type=F32)
    up = jnp.dot(h, wu_ref[...], preferred_element_type=F32)
    act = (gate * _sigmoid(gate) * up).astype(BF16)
    y_ref[...] = x1 + jnp.dot(act, wd_ref[...], preferred_element_type=F32)


def _outproj_ffn(x2d, mix_parts, wo_b, g, wg_b, wu_b, wd_b, tm):
    m, d = x2d.shape
    tm = min(tm, m)
    d_ff = wg_b.shape[1]
    row = lambda i: (i, 0)
    return pl.pallas_call(
        functools.partial(_ffn_kernel, n_mix=len(mix_parts)),
        grid=(m // tm,),
        in_specs=[pl.BlockSpec((tm, d), row)] + [pl.BlockSpec((tm, p.shape[1]), row) for p in mix_parts]
        + [_const_spec(wo_b.shape), _const_spec((1, d)),
           _const_spec((d, d_ff)), _const_spec((d, d_ff)), _const_spec((d_ff, d))],
        out_specs=pl.BlockSpec((tm, d), row),
        out_shape=jax.ShapeDtypeStruct((m, d), F32),
        compiler_params=_params(1),
        name="outproj_ffn",
    )(x2d, *mix_parts, wo_b, g, wg_b, wu_b, wd_b)


TOKEN_TILE = 512
MIX_TIME_TILE = 256
SAMPLE_CONV_BATCH = 32


def kernel(x_prompt, x_sample, cache_k, cache_v, state_conv, attn_norm_g, w_in, q_norm_g, k_norm_g,
           conv_w, conv_b, conv_ln_g, conv_ln_b, w_out, ffn_norm_g, w_gate, w_up, w_down):
    batch, seq, d_model = x_prompt.shape
    n_dec, t_new, _ = x_sample.shape
    depth = w_in.shape[0]
    d_conv = conv_w.shape[-1]
    l_buf = cache_k.shape[2]
    assert seq == WIN_MAX and l_buf == WIN_MAX and seq % (BLK * PATTERNS[-1][1]) == 0
    assert w_in.shape[-1] == 3 * D_ATTN + 2 * d_conv and d_conv == D_ATTN

    seg = jnp.asarray(np.kron(np.eye(N_HEADS), np.ones((HEAD_DIM, HEAD_DIM))), BF16)
    row = lambda v: v.reshape(1, -1)
    to_t = lambda c: jnp.transpose(c, (0, 1, 3, 4, 2)).reshape(depth * n_dec, D_ATTN, l_buf)
    from_t = lambda c, n: jnp.transpose(c.reshape(n, N_HEADS, HEAD_DIM, -1), (0, 3, 1, 2))
    cache_kt, cache_vt = to_t(cache_k), to_t(cache_v)
    state_t = jnp.transpose(state_conv, (0, 2, 1, 3)).reshape(depth * CONV_LEFT, n_dec, d_conv)

    y_p = x_prompt.reshape(batch * seq, d_model)
    y_s = x_sample.reshape(n_dec * t_new, d_model)
    outs = [[] for _ in range(6)]
    for l in range(depth):
        w_in_b = w_in[l].astype(BF16)
        wo_b, wg_b, wu_b, wd_b = (w[l].astype(BF16) for w in (w_out, w_gate, w_up, w_down))
        g_attn, g_ffn = row(attn_norm_g[l]), row(ffn_norm_g[l])
        qg = row(jnp.tile(q_norm_g[l], N_HEADS))
        kg = row(jnp.tile(k_norm_g[l], N_HEADS))
        cb, lg, lb = row(conv_b[l]), row(conv_ln_g[l]), row(conv_ln_b[l])

        q, k, v, u_s = _inproj_sample(y_s, g_attn, w_in_b, qg, kg, seg, TOKEN_TILE)
        c_s, ns = _sample_conv(state_t, u_s, conv_w[l], cb, lg, lb, l, t_new, min(SAMPLE_CONV_BATCH, n_dec))
        qkv, kt, vt, u = _inproj_prompt(y_p, g_attn, w_in_b, qg, kg, seg, batch, seq, TOKEN_TILE)
        attn = {dil: _band_attention(*qkv[dil], w, dil) for w, dil in PATTERNS}
        mix = _mix_prompt(u, conv_w[l], cb, lg, lb, attn, batch, seq, MIX_TIME_TILE)

        as3 = lambda t: t.reshape(n_dec, t_new, D_ATTN)
        y_p, attn_s, nk, nv = _ffn_with_cache(y_p, mix, wo_b, g_ffn, w_gate[l], w_up[l], w_down[l],
                                              as3(q), as3(k), as3(v), cache_kt, cache_vt, l, TOKEN_TILE)
        y_s = _outproj_ffn(y_s, [attn_s.reshape(n_dec * t_new, D_ATTN), c_s],
                           wo_b, g_ffn, wg_b, wu_b, wd_b, TOKEN_TILE)
        outs[0].append(from_t(kt, batch))
        outs[1].append(from_t(vt, batch))
        outs[2].append(u.reshape(batch, seq, d_conv)[:, seq - CONV_LEFT:])
        outs[3].append(from_t(nk, n_dec))
        outs[4].append(from_t(nv, n_dec))
        outs[5].append(jnp.transpose(ns, (1, 0, 2)))

    stack = lambda xs: xs[0][None] if len(xs) == 1 else jnp.stack(xs)
    return (y_p.reshape(batch, seq, d_model), y_s.reshape(n_dec, t_new, d_model)) + tuple(
        stack(o) for o in outs)
```

```python
import functools

import numpy as np
import jax
import jax.numpy as jnp
from jax import lax
from jax.experimental import pallas as pl
from jax.experimental.pallas import tpu as pltpu

N_HEADS = 8
HEAD_DIM = 64
D_ATTN = N_HEADS * HEAD_DIM
PATTERNS = ((128, 1), (512, 4), (2048, 16))
WIN_MAX = max(w for w, _ in PATTERNS)
CONV_WIDTH = 31
CONV_LEFT = CONV_WIDTH - 1
BLK = 128
RMS_EPS = 1e-6
LN_EPS = 1e-5
ATTN_SCALE = HEAD_DIM ** -0.5
LOG2E = 1.4426950408889634
NEG_BIG = -1e30

LANES = 128
MXU_COLS = 256
HEADS_PER_LANE_TILE = LANES // HEAD_DIM
VMEM_LIMIT_BYTES = 56 * 1024 * 1024

F32 = jnp.float32
BF16 = jnp.bfloat16


def _params(n_axes):
    return pltpu.CompilerParams(dimension_semantics=("arbitrary",) * n_axes,
                                vmem_limit_bytes=VMEM_LIMIT_BYTES)


def _const_spec(shape):
    return pl.BlockSpec(shape, lambda *_: (0,) * len(shape), pipeline_mode=pl.Buffered(1))


def _sigmoid(x):
    return 1.0 / (1.0 + jnp.exp(-x))


def _rms_rows(x, g):
    ms = jnp.mean(x * x, axis=-1, keepdims=True)
    return x * lax.rsqrt(ms + RMS_EPS) * g


def _class_lanes(c):
    return slice(c * D_ATTN, (c + 1) * D_ATTN)


def _tiles_scratch(rows, width):
    return pltpu.VMEM((width // LANES, rows, LANES), F32)


def _put_tiles(scr, x):
    for lt in range(scr.shape[0]):
        scr[lt] = x[:, lt * LANES:(lt + 1) * LANES]


def _get_tiles(scr):
    return jnp.concatenate([scr[lt] for lt in range(scr.shape[0])], axis=1)


def _get_rows(scr, start, n, stride):
    return jnp.concatenate([scr[lt, pl.ds(start, n, stride=stride), :] for lt in range(scr.shape[0])],
                           axis=1)


def _put_rows(scr, start, stride, x):
    for lt in range(scr.shape[0]):
        scr[lt, pl.ds(start, x.shape[0], stride=stride), :] = x[:, lt * LANES:(lt + 1) * LANES]


def _inproj_kernel(x_ref, g_ref, w_ref, qg_ref, kg_ref, seg_ref, *refs, dilations):
    h = _rms_rows(x_ref[...], g_ref[...]).astype(BF16)

    def proj(col):
        return jnp.dot(h, w_ref[:, col * D_ATTN:(col + 1) * D_ATTN], preferred_element_type=F32)

    def head_rms(t, gain):
        ss = jnp.dot((t * t).astype(BF16), seg_ref[...], preferred_element_type=F32)
        return t * lax.rsqrt(ss * (1.0 / HEAD_DIM) + RMS_EPS) * gain

    q = head_rms(proj(0), qg_ref[...]) * (ATTN_SCALE * LOG2E)
    k = head_rms(proj(1), kg_ref[...])
    v = proj(2)
    u = proj(3) * _sigmoid(proj(4))
    if not dilations:
        q_ref, k_ref, v_ref, u_ref = refs
        q_ref[...] = q
        k_ref[...] = k
        v_ref[...] = v
        u_ref[...] = u
        return

    n_views = 3 * len(dilations)
    nat = refs[0:3]
    views = refs[3:3 + n_views]
    kt_ref, vt_ref, u_ref = refs[3 + n_views:6 + n_views]
    scratch = refs[6 + n_views:]
    tm = x_ref.shape[0]
    u_ref[...] = u
    kt_ref[0] = k.T
    vt_ref[0] = v.T
    for i, t in enumerate((q, k, v)):
        nat[i][...] = t.astype(BF16)
        _put_tiles(scratch[i], t)
        for j, dil in enumerate(dilations):
            for c in range(dil):
                views[3 * j + i][0, :, _class_lanes(c)] = (
                    _get_rows(scratch[i], c, tm // dil, dil).astype(BF16))


def _inproj_sample(x2d, g, w_in_b, qg, kg, seg, tm):
    m, d = x2d.shape
    tm = min(tm, m)
    row = lambda i: (i, 0)
    out_f = jax.ShapeDtypeStruct((m, D_ATTN), F32)
    tile = pl.BlockSpec((tm, D_ATTN), row)
    return pl.pallas_call(
        functools.partial(_inproj_kernel, dilations=()),
        grid=(m // tm,),
        in_specs=[pl.BlockSpec((tm, d), row), _const_spec((1, d)), _const_spec(w_in_b.shape),
                  _const_spec((1, D_ATTN)), _const_spec((1, D_ATTN)), _const_spec((D_ATTN, D_ATTN))],
        out_specs=[tile] * 4,
        out_shape=[out_f] * 4,
        compiler_params=_params(1),
        name="inproj_sample",
    )(x2d, g, w_in_b, qg, kg, seg)


def _inproj_prompt(x2d, g, w_in_b, qg, kg, seg, batch, seq, tm):
    m, d = x2d.shape
    n_t = seq // tm
    dilations = tuple(dil for _, dil in PATTERNS if dil > 1)
    flat = lambda b, t: (b * n_t + t, 0)
    nat_spec = pl.BlockSpec((tm, D_ATTN), flat)
    nat_b = jax.ShapeDtypeStruct((m, D_ATTN), BF16)
    view_specs, view_shapes = [], []
    for dil in dilations:
        view_specs += [pl.BlockSpec((1, tm // dil, dil * D_ATTN), lambda b, t: (b, t, 0))] * 3
        view_shapes += [jax.ShapeDtypeStruct((batch, seq // dil, dil * D_ATTN), BF16)] * 3
    t_spec = pl.BlockSpec((1, D_ATTN, tm), lambda b, t: (b, 0, t))
    t_shape = jax.ShapeDtypeStruct((batch, D_ATTN, seq), F32)
    res = pl.pallas_call(
        functools.partial(_inproj_kernel, dilations=dilations),
        grid=(batch, n_t),
        in_specs=[pl.BlockSpec((tm, d), flat), _const_spec((1, d)), _const_spec(w_in_b.shape),
                  _const_spec((1, D_ATTN)), _const_spec((1, D_ATTN)), _const_spec((D_ATTN, D_ATTN))],
        out_specs=[nat_spec] * 3 + view_specs + [t_spec, t_spec, nat_spec],
        out_shape=[nat_b] * 3 + view_shapes + [t_shape, t_shape, jax.ShapeDtypeStruct((m, D_ATTN), F32)],
        scratch_shapes=[_tiles_scratch(tm, D_ATTN)] * 3,
        compiler_params=_params(2),
        name="inproj_prompt",
    )(x2d, g, w_in_b, qg, kg, seg)
    n_views = 3 * len(dilations)
    qkv = {1: tuple(t.reshape(batch, seq, D_ATTN) for t in res[0:3])}
    for j, dil in enumerate(dilations):
        qkv[dil] = tuple(res[3 + 3 * j:6 + 3 * j])
    kt, vt, u = res[3 + n_views:]
    return qkv, kt, vt, u


def _band_attn_kernel(q_ref, k_ref, v_ref, bias0_ref, bias_ref, o_ref, lse_ref, *, n_blocks, n_classes):
    lane = lax.broadcasted_iota(jnp.int32, (BLK, LANES), 1)
    upper = lane >= HEAD_DIM
    n_pairs = N_HEADS // HEADS_PER_LANE_TILE

    def block(cl, r0, k0, n_keys, bias):
        vps, scores = [], []
        for hp in range(n_pairs):
            lanes = slice(cl * D_ATTN + hp * LANES, cl * D_ATTN + (hp + 1) * LANES)
            qp = q_ref[0, pl.ds(r0, BLK), lanes]
            kp = k_ref[0, pl.ds(k0, n_keys), lanes]
            vps.append(v_ref[0, pl.ds(k0, n_keys), lanes])
            for hh in range(HEADS_PER_LANE_TILE):
                qm = jnp.where(upper if hh else ~upper, qp, jnp.zeros_like(qp))
                scores.append(lax.dot_general(qm, kp, (((1,), (1,)), ((), ())),
                                              preferred_element_type=F32) + bias)
        probs, inv_l = [], []
        lse_tile = jnp.zeros((BLK, LANES), F32)
        for h, s in enumerate(scores):
            m = jnp.max(s, axis=-1, keepdims=True)
            p = jnp.exp2(s - m)
            l = jnp.sum(p, axis=-1, keepdims=True)
            probs.append(p.astype(BF16))
            inv_l.append(1.0 / l)
            lse_tile = jnp.where(lane == h, m + jnp.log(l) * LOG2E, lse_tile)
        lse_ref[0, pl.ds(r0, BLK), cl * LANES:(cl + 1) * LANES] = lse_tile
        for hp in range(n_pairs):
            lanes = slice(cl * D_ATTN + hp * LANES, cl * D_ATTN + (hp + 1) * LANES)
            o_lo, o_hi = (jnp.dot(probs[2 * hp + hh], vps[hp], preferred_element_type=F32)
                          * inv_l[2 * hp + hh] for hh in range(HEADS_PER_LANE_TILE))
            o_ref[0, pl.ds(r0, BLK), lanes] = jnp.where(upper, o_hi, o_lo).astype(o_ref.dtype)

    for cl in range(n_classes):
        block(cl, 0, 0, BLK, bias0_ref[...])
        if n_blocks > 1:
            def body(i, carry, cl=cl):
                r0 = pl.multiple_of(i * BLK, BLK)
                block(cl, r0, pl.multiple_of(r0 - BLK, BLK), 2 * BLK, bias_ref[...])
                return carry

            lax.fori_loop(1, n_blocks, body, 0)


def _band_biases(n_sub):
    qi = np.arange(BLK)[:, None]
    ki = np.arange(2 * BLK)[None, :]
    dist = BLK + qi - ki
    band = (dist >= 0) & (dist <= n_sub)
    d0 = qi - np.arange(BLK)[None, :]
    to_bias = lambda ok: np.where(ok, 0.0, NEG_BIG).astype(np.float32)
    return to_bias((d0 >= 0) & (d0 <= n_sub)), to_bias(band)


BAND_CLASSES_PER_STEP = 4


def _band_attention(q, k, v, window, dilation):
    batch, length, _ = q.shape
    cps = min(BAND_CLASSES_PER_STEP, dilation)
    bias0, bias = _band_biases(window // dilation)
    blk = pl.BlockSpec((1, length, cps * D_ATTN), lambda b, c: (b, 0, c))
    return pl.pallas_call(
        functools.partial(_band_attn_kernel, n_blocks=length // BLK, n_classes=cps),
        grid=(batch, dilation // cps),
        in_specs=[blk, blk, blk, _const_spec((BLK, BLK)), _const_spec((BLK, 2 * BLK))],
        out_specs=[blk, pl.BlockSpec((1, length, cps * LANES), lambda b, c: (b, 0, c))],
        out_shape=[jax.ShapeDtypeStruct(q.shape, BF16),
                   jax.ShapeDtypeStruct((batch, length, dilation * LANES), F32)],
        compiler_params=_params(2),
        name=f"band_attn_d{dilation}",
    )(q, k, v, jnp.asarray(bias0), jnp.asarray(bias))


CONV_PAD = 32
CONV_CHUNK = 64
SUBLANES = 8


def _mix_kernel(u_ref, up_ref, w_ref, b_ref, lg_ref, lb_ref, expand_ref, *refs, tt, dilations):
    n_pat = 1 + len(dilations)
    ins = refs[0:2 * n_pat]
    mix_ref, ext_ref = refs[2 * n_pat:2 * n_pat + 2]
    nat = refs[2 * n_pat + 2:]
    t = pl.program_id(1)
    rows = CONV_PAD + tt
    prev = up_ref[0]
    ext_ref[0, 0:CONV_PAD, :] = jnp.where(t > 0, prev, jnp.zeros_like(prev))
    ext_ref[0, CONV_PAD:rows, :] = u_ref[0]
    for s in range(1, SUBLANES):
        ext_ref[s, 0:rows - SUBLANES, :] = ext_ref[0, pl.ds(s, rows - SUBLANES), :]

    for j, dil in enumerate(dilations):
        for i, width in enumerate((D_ATTN, LANES)):
            src, dst = ins[2 + 2 * j + i], nat[2 * j + i]
            for c in range(dil):
                _put_rows(dst, c, dil, src[0, :, c * width:(c + 1) * width].astype(F32))
    os_ = [ins[0][...].astype(F32)] + [_get_tiles(nat[2 * j]) for j in range(len(dilations))]
    ls_ = [ins[1][...]] + [_get_tiles(nat[2 * j + 1]) for j in range(len(dilations))]
    mx = functools.reduce(jnp.maximum, ls_)
    es = [jnp.exp2(l - mx) for l in ls_]
    inv = 1.0 / sum(es)
    attn = None
    for e, o in zip(es, os_):
        w = e * inv
        hi = w.astype(BF16)
        lo = (w - hi.astype(F32)).astype(BF16)
        wide = (jnp.dot(hi, expand_ref[...], preferred_element_type=F32)
                + jnp.dot(lo, expand_ref[...], preferred_element_type=F32))
        attn = wide * o if attn is None else attn + wide * o
    mix_ref[:, 0:D_ATTN] = attn.astype(mix_ref.dtype)

    d_conv = u_ref.shape[-1]
    for ch in range(tt // CONV_CHUNK):
        acc = None
        for j in range(CONV_WIDTH):
            off = CONV_PAD - CONV_LEFT + j
            s = off % SUBLANES
            term = ext_ref[s, pl.ds(ch * CONV_CHUNK + off - s, CONV_CHUNK), :] * w_ref[j:j + 1, :]
            acc = term if acc is None else acc + term
        c = _ln_swish(acc + b_ref[...], lg_ref[...], lb_ref[...])
        mix_ref[ch * CONV_CHUNK:(ch + 1) * CONV_CHUNK, D_ATTN:D_ATTN + d_conv] = c.astype(mix_ref.dtype)


def _ln_swish(c, g, b):
    mu = jnp.mean(c, axis=-1, keepdims=True)
    cc = c - mu
    var = jnp.mean(cc * cc, axis=-1, keepdims=True)
    y = cc * lax.rsqrt(var + LN_EPS) * g + b
    return y * _sigmoid(y)


def _mix_prompt(u, conv_w, conv_b, ln_g, ln_b, attn, batch, seq, tt):
    d_conv = u.shape[-1]
    u3 = u.reshape(batch, seq, d_conv)
    n_t = seq // tt
    per = tt // CONV_PAD
    dilations = tuple(dil for _, dil in PATTERNS if dil > 1)
    cur = pl.BlockSpec((1, tt, d_conv), lambda b, t: (b, t, 0))
    prev = pl.BlockSpec((1, CONV_PAD, d_conv), lambda b, t: (b, jnp.maximum(t * per - 1, 0), 0))
    flat = lambda width: pl.BlockSpec((tt, width), lambda b, t: (b * n_t + t, 0))
    view = lambda dil, width: pl.BlockSpec((1, tt // dil, dil * width), lambda b, t: (b, t, 0))
    ins = [attn[1][0].reshape(batch * seq, D_ATTN), attn[1][1].reshape(batch * seq, LANES)]
    specs = [flat(D_ATTN), flat(LANES)]
    scratch = [pltpu.VMEM((SUBLANES, CONV_PAD + tt, d_conv), F32)]
    for dil in dilations:
        ins += list(attn[dil])
        specs += [view(dil, D_ATTN), view(dil, LANES)]
        scratch += [_tiles_scratch(tt, D_ATTN), _tiles_scratch(tt, LANES)]
    expand = np.zeros((LANES, D_ATTN), np.float32)
    expand[:N_HEADS] = np.kron(np.eye(N_HEADS), np.ones((1, HEAD_DIM)))
    return pl.pallas_call(
        functools.partial(_mix_kernel, tt=tt, dilations=dilations),
        grid=(batch, n_t),
        in_specs=[cur, prev, _const_spec(conv_w.shape), _const_spec((1, d_conv)),
                  _const_spec((1, d_conv)), _const_spec((1, d_conv)),
                  _const_spec((LANES, D_ATTN))] + specs,
        out_specs=pl.BlockSpec((tt, D_ATTN + d_conv), lambda b, t: (b * n_t + t, 0)),
        out_shape=jax.ShapeDtypeStruct((batch * seq, D_ATTN + d_conv), BF16),
        scratch_shapes=scratch,
        compiler_params=_params(2),
        name="mix_prompt",
    )(u3, u3, conv_w, conv_b, ln_g, ln_b, jnp.asarray(expand, BF16), *ins)


def _sample_conv_kernel(st_ref, u_ref, w_ref, b_ref, lg_ref, lb_ref, c_ref, ns_ref, us_ref, cs_ref,
                        *, nb, t_new):
    _put_tiles(us_ref, u_ref[...])
    new = [_get_rows(us_ref, t, nb, t_new) for t in range(t_new)]

    def ext(tau):
        return st_ref[tau] if tau < CONV_LEFT else new[tau - CONV_LEFT]

    for t in range(t_new):
        acc = None
        for j in range(CONV_WIDTH):
            term = ext(t + j) * w_ref[j:j + 1, :]
            acc = term if acc is None else acc + term
        c = _ln_swish(acc + b_ref[...], lg_ref[...], lb_ref[...])
        _put_rows(cs_ref, t, t_new, c)
    c_ref[...] = _get_tiles(cs_ref)
    for tau in range(CONV_LEFT):
        ns_ref[tau] = ext(tau + t_new)


def _sample_conv(state_t, u_s, conv_w, conv_b, ln_g, ln_b, layer, t_new, nb):
    _, n, d_conv = state_t.shape
    return pl.pallas_call(
        functools.partial(_sample_conv_kernel, nb=nb, t_new=t_new),
        grid=(n // nb,),
        in_specs=[pl.BlockSpec((CONV_LEFT, nb, d_conv), lambda i: (layer, i, 0)),
                  pl.BlockSpec((nb * t_new, d_conv), lambda i: (i, 0)),
                  _const_spec(conv_w.shape), _const_spec((1, d_conv)),
                  _const_spec((1, d_conv)), _const_spec((1, d_conv))],
        out_specs=[pl.BlockSpec((nb * t_new, d_conv), lambda i: (i, 0)),
                   pl.BlockSpec((CONV_LEFT, nb, d_conv), lambda i: (0, i, 0))],
        out_shape=[jax.ShapeDtypeStruct((n * t_new, d_conv), F32),
                   jax.ShapeDtypeStruct((CONV_LEFT, n, d_conv), F32)],
        scratch_shapes=[_tiles_scratch(nb * t_new, d_conv)] * 2,
        compiler_params=_params(1),
        name="sample_conv",
    )(state_t, u_s, conv_w, conv_b, ln_g, ln_b)


def _multiplicity(dist):
    dist = np.asarray(dist)
    c = np.zeros(dist.shape, np.float32)
    for window, dilation in PATTERNS:
        c += ((dist >= 0) & (dist <= window) & (dist % dilation == 0)).astype(np.float32)
    return c


def _shift_append(old, new_rows, out_ref, t_new):
    n_ch, l_buf = old.shape
    n_tiles = l_buf // LANES
    lane = lax.broadcasted_iota(jnp.int32, (n_ch, LANES), 1)
    keep = lane < LANES - t_new
    pad = jnp.concatenate([jnp.zeros((LANES - t_new, n_ch), F32), new_rows], axis=0)
    nxt = pad.T
    for j in reversed(range(n_tiles)):
        cur = pltpu.roll(old[:, j * LANES:(j + 1) * LANES], LANES - t_new, axis=1)
        out_ref[0, :, j * LANES:(j + 1) * LANES] = jnp.where(keep, cur, nxt)
        nxt = cur


CACHE_HEADS_PER_UNIT = 4


def _sample_unit(q_ref, kn_ref, vn_ref, ck_ref, cv_ref, cnt_ref, attn_ref, ok_ref, ov_ref, t_new):
    width = ck_ref.shape[1]
    n_heads = width // HEAD_DIM
    rows = n_heads * t_new
    ck = ck_ref[0]
    cv = cv_ref[0]
    kn = kn_ref[0]
    vn = vn_ref[0]
    _shift_append(ck, kn, ok_ref, t_new)
    _shift_append(cv, vn, ov_ref, t_new)

    row = lax.broadcasted_iota(jnp.int32, (rows, width), 0)
    lane = lax.broadcasted_iota(jnp.int32, (rows, width), 1)
    own_head = (row // t_new) == (lane // HEAD_DIM)
    q_rep = jnp.concatenate([q_ref[0]] * n_heads, axis=0)
    q_bd = jnp.where(own_head, q_rep, 0.0)

    cnt = cnt_ref[...]
    s_c = jnp.dot(q_bd.astype(BF16), ck.astype(BF16), preferred_element_type=F32)
    s_c = jnp.where(cnt > 0.0, s_c, NEG_BIG)
    m = jnp.max(s_c, axis=-1, keepdims=True)

    t_of_row = lax.broadcasted_iota(jnp.int32, (rows, 1), 0) % t_new
    s_n, c_n = [], []
    for tp in range(t_new):
        d = t_of_row - tp
        c = jnp.zeros((rows, 1), F32)
        for _, dilation in PATTERNS:
            c = c + jnp.where((d >= 0) & (d % dilation == 0), 1.0, 0.0)
        s = jnp.sum(q_bd * kn[tp:tp + 1, :], axis=-1, keepdims=True)
        s = jnp.where(c > 0.0, s, NEG_BIG)
        m = jnp.maximum(m, s)
        s_n.append(s)
        c_n.append(c)

    p_c = cnt * jnp.exp2(s_c - m)
    l = jnp.sum(p_c, axis=-1, keepdims=True)
    acc = lax.dot_general(p_c.astype(BF16), cv.astype(BF16), (((1,), (1,)), ((), ())),
                          preferred_element_type=F32)
    for tp in range(t_new):
        p = c_n[tp] * jnp.exp2(s_n[tp] - m)
        l = l + p
        acc = acc + p * vn[tp:tp + 1, :]
    acc = jnp.where(own_head, acc * (1.0 / l), 0.0)
    out = acc[0:t_new, :]
    for h in range(1, n_heads):
        out = out + acc[h * t_new:(h + 1) * t_new, :]
    attn_ref[0] = out.astype(attn_ref.dtype)


def _swiglu_chunk(h, wg_ref, wu_ref, wd_ref, j):
    gate = jnp.dot(h, wg_ref[j], preferred_element_type=F32)
    up = jnp.dot(h, wu_ref[j], preferred_element_type=F32)
    act = (gate * _sigmoid(gate) * up).astype(BF16)
    return jnp.dot(act, wd_ref[j], preferred_element_type=F32)


def _ffn_cache_kernel(x_ref, mix_ref, wo_ref, g_ref, wg_ref, wu_ref, wd_ref,
                      q_ref, kn_ref, vn_ref, ck_ref, cv_ref, cnt_ref,
                      y_ref, attn_ref, ok_ref, ov_ref, h_ref, *, steps_per_tile, t_new):
    c = pl.program_id(0) % steps_per_tile
    base, extra = divmod(wg_ref.shape[0], steps_per_tile)

    @pl.when(c == 0)
    def _():
        x1 = x_ref[...] + jnp.dot(mix_ref[...], wo_ref[...], preferred_element_type=F32)
        y_ref[...] = x1
        h_ref[...] = _rms_rows(x1, g_ref[...]).astype(BF16)

    for i in range(base):
        y_ref[...] += _swiglu_chunk(h_ref[...], wg_ref, wu_ref, wd_ref, c * base + i)
    if extra:
        first = steps_per_tile - extra

        @pl.when(c >= first)
        def _():
            y_ref[...] += _swiglu_chunk(h_ref[...], wg_ref, wu_ref, wd_ref, steps_per_tile * base + c - first)

    _sample_unit(q_ref, kn_ref, vn_ref, ck_ref, cv_ref, cnt_ref, attn_ref, ok_ref, ov_ref, t_new)


def _chunk_major(wg, wu, wd):
    d, d_ff = wg.shape
    assert d_ff % MXU_COLS == 0
    n_chunks = d_ff // MXU_COLS
    by_cols = lambda w: w.reshape(d, n_chunks, MXU_COLS).transpose(1, 0, 2).astype(BF16)
    return by_cols(wg), by_cols(wu), wd.reshape(n_chunks, MXU_COLS, d).astype(BF16)


def _ffn_with_cache(x2d, mix, wo_b, g, wg3, wu3, wd3, q_s, k_s, v_s, cache_kt, cache_vt, layer, tm):
    m, d = x2d.shape
    n, t_new, _ = q_s.shape
    l_buf = cache_kt.shape[2]
    assert l_buf == WIN_MAX and t_new % SUBLANES == 0
    uw = CACHE_HEADS_PER_UNIT * HEAD_DIM
    per_n = D_ATTN // uw
    n_tiles, n_units = m // tm, n * per_n
    assert n_units % n_tiles == 0
    steps_per_tile = n_units // n_tiles
    assert wg3.shape[0] >= steps_per_tile
    dist = l_buf + np.arange(t_new)[:, None] - np.arange(l_buf)[None, :]
    cnt = np.tile(_multiplicity(dist), (CACHE_HEADS_PER_UNIT, 1))

    tile = lambda width: pl.BlockSpec((tm, width), lambda u: (u // steps_per_tile, 0))
    new = pl.BlockSpec((1, t_new, uw), lambda u: (u // per_n, 0, u % per_n))
    big = pl.BlockSpec((1, uw, l_buf), lambda u: (u // per_n, u % per_n, 0))
    big_in = pl.BlockSpec((1, uw, l_buf), lambda u: (layer * n + u // per_n, u % per_n, 0))
    return pl.pallas_call(
        functools.partial(_ffn_cache_kernel, steps_per_tile=steps_per_tile, t_new=t_new),
        grid=(n_units,),
        in_specs=[tile(d), tile(mix.shape[1]), _const_spec(wo_b.shape), _const_spec((1, d)),
                  _const_spec(wg3.shape), _const_spec(wu3.shape), _const_spec(wd3.shape),
                  new, new, new, big_in, big_in, _const_spec(cnt.shape)],
        out_specs=[tile(d), new, big, big],
        out_shape=[jax.ShapeDtypeStruct((m, d), F32),
                   jax.ShapeDtypeStruct((n, t_new, D_ATTN), F32),
                   jax.ShapeDtypeStruct((n, D_ATTN, l_buf), F32),
                   jax.ShapeDtypeStruct((n, D_ATTN, l_buf), F32)],
        scratch_shapes=[pltpu.VMEM((tm, d), BF16)],
        compiler_params=_params(1),
        name="ffn_cache",
    )(x2d, mix, wo_b, g, wg3, wu3, wd3, q_s, k_s, v_s, cache_kt, cache_vt, jnp.asarray(cnt))


def _ffn_kernel(x_ref, *refs, n_mix):
    mix_refs = refs[0:n_mix]
    wo_ref, g_ref, wg_ref, wu_ref, wd_ref, y_ref = refs[n_mix:]
    x1 = x_ref[...]
    row = 0
    for mr in mix_refs:
        width = mr.shape[1]
        x1 = x1 + jnp.dot(mr[...].astype(BF16), wo_ref[row:row + width, :], preferred_element_type=F32)
        row += width
    h = _rms_rows(x1, g_ref[...]).astype(BF16)
    y_ref[...] = x1
    for j in range(wg_ref.shape[0]):
        y_ref[...] += _swiglu_chunk(h, wg_ref, wu_ref, wd_ref, j)


def _outproj_ffn(x2d, mix_parts, wo_b, g, wg3, wu3, wd3, tm):
    m, d = x2d.shape
    tm = min(tm, m)
    row = lambda i: (i, 0)
    return pl.pallas_call(
        functools.partial(_ffn_kernel, n_mix=len(mix_parts)),
        grid=(m // tm,),
        in_specs=[pl.BlockSpec((tm, d), row)] + [pl.BlockSpec((tm, p.shape[1]), row) for p in mix_parts]
        + [_const_spec(wo_b.shape), _const_spec((1, d)),
           _const_spec(wg3.shape), _const_spec(wu3.shape), _const_spec(wd3.shape)],
        out_specs=pl.BlockSpec((tm, d), row),
        out_shape=jax.ShapeDtypeStruct((m, d), F32),
        compiler_params=_params(1),
        name="outproj_ffn",
    )(x2d, *mix_parts, wo_b, g, wg3, wu3, wd3)


TOKEN_TILE = 512
MIX_TIME_TILE = 512
SAMPLE_CONV_BATCH = 32


def kernel(x_prompt, x_sample, cache_k, cache_v, state_conv, attn_norm_g, w_in, q_norm_g, k_norm_g,
           conv_w, conv_b, conv_ln_g, conv_ln_b, w_out, ffn_norm_g, w_gate, w_up, w_down):
    batch, seq, d_model = x_prompt.shape
    n_dec, t_new, _ = x_sample.shape
    depth = w_in.shape[0]
    d_conv = conv_w.shape[-1]
    l_buf = cache_k.shape[2]
    assert seq == WIN_MAX and l_buf == WIN_MAX and seq % (BLK * PATTERNS[-1][1]) == 0
    assert w_in.shape[-1] == 3 * D_ATTN + 2 * d_conv and d_conv == D_ATTN

    seg = jnp.asarray(np.kron(np.eye(N_HEADS), np.ones((HEAD_DIM, HEAD_DIM))), BF16)
    row = lambda v: v.reshape(1, -1)
    to_t = lambda c: jnp.transpose(c, (0, 1, 3, 4, 2)).reshape(depth * n_dec, D_ATTN, l_buf)
    from_t = lambda c, n: jnp.transpose(c.reshape(n, N_HEADS, HEAD_DIM, -1), (0, 3, 1, 2))
    cache_kt, cache_vt = to_t(cache_k), to_t(cache_v)
    state_t = jnp.transpose(state_conv, (0, 2, 1, 3)).reshape(depth * CONV_LEFT, n_dec, d_conv)

    y_p = x_prompt.reshape(batch * seq, d_model)
    y_s = x_sample.reshape(n_dec * t_new, d_model)
    outs = [[] for _ in range(6)]
    for l in range(depth):
        w_in_b = w_in[l].astype(BF16)
        wo_b = w_out[l].astype(BF16)
        wg3, wu3, wd3 = _chunk_major(w_gate[l], w_up[l], w_down[l])
        g_attn, g_ffn = row(attn_norm_g[l]), row(ffn_norm_g[l])
        qg = row(jnp.tile(q_norm_g[l], N_HEADS))
        kg = row(jnp.tile(k_norm_g[l], N_HEADS))
        cb, lg, lb = row(conv_b[l]), row(conv_ln_g[l]), row(conv_ln_b[l])

        q, k, v, u_s = _inproj_sample(y_s, g_attn, w_in_b, qg, kg, seg, TOKEN_TILE)
        c_s, ns = _sample_conv(state_t, u_s, conv_w[l], cb, lg, lb, l, t_new, min(SAMPLE_CONV_BATCH, n_dec))
        qkv, kt, vt, u = _inproj_prompt(y_p, g_attn, w_in_b, qg, kg, seg, batch, seq, TOKEN_TILE)
        attn = {dil: _band_attention(*qkv[dil], w, dil) for w, dil in PATTERNS}
        mix = _mix_prompt(u, conv_w[l], cb, lg, lb, attn, batch, seq, MIX_TIME_TILE)

        as3 = lambda t: t.reshape(n_dec, t_new, D_ATTN)
        y_p, attn_s, nk, nv = _ffn_with_cache(y_p, mix, wo_b, g_ffn, wg3, wu3, wd3,
                                              as3(q), as3(k), as3(v), cache_kt, cache_vt, l, TOKEN_TILE)
        y_s = _outproj_ffn(y_s, [attn_s.reshape(n_dec * t_new, D_ATTN), c_s],
                           wo_b, g_ffn, wg3, wu3, wd3, TOKEN_TILE)
        outs[0].append(from_t(kt, batch))
        outs[1].append(from_t(vt, batch))
        outs[2].append(u.reshape(batch, seq, d_conv)[:, seq - CONV_LEFT:])
        outs[3].append(from_t(nk, n_dec))
        outs[4].append(from_t(nv, n_dec))
        outs[5].append(jnp.transpose(ns, (1, 0, 2)))

    stack = lambda xs: xs[0][None] if len(xs) == 1 else jnp.stack(xs)
    return (y_p.reshape(batch, seq, d_model), y_s.reshape(n_dec, t_new, d_model)) + tuple(
        stack(o) for o in outs)
```

```python
import functools

import numpy as np
import jax
import jax.numpy as jnp
from jax import lax
from jax.experimental import pallas as pl
from jax.experimental.pallas import tpu as pltpu

N_HEADS = 8
HEAD_DIM = 64
D_ATTN = N_HEADS * HEAD_DIM
PATTERNS = ((128, 1), (512, 4), (2048, 16))
WIN_MAX = max(w for w, _ in PATTERNS)
CONV_WIDTH = 31
CONV_LEFT = CONV_WIDTH - 1
BLK = 128
RMS_EPS = 1e-6
LN_EPS = 1e-5
ATTN_SCALE = HEAD_DIM ** -0.5
LOG2E = 1.4426950408889634
NEG_BIG = -1e30

LANES = 128
MXU_COLS = 256
HEADS_PER_LANE_TILE = LANES // HEAD_DIM
VMEM_LIMIT_BYTES = 56 * 1024 * 1024

F32 = jnp.float32
BF16 = jnp.bfloat16


def _params(n_axes):
    return pltpu.CompilerParams(dimension_semantics=("arbitrary",) * n_axes,
                                vmem_limit_bytes=VMEM_LIMIT_BYTES)


def _const_spec(shape):
    return pl.BlockSpec(shape, lambda *_: (0,) * len(shape), pipeline_mode=pl.Buffered(1))


def _sigmoid(x):
    return 1.0 / (1.0 + jnp.exp(-x))


def _rms_rows(x, g):
    ms = jnp.mean(x * x, axis=-1, keepdims=True)
    return x * lax.rsqrt(ms + RMS_EPS) * g


def _class_lanes(c):
    return slice(c * D_ATTN, (c + 1) * D_ATTN)


def _tiles_scratch(rows, width):
    return pltpu.VMEM((width // LANES, rows, LANES), F32)


def _put_tiles(scr, x):
    for lt in range(scr.shape[0]):
        scr[lt] = x[:, lt * LANES:(lt + 1) * LANES]


def _get_tiles(scr):
    return jnp.concatenate([scr[lt] for lt in range(scr.shape[0])], axis=1)


def _get_rows(scr, start, n, stride):
    return jnp.concatenate([scr[lt, pl.ds(start, n, stride=stride), :] for lt in range(scr.shape[0])],
                           axis=1)


def _put_rows(scr, start, stride, x):
    for lt in range(scr.shape[0]):
        scr[lt, pl.ds(start, x.shape[0], stride=stride), :] = x[:, lt * LANES:(lt + 1) * LANES]


def _inproj_kernel(x_ref, g_ref, w_ref, qg_ref, kg_ref, seg_ref, *refs, dilations):
    h = _rms_rows(x_ref[...], g_ref[...]).astype(BF16)

    def proj(col):
        return jnp.dot(h, w_ref[:, col * D_ATTN:(col + 1) * D_ATTN], preferred_element_type=F32)

    def head_rms(t, gain):
        ss = jnp.dot((t * t).astype(BF16), seg_ref[...], preferred_element_type=F32)
        return t * lax.rsqrt(ss * (1.0 / HEAD_DIM) + RMS_EPS) * gain

    q = head_rms(proj(0), qg_ref[...]) * (ATTN_SCALE * LOG2E)
    k = head_rms(proj(1), kg_ref[...])
    v = proj(2)
    u = proj(3) * _sigmoid(proj(4))
    if not dilations:
        q_ref, k_ref, v_ref, u_ref = refs
        q_ref[...] = q
        k_ref[...] = k
        v_ref[...] = v
        u_ref[...] = u
        return

    n_views = 3 * len(dilations)
    nat = refs[0:3]
    views = refs[3:3 + n_views]
    kt_ref, vt_ref, u_ref = refs[3 + n_views:6 + n_views]
    scratch = refs[6 + n_views:]
    tm = x_ref.shape[0]
    u_ref[...] = u
    kt_ref[0] = k.T
    vt_ref[0] = v.T
    for i, t in enumerate((q, k, v)):
        nat[i][...] = t.astype(BF16)
        _put_tiles(scratch[i], t)
        for j, dil in enumerate(dilations):
            for c in range(dil):
                views[3 * j + i][0, :, _class_lanes(c)] = (
                    _get_rows(scratch[i], c, tm // dil, dil).astype(BF16))


def _inproj_sample(x2d, g, w_in_b, qg, kg, seg, tm):
    m, d = x2d.shape
    tm = min(tm, m)
    row = lambda i: (i, 0)
    out_f = jax.ShapeDtypeStruct((m, D_ATTN), F32)
    tile = pl.BlockSpec((tm, D_ATTN), row)
    return pl.pallas_call(
        functools.partial(_inproj_kernel, dilations=()),
        grid=(m // tm,),
        in_specs=[pl.BlockSpec((tm, d), row), _const_spec((1, d)), _const_spec(w_in_b.shape),
                  _const_spec((1, D_ATTN)), _const_spec((1, D_ATTN)), _const_spec((D_ATTN, D_ATTN))],
        out_specs=[tile] * 4,
        out_shape=[out_f] * 4,
        compiler_params=_params(1),
        name="inproj_sample",
    )(x2d, g, w_in_b, qg, kg, seg)


def _inproj_prompt(x2d, g, w_in_b, qg, kg, seg, batch, seq, tm):
    m, d = x2d.shape
    n_t = seq // tm
    dilations = tuple(dil for _, dil in PATTERNS if dil > 1)
    flat = lambda b, t: (b * n_t + t, 0)
    nat_spec = pl.BlockSpec((tm, D_ATTN), flat)
    nat_b = jax.ShapeDtypeStruct((m, D_ATTN), BF16)
    view_specs, view_shapes = [], []
    for dil in dilations:
        view_specs += [pl.BlockSpec((1, tm // dil, dil * D_ATTN), lambda b, t: (b, t, 0))] * 3
        view_shapes += [jax.ShapeDtypeStruct((batch, seq // dil, dil * D_ATTN), BF16)] * 3
    t_spec = pl.BlockSpec((1, D_ATTN, tm), lambda b, t: (b, 0, t))
    t_shape = jax.ShapeDtypeStruct((batch, D_ATTN, seq), F32)
    res = pl.pallas_call(
        functools.partial(_inproj_kernel, dilations=dilations),
        grid=(batch, n_t),
        in_specs=[pl.BlockSpec((tm, d), flat), _const_spec((1, d)), _const_spec(w_in_b.shape),
                  _const_spec((1, D_ATTN)), _const_spec((1, D_ATTN)), _const_spec((D_ATTN, D_ATTN))],
        out_specs=[nat_spec] * 3 + view_specs + [t_spec, t_spec, nat_spec],
        out_shape=[nat_b] * 3 + view_shapes + [t_shape, t_shape, jax.ShapeDtypeStruct((m, D_ATTN), F32)],
        scratch_shapes=[_tiles_scratch(tm, D_ATTN)] * 3,
        compiler_params=_params(2),
        name="inproj_prompt",
    )(x2d, g, w_in_b, qg, kg, seg)
    n_views = 3 * len(dilations)
    qkv = {1: tuple(t.reshape(batch, seq, D_ATTN) for t in res[0:3])}
    for j, dil in enumerate(dilations):
        qkv[dil] = tuple(res[3 + 3 * j:6 + 3 * j])
    kt, vt, u = res[3 + n_views:]
    return qkv, kt, vt, u


def _band_attn_kernel(q_ref, k_ref, v_ref, bias0_ref, bias_ref, o_ref, lse_ref, *, n_blocks, n_classes):
    lane = lax.broadcasted_iota(jnp.int32, (BLK, LANES), 1)
    upper = lane >= HEAD_DIM
    n_pairs = N_HEADS // HEADS_PER_LANE_TILE

    def block(cl, r0, k0, n_keys, bias):
        vps, scores = [], []
        for hp in range(n_pairs):
            lanes = slice(cl * D_ATTN + hp * LANES, cl * D_ATTN + (hp + 1) * LANES)
            qp = q_ref[0, pl.ds(r0, BLK), lanes]
            kp = k_ref[0, pl.ds(k0, n_keys), lanes]
            vps.append(v_ref[0, pl.ds(k0, n_keys), lanes])
            q2 = jnp.concatenate([jnp.where(~upper, qp, jnp.zeros_like(qp)),
                                  jnp.where(upper, qp, jnp.zeros_like(qp))], axis=0)
            s2 = lax.dot_general(q2, kp, (((1,), (1,)), ((), ())), preferred_element_type=F32)
            scores += [s2[hh * BLK:(hh + 1) * BLK] + bias for hh in range(HEADS_PER_LANE_TILE)]
        probs, inv_l = [], []
        lse_tile = jnp.zeros((BLK, LANES), F32)
        for h, s in enumerate(scores):
            m = jnp.max(s, axis=-1, keepdims=True)
            p = jnp.exp2(s - m)
            l = jnp.sum(p, axis=-1, keepdims=True)
            probs.append(p.astype(BF16))
            inv_l.append(1.0 / l)
            lse_tile = jnp.where(lane == h, m + jnp.log(l) * LOG2E, lse_tile)
        lse_ref[0, pl.ds(r0, BLK), cl * LANES:(cl + 1) * LANES] = lse_tile
        for hp in range(n_pairs):
            lanes = slice(cl * D_ATTN + hp * LANES, cl * D_ATTN + (hp + 1) * LANES)
            o2 = jnp.dot(jnp.concatenate(probs[2 * hp:2 * hp + 2], axis=0), vps[hp],
                         preferred_element_type=F32)
            o_lo, o_hi = (o2[hh * BLK:(hh + 1) * BLK] * inv_l[2 * hp + hh]
                          for hh in range(HEADS_PER_LANE_TILE))
            o_ref[0, pl.ds(r0, BLK), lanes] = jnp.where(upper, o_hi, o_lo).astype(o_ref.dtype)

    for cl in range(n_classes):
        block(cl, 0, 0, BLK, bias0_ref[...])
        if n_blocks > 1:
            def body(i, carry, cl=cl):
                r0 = pl.multiple_of(i * BLK, BLK)
                block(cl, r0, pl.multiple_of(r0 - BLK, BLK), 2 * BLK, bias_ref[...])
                return carry

            lax.fori_loop(1, n_blocks, body, 0)


def _band_biases(n_sub):
    qi = np.arange(BLK)[:, None]
    ki = np.arange(2 * BLK)[None, :]
    dist = BLK + qi - ki
    band = (dist >= 0) & (dist <= n_sub)
    d0 = qi - np.arange(BLK)[None, :]
    to_bias = lambda ok: np.where(ok, 0.0, NEG_BIG).astype(np.float32)
    return to_bias((d0 >= 0) & (d0 <= n_sub)), to_bias(band)


BAND_CLASSES_PER_STEP = 4


def _band_attention(q, k, v, window, dilation):
    batch, length, _ = q.shape
    cps = min(BAND_CLASSES_PER_STEP, dilation)
    bias0, bias = _band_biases(window // dilation)
    blk = pl.BlockSpec((1, length, cps * D_ATTN), lambda b, c: (b, 0, c))
    return pl.pallas_call(
        functools.partial(_band_attn_kernel, n_blocks=length // BLK, n_classes=cps),
        grid=(batch, dilation // cps),
        in_specs=[blk, blk, blk, _const_spec((BLK, BLK)), _const_spec((BLK, 2 * BLK))],
        out_specs=[blk, pl.BlockSpec((1, length, cps * LANES), lambda b, c: (b, 0, c))],
        out_shape=[jax.ShapeDtypeStruct(q.shape, BF16),
                   jax.ShapeDtypeStruct((batch, length, dilation * LANES), F32)],
        compiler_params=_params(2),
        name=f"band_attn_d{dilation}",
    )(q, k, v, jnp.asarray(bias0), jnp.asarray(bias))


CONV_PAD = 32
CONV_CHUNK = 64
SUBLANES = 8


def _mix_kernel(u_ref, up_ref, w_ref, b_ref, lg_ref, lb_ref, expand_ref, *refs, tt, dilations):
    n_pat = 1 + len(dilations)
    ins = refs[0:2 * n_pat]
    mix_ref, ext_ref = refs[2 * n_pat:2 * n_pat + 2]
    nat = refs[2 * n_pat + 2:]
    t = pl.program_id(1)
    rows = CONV_PAD + tt
    prev = up_ref[0]
    ext_ref[0, 0:CONV_PAD, :] = jnp.where(t > 0, prev, jnp.zeros_like(prev))
    ext_ref[0, CONV_PAD:rows, :] = u_ref[0]
    for s in range(1, SUBLANES):
        ext_ref[s, 0:rows - SUBLANES, :] = ext_ref[0, pl.ds(s, rows - SUBLANES), :]

    for j, dil in enumerate(dilations):
        for i, width in enumerate((D_ATTN, LANES)):
            src, dst = ins[2 + 2 * j + i], nat[2 * j + i]
            for c in range(dil):
                _put_rows(dst, c, dil, src[0, :, c * width:(c + 1) * width].astype(F32))
    os_ = [ins[0][...].astype(F32)] + [_get_tiles(nat[2 * j]) for j in range(len(dilations))]
    ls_ = [ins[1][...]] + [_get_tiles(nat[2 * j + 1]) for j in range(len(dilations))]
    mx = functools.reduce(jnp.maximum, ls_)
    es = [jnp.exp2(l - mx) for l in ls_]
    inv = 1.0 / sum(es)
    attn = None
    for e, o in zip(es, os_):
        w = e * inv
        hi = w.astype(BF16)
        lo = (w - hi.astype(F32)).astype(BF16)
        wide = (jnp.dot(hi, expand_ref[...], preferred_element_type=F32)
                + jnp.dot(lo, expand_ref[...], preferred_element_type=F32))
        attn = wide * o if attn is None else attn + wide * o
    mix_ref[:, 0:D_ATTN] = attn.astype(mix_ref.dtype)

    d_conv = u_ref.shape[-1]
    for ch in range(tt // CONV_CHUNK):
        acc = None
        for j in range(CONV_WIDTH):
            off = CONV_PAD - CONV_LEFT + j
            s = off % SUBLANES
            term = ext_ref[s, pl.ds(ch * CONV_CHUNK + off - s, CONV_CHUNK), :] * w_ref[j:j + 1, :]
            acc = term if acc is None else acc + term
        c = _ln_swish(acc + b_ref[...], lg_ref[...], lb_ref[...])
        mix_ref[ch * CONV_CHUNK:(ch + 1) * CONV_CHUNK, D_ATTN:D_ATTN + d_conv] = c.astype(mix_ref.dtype)


def _ln_swish(c, g, b):
    mu = jnp.mean(c, axis=-1, keepdims=True)
    cc = c - mu
    var = jnp.mean(cc * cc, axis=-1, keepdims=True)
    y = cc * lax.rsqrt(var + LN_EPS) * g + b
    return y * _sigmoid(y)


def _mix_prompt(u, conv_w, conv_b, ln_g, ln_b, attn, batch, seq, tt):
    d_conv = u.shape[-1]
    u3 = u.reshape(batch, seq, d_conv)
    n_t = seq // tt
    per = tt // CONV_PAD
    dilations = tuple(dil for _, dil in PATTERNS if dil > 1)
    cur = pl.BlockSpec((1, tt, d_conv), lambda b, t: (b, t, 0))
    prev = pl.BlockSpec((1, CONV_PAD, d_conv), lambda b, t: (b, jnp.maximum(t * per - 1, 0), 0))
    flat = lambda width: pl.BlockSpec((tt, width), lambda b, t: (b * n_t + t, 0))
    view = lambda dil, width: pl.BlockSpec((1, tt // dil, dil * width), lambda b, t: (b, t, 0))
    ins = [attn[1][0].reshape(batch * seq, D_ATTN), attn[1][1].reshape(batch * seq, LANES)]
    specs = [flat(D_ATTN), flat(LANES)]
    scratch = [pltpu.VMEM((SUBLANES, CONV_PAD + tt, d_conv), F32)]
    for dil in dilations:
        ins += list(attn[dil])
        specs += [view(dil, D_ATTN), view(dil, LANES)]
        scratch += [_tiles_scratch(tt, D_ATTN), _tiles_scratch(tt, LANES)]
    expand = np.zeros((LANES, D_ATTN), np.float32)
    expand[:N_HEADS] = np.kron(np.eye(N_HEADS), np.ones((1, HEAD_DIM)))
    return pl.pallas_call(
        functools.partial(_mix_kernel, tt=tt, dilations=dilations),
        grid=(batch, n_t),
        in_specs=[cur, prev, _const_spec(conv_w.shape), _const_spec((1, d_conv)),
                  _const_spec((1, d_conv)), _const_spec((1, d_conv)),
                  _const_spec((LANES, D_ATTN))] + specs,
        out_specs=pl.BlockSpec((tt, D_ATTN + d_conv), lambda b, t: (b * n_t + t, 0)),
        out_shape=jax.ShapeDtypeStruct((batch * seq, D_ATTN + d_conv), BF16),
        scratch_shapes=scratch,
        compiler_params=_params(2),
        name="mix_prompt",
    )(u3, u3, conv_w, conv_b, ln_g, ln_b, jnp.asarray(expand, BF16), *ins)


def _sample_conv_kernel(st_ref, u_ref, w_ref, b_ref, lg_ref, lb_ref, c_ref, ns_ref, us_ref, cs_ref,
                        *, nb, t_new):
    _put_tiles(us_ref, u_ref[...])
    new = [_get_rows(us_ref, t, nb, t_new) for t in range(t_new)]

    def ext(tau):
        return st_ref[tau] if tau < CONV_LEFT else new[tau - CONV_LEFT]

    for t in range(t_new):
        acc = None
        for j in range(CONV_WIDTH):
            term = ext(t + j) * w_ref[j:j + 1, :]
            acc = term if acc is None else acc + term
        c = _ln_swish(acc + b_ref[...], lg_ref[...], lb_ref[...])
        _put_rows(cs_ref, t, t_new, c)
    c_ref[...] = _get_tiles(cs_ref)
    for tau in range(CONV_LEFT):
        ns_ref[tau] = ext(tau + t_new)


def _sample_conv(state_t, u_s, conv_w, conv_b, ln_g, ln_b, layer, t_new, nb):
    _, n, d_conv = state_t.shape
    return pl.pallas_call(
        functools.partial(_sample_conv_kernel, nb=nb, t_new=t_new),
        grid=(n // nb,),
        in_specs=[pl.BlockSpec((CONV_LEFT, nb, d_conv), lambda i: (layer, i, 0)),
                  pl.BlockSpec((nb * t_new, d_conv), lambda i: (i, 0)),
                  _const_spec(conv_w.shape), _const_spec((1, d_conv)),
                  _const_spec((1, d_conv)), _const_spec((1, d_conv))],
        out_specs=[pl.BlockSpec((nb * t_new, d_conv), lambda i: (i, 0)),
                   pl.BlockSpec((CONV_LEFT, nb, d_conv), lambda i: (0, i, 0))],
        out_shape=[jax.ShapeDtypeStruct((n * t_new, d_conv), F32),
                   jax.ShapeDtypeStruct((CONV_LEFT, n, d_conv), F32)],
        scratch_shapes=[_tiles_scratch(nb * t_new, d_conv)] * 2,
        compiler_params=_params(1),
        name="sample_conv",
    )(state_t, u_s, conv_w, conv_b, ln_g, ln_b)


def _multiplicity(dist):
    dist = np.asarray(dist)
    c = np.zeros(dist.shape, np.float32)
    for window, dilation in PATTERNS:
        c += ((dist >= 0) & (dist <= window) & (dist % dilation == 0)).astype(np.float32)
    return c


def _shift_append(old, new_rows, out_ref, t_new):
    n_ch, l_buf = old.shape
    n_tiles = l_buf // LANES
    lane = lax.broadcasted_iota(jnp.int32, (n_ch, LANES), 1)
    keep = lane < LANES - t_new
    pad = jnp.concatenate([jnp.zeros((LANES - t_new, n_ch), F32), new_rows], axis=0)
    nxt = pad.T
    for j in reversed(range(n_tiles)):
        cur = pltpu.roll(old[:, j * LANES:(j + 1) * LANES], LANES - t_new, axis=1)
        out_ref[0, :, j * LANES:(j + 1) * LANES] = jnp.where(keep, cur, nxt)
        nxt = cur


CACHE_HEADS_PER_UNIT = 4


def _sample_unit(q_ref, kn_ref, vn_ref, ck, cv, cnt_ref, attn_ref, ok_ref, ov_ref, t_new):
    width = ck.shape[0]
    n_heads = width // HEAD_DIM
    rows = n_heads * t_new
    kn = kn_ref[0]
    vn = vn_ref[0]
    _shift_append(ck, kn, ok_ref, t_new)
    _shift_append(cv, vn, ov_ref, t_new)

    row = lax.broadcasted_iota(jnp.int32, (rows, width), 0)
    lane = lax.broadcasted_iota(jnp.int32, (rows, width), 1)
    own_head = (row // t_new) == (lane // HEAD_DIM)
    q_rep = jnp.concatenate([q_ref[0]] * n_heads, axis=0)
    q_bd = jnp.where(own_head, q_rep, 0.0)

    cnt = cnt_ref[...]
    s_c = jnp.dot(q_bd.astype(BF16), ck.astype(BF16), preferred_element_type=F32)
    s_c = jnp.where(cnt > 0.0, s_c, NEG_BIG)
    m = jnp.max(s_c, axis=-1, keepdims=True)

    t_of_row = lax.broadcasted_iota(jnp.int32, (rows, 1), 0) % t_new
    s_n, c_n = [], []
    for tp in range(t_new):
        d = t_of_row - tp
        c = jnp.zeros((rows, 1), F32)
        for _, dilation in PATTERNS:
            c = c + jnp.where((d >= 0) & (d % dilation == 0), 1.0, 0.0)
        s = jnp.sum(q_bd * kn[tp:tp + 1, :], axis=-1, keepdims=True)
        s = jnp.where(c > 0.0, s, NEG_BIG)
        m = jnp.maximum(m, s)
        s_n.append(s)
        c_n.append(c)

    p_c = cnt * jnp.exp2(s_c - m)
    l = jnp.sum(p_c, axis=-1, keepdims=True)
    acc = lax.dot_general(p_c.astype(BF16), cv.astype(BF16), (((1,), (1,)), ((), ())),
                          preferred_element_type=F32)
    for tp in range(t_new):
        p = c_n[tp] * jnp.exp2(s_n[tp] - m)
        l = l + p
        acc = acc + p * vn[tp:tp + 1, :]
    acc = jnp.where(own_head, acc * (1.0 / l), 0.0)
    out = acc[0:t_new, :]
    for h in range(1, n_heads):
        out = out + acc[h * t_new:(h + 1) * t_new, :]
    attn_ref[0] = out.astype(attn_ref.dtype)


CACHE_RING = 3


def _swiglu_chunk(h, wg_ref, wu_ref, wd_ref, j):
    gate = jnp.dot(h, wg_ref[j], preferred_element_type=F32)
    up = jnp.dot(h, wu_ref[j], preferred_element_type=F32)
    act = (gate * _sigmoid(gate) * up).astype(BF16)
    return jnp.dot(act, wd_ref[j], preferred_element_type=F32)


def _ffn_cache_kernel(x_ref, mix_ref, wo_ref, g_ref, wg_ref, wu_ref, wd_ref,
                      q_ref, kn_ref, vn_ref, ck_hbm, cv_hbm, cnt_ref,
                      y_ref, attn_ref, ok_ref, ov_ref, h_ref, kbuf, vbuf, sem,
                      *, steps_per_tile, t_new, first_row, units_per_row):
    u = pl.program_id(0)
    n_units = pl.num_programs(0)
    c = u % steps_per_tile
    base, extra = divmod(wg_ref.shape[0], steps_per_tile)
    uw = kbuf.shape[1]

    def fetch(unit, slot):
        row = first_row + unit // units_per_row
        h0 = pl.multiple_of((unit % units_per_row) * uw, uw)
        return (pltpu.make_async_copy(ck_hbm.at[row, pl.ds(h0, uw), :], kbuf.at[slot], sem.at[0, slot]),
                pltpu.make_async_copy(cv_hbm.at[row, pl.ds(h0, uw), :], vbuf.at[slot], sem.at[1, slot]))

    @pl.when(u == 0)
    def _():
        for unit in range(CACHE_RING - 1):
            for cp in fetch(unit, unit):
                cp.start()

    ahead = u + (CACHE_RING - 1)

    @pl.when(ahead < n_units)
    def _():
        for cp in fetch(ahead, ahead % CACHE_RING):
            cp.start()

    @pl.when(c == 0)
    def _():
        x1 = x_ref[...] + jnp.dot(mix_ref[...], wo_ref[...], preferred_element_type=F32)
        y_ref[...] = x1
        h_ref[...] = _rms_rows(x1, g_ref[...]).astype(BF16)

    for i in range(base):
        y_ref[...] += _swiglu_chunk(h_ref[...], wg_ref, wu_ref, wd_ref, c * base + i)
    if extra:
        first = steps_per_tile - extra

        @pl.when(c >= first)
        def _():
            y_ref[...] += _swiglu_chunk(h_ref[...], wg_ref, wu_ref, wd_ref, steps_per_tile * base + c - first)

    slot = u % CACHE_RING
    for cp in fetch(u, slot):
        cp.wait()
    _sample_unit(q_ref, kn_ref, vn_ref, kbuf[slot], vbuf[slot], cnt_ref, attn_ref, ok_ref, ov_ref, t_new)


def _chunk_major(wg, wu, wd):
    d, d_ff = wg.shape
    assert d_ff % MXU_COLS == 0
    n_chunks = d_ff // MXU_COLS
    by_cols = lambda w: w.reshape(d, n_chunks, MXU_COLS).transpose(1, 0, 2).astype(BF16)
    return by_cols(wg), by_cols(wu), wd.reshape(n_chunks, MXU_COLS, d).astype(BF16)


def _ffn_with_cache(x2d, mix, wo_b, g, wg3, wu3, wd3, q_s, k_s, v_s, cache_kt, cache_vt, layer, tm):
    m, d = x2d.shape
    n, t_new, _ = q_s.shape
    l_buf = cache_kt.shape[2]
    assert l_buf == WIN_MAX and t_new % SUBLANES == 0
    uw = CACHE_HEADS_PER_UNIT * HEAD_DIM
    per_n = D_ATTN // uw
    n_tiles, n_units = m // tm, n * per_n
    assert n_units % n_tiles == 0
    steps_per_tile = n_units // n_tiles
    assert wg3.shape[0] >= steps_per_tile and n_units >= CACHE_RING
    dist = l_buf + np.arange(t_new)[:, None] - np.arange(l_buf)[None, :]
    cnt = np.tile(_multiplicity(dist), (CACHE_HEADS_PER_UNIT, 1))

    tile = lambda width: pl.BlockSpec((tm, width), lambda u: (u // steps_per_tile, 0))
    new = pl.BlockSpec((1, t_new, uw), lambda u: (u // per_n, 0, u % per_n))
    big = pl.BlockSpec((1, uw, l_buf), lambda u: (u // per_n, u % per_n, 0))
    in_hbm = pl.BlockSpec(memory_space=pl.ANY)
    return pl.pallas_call(
        functools.partial(_ffn_cache_kernel, steps_per_tile=steps_per_tile, t_new=t_new,
                          first_row=layer * n, units_per_row=per_n),
        grid=(n_units,),
        in_specs=[tile(d), tile(mix.shape[1]), _const_spec(wo_b.shape), _const_spec((1, d)),
                  _const_spec(wg3.shape), _const_spec(wu3.shape), _const_spec(wd3.shape),
                  new, new, new, in_hbm, in_hbm, _const_spec(cnt.shape)],
        out_specs=[tile(d), new, big, big],
        out_shape=[jax.ShapeDtypeStruct((m, d), F32),
                   jax.ShapeDtypeStruct((n, t_new, D_ATTN), F32),
                   jax.ShapeDtypeStruct((n, D_ATTN, l_buf), F32),
                   jax.ShapeDtypeStruct((n, D_ATTN, l_buf), F32)],
        scratch_shapes=[pltpu.VMEM((tm, d), BF16),
                        pltpu.VMEM((CACHE_RING, uw, l_buf), F32), pltpu.VMEM((CACHE_RING, uw, l_buf), F32),
                        pltpu.SemaphoreType.DMA((2, CACHE_RING))],
        compiler_params=_params(1),
        name="ffn_cache",
    )(x2d, mix, wo_b, g, wg3, wu3, wd3, q_s, k_s, v_s, cache_kt, cache_vt, jnp.asarray(cnt))


def _ffn_kernel(x_ref, *refs, n_mix):
    mix_refs = refs[0:n_mix]
    wo_ref, g_ref, wg_ref, wu_ref, wd_ref, y_ref = refs[n_mix:]
    x1 = x_ref[...]
    row = 0
    for mr in mix_refs:
        width = mr.shape[1]
        x1 = x1 + jnp.dot(mr[...].astype(BF16), wo_ref[row:row + width, :], preferred_element_type=F32)
        row += width
    h = _rms_rows(x1, g_ref[...]).astype(BF16)
    y_ref[...] = x1
    for j in range(wg_ref.shape[0]):
        y_ref[...] += _swiglu_chunk(h, wg_ref, wu_ref, wd_ref, j)


def _outproj_ffn(x2d, mix_parts, wo_b, g, wg3, wu3, wd3, tm):
    m, d = x2d.shape
    tm = min(tm, m)
    row = lambda i: (i, 0)
    return pl.pallas_call(
        functools.partial(_ffn_kernel, n_mix=len(mix_parts)),
        grid=(m // tm,),
        in_specs=[pl.BlockSpec((tm, d), row)] + [pl.BlockSpec((tm, p.shape[1]), row) for p in mix_parts]
        + [_const_spec(wo_b.shape), _const_spec((1, d)),
           _const_spec(wg3.shape), _const_spec(wu3.shape), _const_spec(wd3.shape)],
        out_specs=pl.BlockSpec((tm, d), row),
        out_shape=jax.ShapeDtypeStruct((m, d), F32),
        compiler_params=_params(1),
        name="outproj_ffn",
    )(x2d, *mix_parts, wo_b, g, wg3, wu3, wd3)


TOKEN_TILE = 512
MIX_TIME_TILE = 512
SAMPLE_CONV_BATCH = 32


def kernel(x_prompt, x_sample, cache_k, cache_v, state_conv, attn_norm_g, w_in, q_norm_g, k_norm_g,
           conv_w, conv_b, conv_ln_g, conv_ln_b, w_out, ffn_norm_g, w_gate, w_up, w_down):
    batch, seq, d_model = x_prompt.shape
    n_dec, t_new, _ = x_sample.shape
    depth = w_in.shape[0]
    d_conv = conv_w.shape[-1]
    l_buf = cache_k.shape[2]
    assert seq == WIN_MAX and l_buf == WIN_MAX and seq % (BLK * PATTERNS[-1][1]) == 0
    assert w_in.shape[-1] == 3 * D_ATTN + 2 * d_conv and d_conv == D_ATTN

    seg = jnp.asarray(np.kron(np.eye(N_HEADS), np.ones((HEAD_DIM, HEAD_DIM))), BF16)
    row = lambda v: v.reshape(1, -1)
    to_t = lambda c: jnp.transpose(c, (0, 1, 3, 4, 2)).reshape(depth * n_dec, D_ATTN, l_buf)
    from_t = lambda c, n: jnp.transpose(c.reshape(n, N_HEADS, HEAD_DIM, -1), (0, 3, 1, 2))
    cache_kt, cache_vt = to_t(cache_k), to_t(cache_v)
    state_t = jnp.transpose(state_conv, (0, 2, 1, 3)).reshape(depth * CONV_LEFT, n_dec, d_conv)

    y_p = x_prompt.reshape(batch * seq, d_model)
    y_s = x_sample.reshape(n_dec * t_new, d_model)
    outs = [[] for _ in range(6)]
    for l in range(depth):
        w_in_b = w_in[l].astype(BF16)
        wo_b = w_out[l].astype(BF16)
        wg3, wu3, wd3 = _chunk_major(w_gate[l], w_up[l], w_down[l])
        g_attn, g_ffn = row(attn_norm_g[l]), row(ffn_norm_g[l])
        qg = row(jnp.tile(q_norm_g[l], N_HEADS))
        kg = row(jnp.tile(k_norm_g[l], N_HEADS))
        cb, lg, lb = row(conv_b[l]), row(conv_ln_g[l]), row(conv_ln_b[l])

        q, k, v, u_s = _inproj_sample(y_s, g_attn, w_in_b, qg, kg, seg, TOKEN_TILE)
        c_s, ns = _sample_conv(state_t, u_s, conv_w[l], cb, lg, lb, l, t_new, min(SAMPLE_CONV_BATCH, n_dec))
        qkv, kt, vt, u = _inproj_prompt(y_p, g_attn, w_in_b, qg, kg, seg, batch, seq, TOKEN_TILE)
        attn = {dil: _band_attention(*qkv[dil], w, dil) for w, dil in PATTERNS}
        mix = _mix_prompt(u, conv_w[l], cb, lg, lb, attn, batch, seq, MIX_TIME_TILE)

        as3 = lambda t: t.reshape(n_dec, t_new, D_ATTN)
        y_p, attn_s, nk, nv = _ffn_with_cache(y_p, mix, wo_b, g_ffn, wg3, wu3, wd3,
                                              as3(q), as3(k), as3(v), cache_kt, cache_vt, l, TOKEN_TILE)
        y_s = _outproj_ffn(y_s, [attn_s.reshape(n_dec * t_new, D_ATTN), c_s],
                           wo_b, g_ffn, wg3, wu3, wd3, TOKEN_TILE)
        outs[0].append(from_t(kt, batch))
        outs[1].append(from_t(vt, batch))
        outs[2].append(u.reshape(batch, seq, d_conv)[:, seq - CONV_LEFT:])
        outs[3].append(from_t(nk, n_dec))
        outs[4].append(from_t(nv, n_dec))
        outs[5].append(jnp.transpose(ns, (1, 0, 2)))

    stack = lambda xs: xs[0][None] if len(xs) == 1 else jnp.stack(xs)
    return (y_p.reshape(batch, seq, d_model), y_s.reshape(n_dec, t_new, d_model)) + tuple(
        stack(o) for o in outs)
```

```python
import functools

import numpy as np
import jax
import jax.numpy as jnp
from jax import lax
from jax.experimental import pallas as pl
from jax.experimental.pallas import tpu as pltpu

N_HEADS = 8
HEAD_DIM = 64
D_ATTN = N_HEADS * HEAD_DIM
PATTERNS = ((128, 1), (512, 4), (2048, 16))
WIN_MAX = max(w for w, _ in PATTERNS)
CONV_WIDTH = 31
CONV_LEFT = CONV_WIDTH - 1
BLK = 128
RMS_EPS = 1e-6
LN_EPS = 1e-5
ATTN_SCALE = HEAD_DIM ** -0.5
LOG2E = 1.4426950408889634
NEG_BIG = -1e30

LANES = 128
MXU_COLS = 256
HEADS_PER_LANE_TILE = LANES // HEAD_DIM
VMEM_LIMIT_BYTES = 56 * 1024 * 1024

F32 = jnp.float32
BF16 = jnp.bfloat16


def _params(n_axes):
    return pltpu.CompilerParams(dimension_semantics=("arbitrary",) * n_axes,
                                vmem_limit_bytes=VMEM_LIMIT_BYTES)


def _const_spec(shape):
    return pl.BlockSpec(shape, lambda *_: (0,) * len(shape), pipeline_mode=pl.Buffered(1))


def _sigmoid(x):
    return 1.0 / (1.0 + jnp.exp(-x))


def _rms_rows(x, g):
    ms = jnp.mean(x * x, axis=-1, keepdims=True)
    return x * lax.rsqrt(ms + RMS_EPS) * g


def _class_lanes(c):
    return slice(c * D_ATTN, (c + 1) * D_ATTN)


def _tiles_scratch(rows, width):
    return pltpu.VMEM((width // LANES, rows, LANES), F32)


def _put_tiles(scr, x):
    for lt in range(scr.shape[0]):
        scr[lt] = x[:, lt * LANES:(lt + 1) * LANES]


def _get_tiles(scr):
    return jnp.concatenate([scr[lt] for lt in range(scr.shape[0])], axis=1)


def _get_rows(scr, start, n, stride):
    return jnp.concatenate([scr[lt, pl.ds(start, n, stride=stride), :] for lt in range(scr.shape[0])],
                           axis=1)


def _put_rows(scr, start, stride, x):
    for lt in range(scr.shape[0]):
        scr[lt, pl.ds(start, x.shape[0], stride=stride), :] = x[:, lt * LANES:(lt + 1) * LANES]


def _inproj_kernel(x_ref, g_ref, w_ref, qg_ref, kg_ref, seg_ref, *refs, dilations):
    h = _rms_rows(x_ref[...], g_ref[...]).astype(BF16)

    def proj(col):
        return jnp.dot(h, w_ref[:, col * D_ATTN:(col + 1) * D_ATTN], preferred_element_type=F32)

    def head_rms(t, gain):
        ss = jnp.dot((t * t).astype(BF16), seg_ref[...], preferred_element_type=F32)
        return t * lax.rsqrt(ss * (1.0 / HEAD_DIM) + RMS_EPS) * gain

    q = head_rms(proj(0), qg_ref[...]) * (ATTN_SCALE * LOG2E)
    k = head_rms(proj(1), kg_ref[...])
    v = proj(2)
    u = proj(3) * _sigmoid(proj(4))
    if not dilations:
        q_ref, k_ref, v_ref, u_ref = refs
        q_ref[...] = q
        k_ref[...] = k
        v_ref[...] = v
        u_ref[...] = u
        return

    n_views = 3 * len(dilations)
    nat = refs[0:3]
    views = refs[3:3 + n_views]
    kt_ref, vt_ref, u_ref = refs[3 + n_views:6 + n_views]
    scratch = refs[6 + n_views:]
    tm = x_ref.shape[0]
    u_ref[...] = u
    kt_ref[0] = k.T
    vt_ref[0] = v.T
    for i, t in enumerate((q, k, v)):
        nat[i][...] = t.astype(BF16)
        _put_tiles(scratch[i], t)
        for j, dil in enumerate(dilations):
            for c in range(dil):
                views[3 * j + i][0, :, _class_lanes(c)] = (
                    _get_rows(scratch[i], c, tm // dil, dil).astype(BF16))


def _inproj_sample(x2d, g, w_in_b, qg, kg, seg, tm):
    m, d = x2d.shape
    tm = min(tm, m)
    row = lambda i: (i, 0)
    out_f = jax.ShapeDtypeStruct((m, D_ATTN), F32)
    tile = pl.BlockSpec((tm, D_ATTN), row)
    return pl.pallas_call(
        functools.partial(_inproj_kernel, dilations=()),
        grid=(m // tm,),
        in_specs=[pl.BlockSpec((tm, d), row), _const_spec((1, d)), _const_spec(w_in_b.shape),
                  _const_spec((1, D_ATTN)), _const_spec((1, D_ATTN)), _const_spec((D_ATTN, D_ATTN))],
        out_specs=[tile] * 4,
        out_shape=[out_f] * 4,
        compiler_params=_params(1),
        name="inproj_sample",
    )(x2d, g, w_in_b, qg, kg, seg)


def _inproj_prompt(x2d, g, w_in_b, qg, kg, seg, batch, seq, tm):
    m, d = x2d.shape
    n_t = seq // tm
    dilations = tuple(dil for _, dil in PATTERNS if dil > 1)
    flat = lambda b, t: (b * n_t + t, 0)
    nat_spec = pl.BlockSpec((tm, D_ATTN), flat)
    nat_b = jax.ShapeDtypeStruct((m, D_ATTN), BF16)
    view_specs, view_shapes = [], []
    for dil in dilations:
        view_specs += [pl.BlockSpec((1, tm // dil, dil * D_ATTN), lambda b, t: (b, t, 0))] * 3
        view_shapes += [jax.ShapeDtypeStruct((batch, seq // dil, dil * D_ATTN), BF16)] * 3
    t_spec = pl.BlockSpec((1, D_ATTN, tm), lambda b, t: (b, 0, t))
    t_shape = jax.ShapeDtypeStruct((batch, D_ATTN, seq), F32)
    res = pl.pallas_call(
        functools.partial(_inproj_kernel, dilations=dilations),
        grid=(batch, n_t),
        in_specs=[pl.BlockSpec((tm, d), flat), _const_spec((1, d)), _const_spec(w_in_b.shape),
                  _const_spec((1, D_ATTN)), _const_spec((1, D_ATTN)), _const_spec((D_ATTN, D_ATTN))],
        out_specs=[nat_spec] * 3 + view_specs + [t_spec, t_spec, nat_spec],
        out_shape=[nat_b] * 3 + view_shapes + [t_shape, t_shape, jax.ShapeDtypeStruct((m, D_ATTN), F32)],
        scratch_shapes=[_tiles_scratch(tm, D_ATTN)] * 3,
        compiler_params=_params(2),
        name="inproj_prompt",
    )(x2d, g, w_in_b, qg, kg, seg)
    n_views = 3 * len(dilations)
    qkv = {1: tuple(t.reshape(batch, seq, D_ATTN) for t in res[0:3])}
    for j, dil in enumerate(dilations):
        qkv[dil] = tuple(res[3 + 3 * j:6 + 3 * j])
    kt, vt, u = res[3 + n_views:]
    return qkv, kt, vt, u


def _band_attn_kernel(q_ref, k_ref, v_ref, bias0_ref, bias_ref, o_ref, lse_ref, *, n_blocks, n_classes):
    lane = lax.broadcasted_iota(jnp.int32, (BLK, LANES), 1)
    upper = lane >= HEAD_DIM
    n_pairs = N_HEADS // HEADS_PER_LANE_TILE

    def block(cl, r0, k0, n_keys, bias):
        vps, scores = [], []
        for hp in range(n_pairs):
            lanes = slice(cl * D_ATTN + hp * LANES, cl * D_ATTN + (hp + 1) * LANES)
            qp = q_ref[0, pl.ds(r0, BLK), lanes]
            kp = k_ref[0, pl.ds(k0, n_keys), lanes]
            vps.append(v_ref[0, pl.ds(k0, n_keys), lanes])
            q2 = jnp.concatenate([jnp.where(~upper, qp, jnp.zeros_like(qp)),
                                  jnp.where(upper, qp, jnp.zeros_like(qp))], axis=0)
            s2 = lax.dot_general(q2, kp, (((1,), (1,)), ((), ())), preferred_element_type=F32)
            scores += [s2[hh * BLK:(hh + 1) * BLK] + bias for hh in range(HEADS_PER_LANE_TILE)]
        probs, inv_l = [], []
        lse_tile = jnp.zeros((BLK, LANES), F32)
        for h, s in enumerate(scores):
            m = jnp.max(s, axis=-1, keepdims=True)
            p = jnp.exp2(s - m)
            l = jnp.sum(p, axis=-1, keepdims=True)
            probs.append(p.astype(BF16))
            inv_l.append(1.0 / l)
            lse_tile = jnp.where(lane == h, m + jnp.log(l) * LOG2E, lse_tile)
        lse_ref[0, pl.ds(r0, BLK), cl * LANES:(cl + 1) * LANES] = lse_tile
        for hp in range(n_pairs):
            lanes = slice(cl * D_ATTN + hp * LANES, cl * D_ATTN + (hp + 1) * LANES)
            o2 = jnp.dot(jnp.concatenate(probs[2 * hp:2 * hp + 2], axis=0), vps[hp],
                         preferred_element_type=F32)
            o_lo, o_hi = (o2[hh * BLK:(hh + 1) * BLK] * inv_l[2 * hp + hh]
                          for hh in range(HEADS_PER_LANE_TILE))
            o_ref[0, pl.ds(r0, BLK), lanes] = jnp.where(upper, o_hi, o_lo).astype(o_ref.dtype)

    for cl in range(n_classes):
        block(cl, 0, 0, BLK, bias0_ref[...])
        if n_blocks > 1:
            def body(i, carry, cl=cl):
                r0 = pl.multiple_of(i * BLK, BLK)
                block(cl, r0, pl.multiple_of(r0 - BLK, BLK), 2 * BLK, bias_ref[...])
                return carry

            lax.fori_loop(1, n_blocks, body, 0)


def _band_biases(n_sub):
    qi = np.arange(BLK)[:, None]
    ki = np.arange(2 * BLK)[None, :]
    dist = BLK + qi - ki
    band = (dist >= 0) & (dist <= n_sub)
    d0 = qi - np.arange(BLK)[None, :]
    to_bias = lambda ok: np.where(ok, 0.0, NEG_BIG).astype(np.float32)
    return to_bias((d0 >= 0) & (d0 <= n_sub)), to_bias(band)


BAND_CLASSES_PER_STEP = 4


def _band_attention(q, k, v, window, dilation):
    batch, length, _ = q.shape
    cps = min(BAND_CLASSES_PER_STEP, dilation)
    bias0, bias = _band_biases(window // dilation)
    blk = pl.BlockSpec((1, length, cps * D_ATTN), lambda b, c: (b, 0, c))
    return pl.pallas_call(
        functools.partial(_band_attn_kernel, n_blocks=length // BLK, n_classes=cps),
        grid=(batch, dilation // cps),
        in_specs=[blk, blk, blk, _const_spec((BLK, BLK)), _const_spec((BLK, 2 * BLK))],
        out_specs=[blk, pl.BlockSpec((1, length, cps * LANES), lambda b, c: (b, 0, c))],
        out_shape=[jax.ShapeDtypeStruct(q.shape, BF16),
                   jax.ShapeDtypeStruct((batch, length, dilation * LANES), F32)],
        compiler_params=_params(2),
        name=f"band_attn_d{dilation}",
    )(q, k, v, jnp.asarray(bias0), jnp.asarray(bias))


CONV_PAD = 32
CONV_CHUNK = 64
SUBLANES = 8


def _mix_kernel(u_ref, up_ref, w_ref, b_ref, lg_ref, lb_ref, expand_ref, *refs, tt, dilations):
    n_pat = 1 + len(dilations)
    ins = refs[0:2 * n_pat]
    mix_ref, ext_ref = refs[2 * n_pat:2 * n_pat + 2]
    nat = refs[2 * n_pat + 2:]
    t = pl.program_id(1)
    rows = CONV_PAD + tt
    prev = up_ref[0]
    ext_ref[0, 0:CONV_PAD, :] = jnp.where(t > 0, prev, jnp.zeros_like(prev))
    ext_ref[0, CONV_PAD:rows, :] = u_ref[0]
    for s in range(1, SUBLANES):
        ext_ref[s, 0:rows - SUBLANES, :] = ext_ref[0, pl.ds(s, rows - SUBLANES), :]

    for j, dil in enumerate(dilations):
        for i, width in enumerate((D_ATTN, LANES)):
            src, dst = ins[2 + 2 * j + i], nat[2 * j + i]
            for c in range(dil):
                _put_rows(dst, c, dil, src[0, :, c * width:(c + 1) * width].astype(F32))
    os_ = [ins[0][...].astype(F32)] + [_get_tiles(nat[2 * j]) for j in range(len(dilations))]
    ls_ = [ins[1][...]] + [_get_tiles(nat[2 * j + 1]) for j in range(len(dilations))]
    mx = functools.reduce(jnp.maximum, ls_)
    es = [jnp.exp2(l - mx) for l in ls_]
    inv = 1.0 / sum(es)
    attn = None
    for e, o in zip(es, os_):
        w = e * inv
        hi = w.astype(BF16)
        lo = (w - hi.astype(F32)).astype(BF16)
        wide = (jnp.dot(hi, expand_ref[...], preferred_element_type=F32)
                + jnp.dot(lo, expand_ref[...], preferred_element_type=F32))
        attn = wide * o if attn is None else attn + wide * o
    mix_ref[:, 0:D_ATTN] = attn.astype(mix_ref.dtype)

    d_conv = u_ref.shape[-1]
    for ch in range(tt // CONV_CHUNK):
        acc = None
        for j in range(CONV_WIDTH):
            off = CONV_PAD - CONV_LEFT + j
            s = off % SUBLANES
            term = ext_ref[s, pl.ds(ch * CONV_CHUNK + off - s, CONV_CHUNK), :] * w_ref[j:j + 1, :]
            acc = term if acc is None else acc + term
        c = _ln_swish(acc + b_ref[...], lg_ref[...], lb_ref[...])
        mix_ref[ch * CONV_CHUNK:(ch + 1) * CONV_CHUNK, D_ATTN:D_ATTN + d_conv] = c.astype(mix_ref.dtype)


def _ln_swish(c, g, b):
    mu = jnp.mean(c, axis=-1, keepdims=True)
    cc = c - mu
    var = jnp.mean(cc * cc, axis=-1, keepdims=True)
    y = cc * lax.rsqrt(var + LN_EPS) * g + b
    return y * _sigmoid(y)


def _mix_prompt(u, conv_w, conv_b, ln_g, ln_b, attn, batch, seq, tt):
    d_conv = u.shape[-1]
    u3 = u.reshape(batch, seq, d_conv)
    n_t = seq // tt
    per = tt // CONV_PAD
    dilations = tuple(dil for _, dil in PATTERNS if dil > 1)
    cur = pl.BlockSpec((1, tt, d_conv), lambda b, t: (b, t, 0))
    prev = pl.BlockSpec((1, CONV_PAD, d_conv), lambda b, t: (b, jnp.maximum(t * per - 1, 0), 0))
    flat = lambda width: pl.BlockSpec((tt, width), lambda b, t: (b * n_t + t, 0))
    view = lambda dil, width: pl.BlockSpec((1, tt // dil, dil * width), lambda b, t: (b, t, 0))
    ins = [attn[1][0].reshape(batch * seq, D_ATTN), attn[1][1].reshape(batch * seq, LANES)]
    specs = [flat(D_ATTN), flat(LANES)]
    scratch = [pltpu.VMEM((SUBLANES, CONV_PAD + tt, d_conv), F32)]
    for dil in dilations:
        ins += list(attn[dil])
        specs += [view(dil, D_ATTN), view(dil, LANES)]
        scratch += [_tiles_scratch(tt, D_ATTN), _tiles_scratch(tt, LANES)]
    expand = np.zeros((LANES, D_ATTN), np.float32)
    expand[:N_HEADS] = np.kron(np.eye(N_HEADS), np.ones((1, HEAD_DIM)))
    return pl.pallas_call(
        functools.partial(_mix_kernel, tt=tt, dilations=dilations),
        grid=(batch, n_t),
        in_specs=[cur, prev, _const_spec(conv_w.shape), _const_spec((1, d_conv)),
                  _const_spec((1, d_conv)), _const_spec((1, d_conv)),
                  _const_spec((LANES, D_ATTN))] + specs,
        out_specs=pl.BlockSpec((tt, D_ATTN + d_conv), lambda b, t: (b * n_t + t, 0)),
        out_shape=jax.ShapeDtypeStruct((batch * seq, D_ATTN + d_conv), BF16),
        scratch_shapes=scratch,
        compiler_params=_params(2),
        name="mix_prompt",
    )(u3, u3, conv_w, conv_b, ln_g, ln_b, jnp.asarray(expand, BF16), *ins)


def _sample_conv_kernel(st_ref, u_ref, w_ref, b_ref, lg_ref, lb_ref, c_ref, ns_ref, us_ref, cs_ref,
                        *, nb, t_new):
    _put_tiles(us_ref, u_ref[...])
    new = [_get_rows(us_ref, t, nb, t_new) for t in range(t_new)]

    def ext(tau):
        return st_ref[tau] if tau < CONV_LEFT else new[tau - CONV_LEFT]

    for t in range(t_new):
        acc = None
        for j in range(CONV_WIDTH):
            term = ext(t + j) * w_ref[j:j + 1, :]
            acc = term if acc is None else acc + term
        c = _ln_swish(acc + b_ref[...], lg_ref[...], lb_ref[...])
        _put_rows(cs_ref, t, t_new, c)
    c_ref[...] = _get_tiles(cs_ref)
    for tau in range(CONV_LEFT):
        ns_ref[tau] = ext(tau + t_new)


def _sample_conv(state_t, u_s, conv_w, conv_b, ln_g, ln_b, layer, t_new, nb):
    _, n, d_conv = state_t.shape
    return pl.pallas_call(
        functools.partial(_sample_conv_kernel, nb=nb, t_new=t_new),
        grid=(n // nb,),
        in_specs=[pl.BlockSpec((CONV_LEFT, nb, d_conv), lambda i: (layer, i, 0)),
                  pl.BlockSpec((nb * t_new, d_conv), lambda i: (i, 0)),
                  _const_spec(conv_w.shape), _const_spec((1, d_conv)),
                  _const_spec((1, d_conv)), _const_spec((1, d_conv))],
        out_specs=[pl.BlockSpec((nb * t_new, d_conv), lambda i: (i, 0)),
                   pl.BlockSpec((CONV_LEFT, nb, d_conv), lambda i: (0, i, 0))],
        out_shape=[jax.ShapeDtypeStruct((n * t_new, d_conv), F32),
                   jax.ShapeDtypeStruct((CONV_LEFT, n, d_conv), F32)],
        scratch_shapes=[_tiles_scratch(nb * t_new, d_conv)] * 2,
        compiler_params=_params(1),
        name="sample_conv",
    )(state_t, u_s, conv_w, conv_b, ln_g, ln_b)


def _multiplicity(dist):
    dist = np.asarray(dist)
    c = np.zeros(dist.shape, np.float32)
    for window, dilation in PATTERNS:
        c += ((dist >= 0) & (dist <= window) & (dist % dilation == 0)).astype(np.float32)
    return c


def _shift_append(old, new_rows, out_ref, t_new):
    n_ch, l_buf = old.shape
    n_tiles = l_buf // LANES
    lane = lax.broadcasted_iota(jnp.int32, (n_ch, LANES), 1)
    keep = lane < LANES - t_new
    pad = jnp.concatenate([jnp.zeros((LANES - t_new, n_ch), F32), new_rows], axis=0)
    nxt = pad.T
    for j in reversed(range(n_tiles)):
        cur = pltpu.roll(old[:, j * LANES:(j + 1) * LANES], LANES - t_new, axis=1)
        out_ref[0, :, j * LANES:(j + 1) * LANES] = jnp.where(keep, cur, nxt)
        nxt = cur


CACHE_HEADS_PER_UNIT = 4


def _sample_attend(q_ref, kn_ref, vn_ref, ck, cv, cnt_ref, attn_ref, t_new):
    width = ck.shape[0]
    n_heads = width // HEAD_DIM
    rows = n_heads * t_new
    kn = kn_ref[0]
    vn = vn_ref[0]

    row = lax.broadcasted_iota(jnp.int32, (rows, width), 0)
    lane = lax.broadcasted_iota(jnp.int32, (rows, width), 1)
    own_head = (row // t_new) == (lane // HEAD_DIM)
    q_rep = jnp.concatenate([q_ref[0]] * n_heads, axis=0)
    q_bd = jnp.where(own_head, q_rep, 0.0)

    cnt = cnt_ref[...]
    s_c = jnp.dot(q_bd.astype(BF16), ck.astype(BF16), preferred_element_type=F32)
    s_c = jnp.where(cnt > 0.0, s_c, NEG_BIG)
    m = jnp.max(s_c, axis=-1, keepdims=True)

    t_of_row = lax.broadcasted_iota(jnp.int32, (rows, 1), 0) % t_new
    s_n, c_n = [], []
    for tp in range(t_new):
        d = t_of_row - tp
        c = jnp.zeros((rows, 1), F32)
        for _, dilation in PATTERNS:
            c = c + jnp.where((d >= 0) & (d % dilation == 0), 1.0, 0.0)
        s = jnp.sum(q_bd * kn[tp:tp + 1, :], axis=-1, keepdims=True)
        s = jnp.where(c > 0.0, s, NEG_BIG)
        m = jnp.maximum(m, s)
        s_n.append(s)
        c_n.append(c)

    p_c = cnt * jnp.exp2(s_c - m)
    l = jnp.sum(p_c, axis=-1, keepdims=True)
    acc = lax.dot_general(p_c.astype(BF16), cv.astype(BF16), (((1,), (1,)), ((), ())),
                          preferred_element_type=F32)
    for tp in range(t_new):
        p = c_n[tp] * jnp.exp2(s_n[tp] - m)
        l = l + p
        acc = acc + p * vn[tp:tp + 1, :]
    acc = jnp.where(own_head, acc * (1.0 / l), 0.0)
    out = acc[0:t_new, :]
    for h in range(1, n_heads):
        out = out + acc[h * t_new:(h + 1) * t_new, :]
    attn_ref[0] = out.astype(attn_ref.dtype)


CACHE_RING = 3


def _swiglu_chunk(h, wg_ref, wu_ref, wd_ref, j):
    cols = pl.ds(pl.multiple_of(j * MXU_COLS, MXU_COLS), MXU_COLS)
    gate = jnp.dot(h, wg_ref[:, cols], preferred_element_type=F32)
    up = jnp.dot(h, wu_ref[:, cols], preferred_element_type=F32)
    act = (gate * _sigmoid(gate) * up).astype(BF16)
    return jnp.dot(act, wd_ref[j], preferred_element_type=F32)


def _ffn_cache_kernel(x_ref, mix_ref, wo_ref, g_ref, wg_ref, wu_ref, wd_ref,
                      q_ref, kn_ref, vn_ref, ck_hbm, cv_hbm, cnt_ref,
                      y_ref, attn_ref, ok_ref, ov_ref, h_ref, kbuf, vbuf, sem,
                      *, steps_per_tile, t_new, first_row, units_per_row):
    u = pl.program_id(0)
    n_units = pl.num_programs(0)
    c = u % steps_per_tile
    base, extra = divmod(wd_ref.shape[0], steps_per_tile)
    uw = kbuf.shape[1]

    def fetch(unit, slot):
        row = first_row + unit // units_per_row
        h0 = pl.multiple_of((unit % units_per_row) * uw, uw)
        return (pltpu.make_async_copy(ck_hbm.at[row, pl.ds(h0, uw), :], kbuf.at[slot], sem.at[0, slot]),
                pltpu.make_async_copy(cv_hbm.at[row, pl.ds(h0, uw), :], vbuf.at[slot], sem.at[1, slot]))

    @pl.when(u == 0)
    def _():
        for unit in range(CACHE_RING - 1):
            for cp in fetch(unit, unit):
                cp.start()

    ahead = u + (CACHE_RING - 1)

    @pl.when(ahead < n_units)
    def _():
        for cp in fetch(ahead, ahead % CACHE_RING):
            cp.start()

    @pl.when(c == 0)
    def _():
        x1 = x_ref[...] + jnp.dot(mix_ref[...], wo_ref[...], preferred_element_type=F32)
        y_ref[...] = x1
        h_ref[...] = _rms_rows(x1, g_ref[...]).astype(BF16)

    slot = u % CACHE_RING
    for cp in fetch(u, slot):
        cp.wait()
    ck, cv = kbuf[slot], vbuf[slot]
    _shift_append(ck, kn_ref[0], ok_ref, t_new)
    _shift_append(cv, vn_ref[0], ov_ref, t_new)
    _sample_attend(q_ref, kn_ref, vn_ref, ck, cv, cnt_ref, attn_ref, t_new)
    for i in range(base):
        y_ref[...] += _swiglu_chunk(h_ref[...], wg_ref, wu_ref, wd_ref, c * base + i)
    if extra:
        first = steps_per_tile - extra

        @pl.when(c >= first)
        def _():
            y_ref[...] += _swiglu_chunk(h_ref[...], wg_ref, wu_ref, wd_ref, steps_per_tile * base + c - first)


def _chunk_major(wg, wu, wd):
    d, d_ff = wg.shape
    assert d_ff % MXU_COLS == 0
    n_chunks = d_ff // MXU_COLS
    return wg.astype(BF16), wu.astype(BF16), wd.reshape(n_chunks, MXU_COLS, d).astype(BF16)


def _ffn_with_cache(x2d, mix, wo_b, g, wg3, wu3, wd3, q_s, k_s, v_s, cache_kt, cache_vt, layer, tm):
    m, d = x2d.shape
    n, t_new, _ = q_s.shape
    l_buf = cache_kt.shape[2]
    assert l_buf == WIN_MAX and t_new % SUBLANES == 0
    uw = CACHE_HEADS_PER_UNIT * HEAD_DIM
    per_n = D_ATTN // uw
    n_tiles, n_units = m // tm, n * per_n
    assert n_units % n_tiles == 0
    steps_per_tile = n_units // n_tiles
    assert wd3.shape[0] >= steps_per_tile and n_units >= CACHE_RING
    dist = l_buf + np.arange(t_new)[:, None] - np.arange(l_buf)[None, :]
    cnt = np.tile(_multiplicity(dist), (CACHE_HEADS_PER_UNIT, 1))

    tile = lambda width: pl.BlockSpec((tm, width), lambda u: (u // steps_per_tile, 0))
    new = pl.BlockSpec((1, t_new, uw), lambda u: (u // per_n, 0, u % per_n))
    big = pl.BlockSpec((1, uw, l_buf), lambda u: (u // per_n, u % per_n, 0))
    in_hbm = pl.BlockSpec(memory_space=pl.ANY)
    return pl.pallas_call(
        functools.partial(_ffn_cache_kernel, steps_per_tile=steps_per_tile, t_new=t_new,
                          first_row=layer * n, units_per_row=per_n),
        grid=(n_units,),
        in_specs=[tile(d), tile(mix.shape[1]), _const_spec(wo_b.shape), _const_spec((1, d)),
                  _const_spec(wg3.shape), _const_spec(wu3.shape), _const_spec(wd3.shape),
                  new, new, new, in_hbm, in_hbm, _const_spec(cnt.shape)],
        out_specs=[tile(d), new, big, big],
        out_shape=[jax.ShapeDtypeStruct((m, d), F32),
                   jax.ShapeDtypeStruct((n, t_new, D_ATTN), F32),
                   jax.ShapeDtypeStruct((n, D_ATTN, l_buf), F32),
                   jax.ShapeDtypeStruct((n, D_ATTN, l_buf), F32)],
        scratch_shapes=[pltpu.VMEM((tm, d), BF16),
                        pltpu.VMEM((CACHE_RING, uw, l_buf), F32), pltpu.VMEM((CACHE_RING, uw, l_buf), F32),
                        pltpu.SemaphoreType.DMA((2, CACHE_RING))],
        compiler_params=_params(1),
        name="ffn_cache",
    )(x2d, mix, wo_b, g, wg3, wu3, wd3, q_s, k_s, v_s, cache_kt, cache_vt, jnp.asarray(cnt))


def _ffn_kernel(x_ref, *refs, n_mix):
    mix_refs = refs[0:n_mix]
    wo_ref, g_ref, wg_ref, wu_ref, wd_ref, y_ref = refs[n_mix:]
    x1 = x_ref[...]
    row = 0
    for mr in mix_refs:
        width = mr.shape[1]
        x1 = x1 + jnp.dot(mr[...].astype(BF16), wo_ref[row:row + width, :], preferred_element_type=F32)
        row += width
    h = _rms_rows(x1, g_ref[...]).astype(BF16)
    y_ref[...] = x1
    for j in range(wd_ref.shape[0]):
        y_ref[...] += _swiglu_chunk(h, wg_ref, wu_ref, wd_ref, j)


def _outproj_ffn(x2d, mix_parts, wo_b, g, wg3, wu3, wd3, tm):
    m, d = x2d.shape
    tm = min(tm, m)
    row = lambda i: (i, 0)
    return pl.pallas_call(
        functools.partial(_ffn_kernel, n_mix=len(mix_parts)),
        grid=(m // tm,),
        in_specs=[pl.BlockSpec((tm, d), row)] + [pl.BlockSpec((tm, p.shape[1]), row) for p in mix_parts]
        + [_const_spec(wo_b.shape), _const_spec((1, d)),
           _const_spec(wg3.shape), _const_spec(wu3.shape), _const_spec(wd3.shape)],
        out_specs=pl.BlockSpec((tm, d), row),
        out_shape=jax.ShapeDtypeStruct((m, d), F32),
        compiler_params=_params(1),
        name="outproj_ffn",
    )(x2d, *mix_parts, wo_b, g, wg3, wu3, wd3)


TOKEN_TILE = 512
MIX_TIME_TILE = 512
SAMPLE_CONV_BATCH = 32


def kernel(x_prompt, x_sample, cache_k, cache_v, state_conv, attn_norm_g, w_in, q_norm_g, k_norm_g,
           conv_w, conv_b, conv_ln_g, conv_ln_b, w_out, ffn_norm_g, w_gate, w_up, w_down):
    batch, seq, d_model = x_prompt.shape
    n_dec, t_new, _ = x_sample.shape
    depth = w_in.shape[0]
    d_conv = conv_w.shape[-1]
    l_buf = cache_k.shape[2]
    assert seq == WIN_MAX and l_buf == WIN_MAX and seq % (BLK * PATTERNS[-1][1]) == 0
    assert w_in.shape[-1] == 3 * D_ATTN + 2 * d_conv and d_conv == D_ATTN

    seg = jnp.asarray(np.kron(np.eye(N_HEADS), np.ones((HEAD_DIM, HEAD_DIM))), BF16)
    row = lambda v: v.reshape(1, -1)
    to_t = lambda c: jnp.transpose(c, (0, 1, 3, 4, 2)).reshape(depth * n_dec, D_ATTN, l_buf)
    from_t = lambda c, n: jnp.transpose(c.reshape(n, N_HEADS, HEAD_DIM, -1), (0, 3, 1, 2))
    cache_kt, cache_vt = to_t(cache_k), to_t(cache_v)
    state_t = jnp.transpose(state_conv, (0, 2, 1, 3)).reshape(depth * CONV_LEFT, n_dec, d_conv)

    y_p = x_prompt.reshape(batch * seq, d_model)
    y_s = x_sample.reshape(n_dec * t_new, d_model)
    outs = [[] for _ in range(6)]
    for l in range(depth):
        w_in_b = w_in[l].astype(BF16)
        wo_b = w_out[l].astype(BF16)
        wg3, wu3, wd3 = _chunk_major(w_gate[l], w_up[l], w_down[l])
        g_attn, g_ffn = row(attn_norm_g[l]), row(ffn_norm_g[l])
        qg = row(jnp.tile(q_norm_g[l], N_HEADS))
        kg = row(jnp.tile(k_norm_g[l], N_HEADS))
        cb, lg, lb = row(conv_b[l]), row(conv_ln_g[l]), row(conv_ln_b[l])

        q, k, v, u_s = _inproj_sample(y_s, g_attn, w_in_b, qg, kg, seg, TOKEN_TILE)
        c_s, ns = _sample_conv(state_t, u_s, conv_w[l], cb, lg, lb, l, t_new, min(SAMPLE_CONV_BATCH, n_dec))
        qkv, kt, vt, u = _inproj_prompt(y_p, g_attn, w_in_b, qg, kg, seg, batch, seq, TOKEN_TILE)
        attn = {dil: _band_attention(*qkv[dil], w, dil) for w, dil in PATTERNS}
        mix = _mix_prompt(u, conv_w[l], cb, lg, lb, attn, batch, seq, MIX_TIME_TILE)

        as3 = lambda t: t.reshape(n_dec, t_new, D_ATTN)
        y_p, attn_s, nk, nv = _ffn_with_cache(y_p, mix, wo_b, g_ffn, wg3, wu3, wd3,
                                              as3(q), as3(k), as3(v), cache_kt, cache_vt, l, TOKEN_TILE)
        y_s = _outproj_ffn(y_s, [attn_s.reshape(n_dec * t_new, D_ATTN), c_s],
                           wo_b, g_ffn, wg3, wu3, wd3, TOKEN_TILE)
        outs[0].append(from_t(kt, batch))
        outs[1].append(from_t(vt, batch))
        outs[2].append(u.reshape(batch, seq, d_conv)[:, seq - CONV_LEFT:])
        outs[3].append(from_t(nk, n_dec))
        outs[4].append(from_t(nv, n_dec))
        outs[5].append(jnp.transpose(ns, (1, 0, 2)))

    stack = lambda xs: xs[0][None] if len(xs) == 1 else jnp.stack(xs)
    return (y_p.reshape(batch, seq, d_model), y_s.reshape(n_dec, t_new, d_model)) + tuple(
        stack(o) for o in outs)
```

```python
import functools

import numpy as np
import jax
import jax.numpy as jnp
from jax import lax
from jax.experimental import pallas as pl
from jax.experimental.pallas import tpu as pltpu

N_HEADS = 8
HEAD_DIM = 64
D_ATTN = N_HEADS * HEAD_DIM
PATTERNS = ((128, 1), (512, 4), (2048, 16))
WIN_MAX = max(w for w, _ in PATTERNS)
CONV_WIDTH = 31
CONV_LEFT = CONV_WIDTH - 1
BLK = 128
RMS_EPS = 1e-6
LN_EPS = 1e-5
ATTN_SCALE = HEAD_DIM ** -0.5
LOG2E = 1.4426950408889634
NEG_BIG = -1e30

LANES = 128
MXU_COLS = 256
HEADS_PER_LANE_TILE = LANES // HEAD_DIM
VMEM_LIMIT_BYTES = 56 * 1024 * 1024

F32 = jnp.float32
BF16 = jnp.bfloat16


def _params(n_axes):
    return pltpu.CompilerParams(dimension_semantics=("arbitrary",) * n_axes,
                                vmem_limit_bytes=VMEM_LIMIT_BYTES)


def _const_spec(shape):
    return pl.BlockSpec(shape, lambda *_: (0,) * len(shape), pipeline_mode=pl.Buffered(1))


def _sigmoid(x):
    return 1.0 / (1.0 + jnp.exp(-x))


def _rms_rows(x, g):
    ms = jnp.mean(x * x, axis=-1, keepdims=True)
    return x * lax.rsqrt(ms + RMS_EPS) * g


def _class_lanes(c):
    return slice(c * D_ATTN, (c + 1) * D_ATTN)


def _tiles_scratch(rows, width):
    return pltpu.VMEM((width // LANES, rows, LANES), F32)


def _put_tiles(scr, x):
    for lt in range(scr.shape[0]):
        scr[lt] = x[:, lt * LANES:(lt + 1) * LANES]


def _get_tiles(scr):
    return jnp.concatenate([scr[lt] for lt in range(scr.shape[0])], axis=1)


def _get_rows(scr, start, n, stride):
    return jnp.concatenate([scr[lt, pl.ds(start, n, stride=stride), :] for lt in range(scr.shape[0])],
                           axis=1)


def _put_rows(scr, start, stride, x):
    for lt in range(scr.shape[0]):
        scr[lt, pl.ds(start, x.shape[0], stride=stride), :] = x[:, lt * LANES:(lt + 1) * LANES]


def _inproj_kernel(x_ref, g_ref, w_ref, qg_ref, kg_ref, seg_ref, *refs, dilations):
    h = _rms_rows(x_ref[...], g_ref[...]).astype(BF16)

    def proj(col):
        return jnp.dot(h, w_ref[:, col * D_ATTN:(col + 1) * D_ATTN], preferred_element_type=F32)

    def head_rms(t, gain):
        ss = jnp.dot((t * t).astype(BF16), seg_ref[...], preferred_element_type=F32)
        return t * lax.rsqrt(ss * (1.0 / HEAD_DIM) + RMS_EPS) * gain

    make = (lambda: head_rms(proj(0), qg_ref[...]) * (ATTN_SCALE * LOG2E),
            lambda: head_rms(proj(1), kg_ref[...]),
            lambda: proj(2))
    if not dilations:
        for ref, f in zip(refs, make):
            ref[...] = f()
        refs[3][...] = proj(3) * _sigmoid(proj(4))
        return

    n_views = 3 * len(dilations)
    nat = refs[0:3]
    views = refs[3:3 + n_views]
    t_refs = (None,) + refs[3 + n_views:5 + n_views]
    u_ref = refs[5 + n_views]
    scratch = refs[6 + n_views:]
    tm = x_ref.shape[0]
    for i, f in enumerate(make):
        t = f()
        if t_refs[i] is not None:
            t_refs[i][0] = t.T
        nat[i][...] = t.astype(BF16)
        _put_tiles(scratch[i], t)
        for j, dil in enumerate(dilations):
            for c in range(dil):
                views[3 * j + i][0, :, _class_lanes(c)] = (
                    _get_rows(scratch[i], c, tm // dil, dil).astype(BF16))
    u_ref[...] = proj(3) * _sigmoid(proj(4))


def _inproj_sample(x2d, g, w_in_b, qg, kg, seg, tm):
    m, d = x2d.shape
    tm = min(tm, m)
    row = lambda i: (i, 0)
    out_f = jax.ShapeDtypeStruct((m, D_ATTN), F32)
    tile = pl.BlockSpec((tm, D_ATTN), row)
    return pl.pallas_call(
        functools.partial(_inproj_kernel, dilations=()),
        grid=(m // tm,),
        in_specs=[pl.BlockSpec((tm, d), row), _const_spec((1, d)), _const_spec(w_in_b.shape),
                  _const_spec((1, D_ATTN)), _const_spec((1, D_ATTN)), _const_spec((D_ATTN, D_ATTN))],
        out_specs=[tile] * 4,
        out_shape=[out_f] * 4,
        compiler_params=_params(1),
        name="inproj_sample",
    )(x2d, g, w_in_b, qg, kg, seg)


def _inproj_prompt(x2d, g, w_in_b, qg, kg, seg, batch, seq, tm):
    m, d = x2d.shape
    n_t = seq // tm
    dilations = tuple(dil for _, dil in PATTERNS if dil > 1)
    flat = lambda b, t: (b * n_t + t, 0)
    nat_spec = pl.BlockSpec((tm, D_ATTN), flat)
    nat_b = jax.ShapeDtypeStruct((m, D_ATTN), BF16)
    view_specs, view_shapes = [], []
    for dil in dilations:
        view_specs += [pl.BlockSpec((1, tm // dil, dil * D_ATTN), lambda b, t: (b, t, 0))] * 3
        view_shapes += [jax.ShapeDtypeStruct((batch, seq // dil, dil * D_ATTN), BF16)] * 3
    t_spec = pl.BlockSpec((1, D_ATTN, tm), lambda b, t: (b, 0, t))
    t_shape = jax.ShapeDtypeStruct((batch, D_ATTN, seq), F32)
    res = pl.pallas_call(
        functools.partial(_inproj_kernel, dilations=dilations),
        grid=(batch, n_t),
        in_specs=[pl.BlockSpec((tm, d), flat), _const_spec((1, d)), _const_spec(w_in_b.shape),
                  _const_spec((1, D_ATTN)), _const_spec((1, D_ATTN)), _const_spec((D_ATTN, D_ATTN))],
        out_specs=[nat_spec] * 3 + view_specs + [t_spec, t_spec, nat_spec],
        out_shape=[nat_b] * 3 + view_shapes + [t_shape, t_shape, jax.ShapeDtypeStruct((m, D_ATTN), F32)],
        scratch_shapes=[_tiles_scratch(tm, D_ATTN)] * 3,
        compiler_params=_params(2),
        name="inproj_prompt",
    )(x2d, g, w_in_b, qg, kg, seg)
    n_views = 3 * len(dilations)
    qkv = {1: tuple(t.reshape(batch, seq, D_ATTN) for t in res[0:3])}
    for j, dil in enumerate(dilations):
        qkv[dil] = tuple(res[3 + 3 * j:6 + 3 * j])
    kt, vt, u = res[3 + n_views:]
    return qkv, kt, vt, u


def _band_attn_kernel(q_ref, k_ref, v_ref, bias0_ref, bias_ref, o_ref, lse_ref, *, n_blocks, n_classes):
    lane = lax.broadcasted_iota(jnp.int32, (BLK, LANES), 1)
    upper = lane >= HEAD_DIM
    n_pairs = N_HEADS // HEADS_PER_LANE_TILE

    def block(cl, r0, k0, n_keys, bias):
        vps, scores = [], []
        for hp in range(n_pairs):
            lanes = slice(cl * D_ATTN + hp * LANES, cl * D_ATTN + (hp + 1) * LANES)
            qp = q_ref[0, pl.ds(r0, BLK), lanes]
            kp = k_ref[0, pl.ds(k0, n_keys), lanes]
            vps.append(v_ref[0, pl.ds(k0, n_keys), lanes])
            q2 = jnp.concatenate([jnp.where(~upper, qp, jnp.zeros_like(qp)),
                                  jnp.where(upper, qp, jnp.zeros_like(qp))], axis=0)
            s2 = lax.dot_general(q2, kp, (((1,), (1,)), ((), ())), preferred_element_type=F32)
            scores += [s2[hh * BLK:(hh + 1) * BLK] + bias for hh in range(HEADS_PER_LANE_TILE)]
        probs, inv_l = [], []
        lse_tile = jnp.zeros((BLK, LANES), F32)
        for h, s in enumerate(scores):
            m = jnp.max(s, axis=-1, keepdims=True)
            p = jnp.exp2(s - m)
            l = jnp.sum(p, axis=-1, keepdims=True)
            probs.append(p.astype(BF16))
            inv_l.append(1.0 / l)
            lse_tile = jnp.where(lane == h, m + jnp.log(l) * LOG2E, lse_tile)
        lse_ref[0, pl.ds(r0, BLK), cl * LANES:(cl + 1) * LANES] = lse_tile
        for hp in range(n_pairs):
            lanes = slice(cl * D_ATTN + hp * LANES, cl * D_ATTN + (hp + 1) * LANES)
            o2 = jnp.dot(jnp.concatenate(probs[2 * hp:2 * hp + 2], axis=0), vps[hp],
                         preferred_element_type=F32)
            o_lo, o_hi = (o2[hh * BLK:(hh + 1) * BLK] * inv_l[2 * hp + hh]
                          for hh in range(HEADS_PER_LANE_TILE))
            o_ref[0, pl.ds(r0, BLK), lanes] = jnp.where(upper, o_hi, o_lo).astype(o_ref.dtype)

    for cl in range(n_classes):
        block(cl, 0, 0, BLK, bias0_ref[...])
        if n_blocks > 1:
            def body(i, carry, cl=cl):
                r0 = pl.multiple_of(i * BLK, BLK)
                block(cl, r0, pl.multiple_of(r0 - BLK, BLK), 2 * BLK, bias_ref[...])
                return carry

            lax.fori_loop(1, n_blocks, body, 0)


def _band_biases(n_sub):
    qi = np.arange(BLK)[:, None]
    ki = np.arange(2 * BLK)[None, :]
    dist = BLK + qi - ki
    band = (dist >= 0) & (dist <= n_sub)
    d0 = qi - np.arange(BLK)[None, :]
    to_bias = lambda ok: np.where(ok, 0.0, NEG_BIG).astype(np.float32)
    return to_bias((d0 >= 0) & (d0 <= n_sub)), to_bias(band)


BAND_CLASSES_PER_STEP = 4


def _band_attention(q, k, v, window, dilation):
    batch, length, _ = q.shape
    cps = min(BAND_CLASSES_PER_STEP, dilation)
    bias0, bias = _band_biases(window // dilation)
    blk = pl.BlockSpec((1, length, cps * D_ATTN), lambda b, c: (b, 0, c))
    return pl.pallas_call(
        functools.partial(_band_attn_kernel, n_blocks=length // BLK, n_classes=cps),
        grid=(batch, dilation // cps),
        in_specs=[blk, blk, blk, _const_spec((BLK, BLK)), _const_spec((BLK, 2 * BLK))],
        out_specs=[blk, pl.BlockSpec((1, length, cps * LANES), lambda b, c: (b, 0, c))],
        out_shape=[jax.ShapeDtypeStruct(q.shape, BF16),
                   jax.ShapeDtypeStruct((batch, length, dilation * LANES), F32)],
        compiler_params=_params(2),
        name=f"band_attn_d{dilation}",
    )(q, k, v, jnp.asarray(bias0), jnp.asarray(bias))


CONV_PAD = 32
CONV_CHUNK = 64
SUBLANES = 8


def _mix_kernel(u_ref, up_ref, w_ref, b_ref, lg_ref, lb_ref, expand_ref, *refs, tt, dilations):
    n_pat = 1 + len(dilations)
    ins = refs[0:2 * n_pat]
    mix_ref, ext_ref = refs[2 * n_pat:2 * n_pat + 2]
    nat = refs[2 * n_pat + 2:]
    t = pl.program_id(1)
    rows = CONV_PAD + tt
    prev = up_ref[0]
    ext_ref[0, 0:CONV_PAD, :] = jnp.where(t > 0, prev, jnp.zeros_like(prev))
    ext_ref[0, CONV_PAD:rows, :] = u_ref[0]
    for s in range(1, SUBLANES):
        ext_ref[s, 0:rows - SUBLANES, :] = ext_ref[0, pl.ds(s, rows - SUBLANES), :]

    for j, dil in enumerate(dilations):
        for i, width in enumerate((D_ATTN, LANES)):
            src, dst = ins[2 + 2 * j + i], nat[2 * j + i]
            for c in range(dil):
                _put_rows(dst, c, dil, src[0, :, c * width:(c + 1) * width].astype(F32))
    os_ = [ins[0][...].astype(F32)] + [_get_tiles(nat[2 * j]) for j in range(len(dilations))]
    ls_ = [ins[1][...]] + [_get_tiles(nat[2 * j + 1]) for j in range(len(dilations))]
    mx = functools.reduce(jnp.maximum, ls_)
    es = [jnp.exp2(l - mx) for l in ls_]
    inv = 1.0 / sum(es)
    attn = None
    for e, o in zip(es, os_):
        w = e * inv
        hi = w.astype(BF16)
        lo = (w - hi.astype(F32)).astype(BF16)
        wide = (jnp.dot(hi, expand_ref[...], preferred_element_type=F32)
                + jnp.dot(lo, expand_ref[...], preferred_element_type=F32))
        attn = wide * o if attn is None else attn + wide * o
    mix_ref[:, 0:D_ATTN] = attn.astype(mix_ref.dtype)

    d_conv = u_ref.shape[-1]
    for ch in range(tt // CONV_CHUNK):
        acc = None
        for j in range(CONV_WIDTH):
            off = CONV_PAD - CONV_LEFT + j
            s = off % SUBLANES
            term = ext_ref[s, pl.ds(ch * CONV_CHUNK + off - s, CONV_CHUNK), :] * w_ref[j:j + 1, :]
            acc = term if acc is None else acc + term
        c = _ln_swish(acc + b_ref[...], lg_ref[...], lb_ref[...])
        mix_ref[ch * CONV_CHUNK:(ch + 1) * CONV_CHUNK, D_ATTN:D_ATTN + d_conv] = c.astype(mix_ref.dtype)


def _ln_swish(c, g, b):
    mu = jnp.mean(c, axis=-1, keepdims=True)
    cc = c - mu
    var = jnp.mean(cc * cc, axis=-1, keepdims=True)
    y = cc * lax.rsqrt(var + LN_EPS) * g + b
    return y * _sigmoid(y)


def _mix_prompt(u, conv_w, conv_b, ln_g, ln_b, attn, batch, seq, tt):
    d_conv = u.shape[-1]
    u3 = u.reshape(batch, seq, d_conv)
    n_t = seq // tt
    per = tt // CONV_PAD
    dilations = tuple(dil for _, dil in PATTERNS if dil > 1)
    cur = pl.BlockSpec((1, tt, d_conv), lambda b, t: (b, t, 0))
    prev = pl.BlockSpec((1, CONV_PAD, d_conv), lambda b, t: (b, jnp.maximum(t * per - 1, 0), 0))
    flat = lambda width: pl.BlockSpec((tt, width), lambda b, t: (b * n_t + t, 0))
    view = lambda dil, width: pl.BlockSpec((1, tt // dil, dil * width), lambda b, t: (b, t, 0))
    ins = [attn[1][0].reshape(batch * seq, D_ATTN), attn[1][1].reshape(batch * seq, LANES)]
    specs = [flat(D_ATTN), flat(LANES)]
    scratch = [pltpu.VMEM((SUBLANES, CONV_PAD + tt, d_conv), F32)]
    for dil in dilations:
        ins += list(attn[dil])
        specs += [view(dil, D_ATTN), view(dil, LANES)]
        scratch += [_tiles_scratch(tt, D_ATTN), _tiles_scratch(tt, LANES)]
    expand = np.zeros((LANES, D_ATTN), np.float32)
    expand[:N_HEADS] = np.kron(np.eye(N_HEADS), np.ones((1, HEAD_DIM)))
    return pl.pallas_call(
        functools.partial(_mix_kernel, tt=tt, dilations=dilations),
        grid=(batch, n_t),
        in_specs=[cur, prev, _const_spec(conv_w.shape), _const_spec((1, d_conv)),
                  _const_spec((1, d_conv)), _const_spec((1, d_conv)),
                  _const_spec((LANES, D_ATTN))] + specs,
        out_specs=pl.BlockSpec((tt, D_ATTN + d_conv), lambda b, t: (b * n_t + t, 0)),
        out_shape=jax.ShapeDtypeStruct((batch * seq, D_ATTN + d_conv), BF16),
        scratch_shapes=scratch,
        compiler_params=_params(2),
        name="mix_prompt",
    )(u3, u3, conv_w, conv_b, ln_g, ln_b, jnp.asarray(expand, BF16), *ins)


def _sample_conv_kernel(st_ref, u_ref, w_ref, b_ref, lg_ref, lb_ref, c_ref, ns_ref, us_ref, cs_ref,
                        *, nb, t_new):
    _put_tiles(us_ref, u_ref[...])
    new = [_get_rows(us_ref, t, nb, t_new) for t in range(t_new)]

    def ext(tau):
        return st_ref[tau] if tau < CONV_LEFT else new[tau - CONV_LEFT]

    for t in range(t_new):
        acc = None
        for j in range(CONV_WIDTH):
            term = ext(t + j) * w_ref[j:j + 1, :]
            acc = term if acc is None else acc + term
        c = _ln_swish(acc + b_ref[...], lg_ref[...], lb_ref[...])
        _put_rows(cs_ref, t, t_new, c)
    c_ref[...] = _get_tiles(cs_ref)
    for tau in range(CONV_LEFT):
        ns_ref[tau] = ext(tau + t_new)


def _sample_conv(state_t, u_s, conv_w, conv_b, ln_g, ln_b, layer, t_new, nb):
    _, n, d_conv = state_t.shape
    return pl.pallas_call(
        functools.partial(_sample_conv_kernel, nb=nb, t_new=t_new),
        grid=(n // nb,),
        in_specs=[pl.BlockSpec((CONV_LEFT, nb, d_conv), lambda i: (layer, i, 0)),
                  pl.BlockSpec((nb * t_new, d_conv), lambda i: (i, 0)),
                  _const_spec(conv_w.shape), _const_spec((1, d_conv)),
                  _const_spec((1, d_conv)), _const_spec((1, d_conv))],
        out_specs=[pl.BlockSpec((nb * t_new, d_conv), lambda i: (i, 0)),
                   pl.BlockSpec((CONV_LEFT, nb, d_conv), lambda i: (0, i, 0))],
        out_shape=[jax.ShapeDtypeStruct((n * t_new, d_conv), F32),
                   jax.ShapeDtypeStruct((CONV_LEFT, n, d_conv), F32)],
        scratch_shapes=[_tiles_scratch(nb * t_new, d_conv)] * 2,
        compiler_params=_params(1),
        name="sample_conv",
    )(state_t, u_s, conv_w, conv_b, ln_g, ln_b)


def _multiplicity(dist):
    dist = np.asarray(dist)
    c = np.zeros(dist.shape, np.float32)
    for window, dilation in PATTERNS:
        c += ((dist >= 0) & (dist <= window) & (dist % dilation == 0)).astype(np.float32)
    return c


def _shift_append(old, new_rows, out_ref, t_new):
    n_ch, l_buf = old.shape
    n_tiles = l_buf // LANES
    lane = lax.broadcasted_iota(jnp.int32, (n_ch, LANES), 1)
    keep = lane < LANES - t_new
    pad = jnp.concatenate([jnp.zeros((LANES - t_new, n_ch), F32), new_rows], axis=0)
    nxt = pad.T
    for j in reversed(range(n_tiles)):
        cur = pltpu.roll(old[:, j * LANES:(j + 1) * LANES], LANES - t_new, axis=1)
        out_ref[0, :, j * LANES:(j + 1) * LANES] = jnp.where(keep, cur, nxt)
        nxt = cur


CACHE_HEADS_PER_UNIT = 4


def _sample_attend(q_ref, kn_ref, vn_ref, ck, cv, cnt_ref, attn_ref, t_new):
    width = ck.shape[0]
    n_heads = width // HEAD_DIM
    rows = n_heads * t_new
    kn = kn_ref[0]
    vn = vn_ref[0]

    row = lax.broadcasted_iota(jnp.int32, (rows, width), 0)
    lane = lax.broadcasted_iota(jnp.int32, (rows, width), 1)
    own_head = (row // t_new) == (lane // HEAD_DIM)
    q_rep = jnp.concatenate([q_ref[0]] * n_heads, axis=0)
    q_bd = jnp.where(own_head, q_rep, 0.0)

    cnt = cnt_ref[...]
    s_c = jnp.dot(q_bd.astype(BF16), ck.astype(BF16), preferred_element_type=F32)
    s_c = jnp.where(cnt > 0.0, s_c, NEG_BIG)
    m = jnp.max(s_c, axis=-1, keepdims=True)

    t_of_row = lax.broadcasted_iota(jnp.int32, (rows, 1), 0) % t_new
    s_n, c_n = [], []
    for tp in range(t_new):
        d = t_of_row - tp
        c = jnp.zeros((rows, 1), F32)
        for _, dilation in PATTERNS:
            c = c + jnp.where((d >= 0) & (d % dilation == 0), 1.0, 0.0)
        s = jnp.sum(q_bd * kn[tp:tp + 1, :], axis=-1, keepdims=True)
        s = jnp.where(c > 0.0, s, NEG_BIG)
        m = jnp.maximum(m, s)
        s_n.append(s)
        c_n.append(c)

    p_c = cnt * jnp.exp2(s_c - m)
    l = jnp.sum(p_c, axis=-1, keepdims=True)
    acc = lax.dot_general(p_c.astype(BF16), cv.astype(BF16), (((1,), (1,)), ((), ())),
                          preferred_element_type=F32)
    for tp in range(t_new):
        p = c_n[tp] * jnp.exp2(s_n[tp] - m)
        l = l + p
        acc = acc + p * vn[tp:tp + 1, :]
    acc = jnp.where(own_head, acc * (1.0 / l), 0.0)
    out = acc[0:t_new, :]
    for h in range(1, n_heads):
        out = out + acc[h * t_new:(h + 1) * t_new, :]
    attn_ref[0] = out.astype(attn_ref.dtype)


CACHE_RING = 3


def _swiglu_chunk(h, wg_ref, wu_ref, wd_ref, j):
    cols = pl.ds(pl.multiple_of(j * MXU_COLS, MXU_COLS), MXU_COLS)
    gate = jnp.dot(h, wg_ref[:, cols], preferred_element_type=F32)
    up = jnp.dot(h, wu_ref[:, cols], preferred_element_type=F32)
    act = (gate * _sigmoid(gate) * up).astype(BF16)
    return jnp.dot(act, wd_ref[j], preferred_element_type=F32)


def _ffn_cache_kernel(x_ref, mix_ref, wo_ref, g_ref, wg_ref, wu_ref, wd_ref,
                      q_ref, kn_ref, vn_ref, ck_hbm, cv_hbm, cnt_ref,
                      y_ref, attn_ref, ok_ref, ov_ref, h_ref, kbuf, vbuf, sem,
                      *, steps_per_tile, t_new, first_row, units_per_row):
    u = pl.program_id(0)
    n_units = pl.num_programs(0)
    c = u % steps_per_tile
    base, extra = divmod(wd_ref.shape[0], steps_per_tile)
    uw = kbuf.shape[1]

    def fetch(unit, slot):
        row = first_row + unit // units_per_row
        h0 = pl.multiple_of((unit % units_per_row) * uw, uw)
        return (pltpu.make_async_copy(ck_hbm.at[row, pl.ds(h0, uw), :], kbuf.at[slot], sem.at[0, slot]),
                pltpu.make_async_copy(cv_hbm.at[row, pl.ds(h0, uw), :], vbuf.at[slot], sem.at[1, slot]))

    @pl.when(u == 0)
    def _():
        for unit in range(CACHE_RING - 1):
            for cp in fetch(unit, unit):
                cp.start()

    ahead = u + (CACHE_RING - 1)

    @pl.when(ahead < n_units)
    def _():
        for cp in fetch(ahead, ahead % CACHE_RING):
            cp.start()

    @pl.when(c == 0)
    def _():
        x1 = x_ref[...] + jnp.dot(mix_ref[...], wo_ref[...], preferred_element_type=F32)
        y_ref[...] = x1
        h_ref[...] = _rms_rows(x1, g_ref[...]).astype(BF16)

    slot = u % CACHE_RING
    for cp in fetch(u, slot):
        cp.wait()
    ck, cv = kbuf[slot], vbuf[slot]
    _shift_append(ck, kn_ref[0], ok_ref, t_new)
    _shift_append(cv, vn_ref[0], ov_ref, t_new)
    _sample_attend(q_ref, kn_ref, vn_ref, ck, cv, cnt_ref, attn_ref, t_new)
    for i in range(base):
        y_ref[...] += _swiglu_chunk(h_ref[...], wg_ref, wu_ref, wd_ref, c * base + i)
    for k in range(extra):
        @pl.when(c == (k + 1) * steps_per_tile // (extra + 1))
        def _(k=k):
            y_ref[...] += _swiglu_chunk(h_ref[...], wg_ref, wu_ref, wd_ref, steps_per_tile * base + k)


def _chunk_major(wg, wu, wd):
    d, d_ff = wg.shape
    assert d_ff % MXU_COLS == 0
    n_chunks = d_ff // MXU_COLS
    return wg.astype(BF16), wu.astype(BF16), wd.reshape(n_chunks, MXU_COLS, d).astype(BF16)


def _ffn_with_cache(x2d, mix, wo_b, g, wg3, wu3, wd3, q_s, k_s, v_s, cache_kt, cache_vt, layer, tm):
    m, d = x2d.shape
    n, t_new, _ = q_s.shape
    l_buf = cache_kt.shape[2]
    assert l_buf == WIN_MAX and t_new % SUBLANES == 0
    uw = CACHE_HEADS_PER_UNIT * HEAD_DIM
    per_n = D_ATTN // uw
    n_tiles, n_units = m // tm, n * per_n
    assert n_units % n_tiles == 0
    steps_per_tile = n_units // n_tiles
    assert wd3.shape[0] >= steps_per_tile and n_units >= CACHE_RING
    dist = l_buf + np.arange(t_new)[:, None] - np.arange(l_buf)[None, :]
    cnt = np.tile(_multiplicity(dist), (CACHE_HEADS_PER_UNIT, 1))

    tile = lambda width: pl.BlockSpec((tm, width), lambda u: (u // steps_per_tile, 0))
    new = pl.BlockSpec((1, t_new, uw), lambda u: (u // per_n, 0, u % per_n))
    big = pl.BlockSpec((1, uw, l_buf), lambda u: (u // per_n, u % per_n, 0))
    in_hbm = pl.BlockSpec(memory_space=pl.ANY)
    return pl.pallas_call(
        functools.partial(_ffn_cache_kernel, steps_per_tile=steps_per_tile, t_new=t_new,
                          first_row=layer * n, units_per_row=per_n),
        grid=(n_units,),
        in_specs=[tile(d), tile(mix.shape[1]), _const_spec(wo_b.shape), _const_spec((1, d)),
                  _const_spec(wg3.shape), _const_spec(wu3.shape), _const_spec(wd3.shape),
                  new, new, new, in_hbm, in_hbm, _const_spec(cnt.shape)],
        out_specs=[tile(d), new, big, big],
        out_shape=[jax.ShapeDtypeStruct((m, d), F32),
                   jax.ShapeDtypeStruct((n, t_new, D_ATTN), F32),
                   jax.ShapeDtypeStruct((n, D_ATTN, l_buf), F32),
                   jax.ShapeDtypeStruct((n, D_ATTN, l_buf), F32)],
        scratch_shapes=[pltpu.VMEM((tm, d), BF16),
                        pltpu.VMEM((CACHE_RING, uw, l_buf), F32), pltpu.VMEM((CACHE_RING, uw, l_buf), F32),
                        pltpu.SemaphoreType.DMA((2, CACHE_RING))],
        compiler_params=_params(1),
        name="ffn_cache",
    )(x2d, mix, wo_b, g, wg3, wu3, wd3, q_s, k_s, v_s, cache_kt, cache_vt, jnp.asarray(cnt))


def _ffn_kernel(x_ref, *refs, n_mix):
    mix_refs = refs[0:n_mix]
    wo_ref, g_ref, wg_ref, wu_ref, wd_ref, y_ref = refs[n_mix:]
    x1 = x_ref[...]
    row = 0
    for mr in mix_refs:
        width = mr.shape[1]
        x1 = x1 + jnp.dot(mr[...].astype(BF16), wo_ref[row:row + width, :], preferred_element_type=F32)
        row += width
    h = _rms_rows(x1, g_ref[...]).astype(BF16)
    y_ref[...] = x1
    for j in range(wd_ref.shape[0]):
        y_ref[...] += _swiglu_chunk(h, wg_ref, wu_ref, wd_ref, j)


def _outproj_ffn(x2d, mix_parts, wo_b, g, wg3, wu3, wd3, tm):
    m, d = x2d.shape
    tm = min(tm, m)
    row = lambda i: (i, 0)
    return pl.pallas_call(
        functools.partial(_ffn_kernel, n_mix=len(mix_parts)),
        grid=(m // tm,),
        in_specs=[pl.BlockSpec((tm, d), row)] + [pl.BlockSpec((tm, p.shape[1]), row) for p in mix_parts]
        + [_const_spec(wo_b.shape), _const_spec((1, d)),
           _const_spec(wg3.shape), _const_spec(wu3.shape), _const_spec(wd3.shape)],
        out_specs=pl.BlockSpec((tm, d), row),
        out_shape=jax.ShapeDtypeStruct((m, d), F32),
        compiler_params=_params(1),
        name="outproj_ffn",
    )(x2d, *mix_parts, wo_b, g, wg3, wu3, wd3)


TOKEN_TILE = 512
MIX_TIME_TILE = 512
SAMPLE_CONV_BATCH = 32


def kernel(x_prompt, x_sample, cache_k, cache_v, state_conv, attn_norm_g, w_in, q_norm_g, k_norm_g,
           conv_w, conv_b, conv_ln_g, conv_ln_b, w_out, ffn_norm_g, w_gate, w_up, w_down):
    batch, seq, d_model = x_prompt.shape
    n_dec, t_new, _ = x_sample.shape
    depth = w_in.shape[0]
    d_conv = conv_w.shape[-1]
    l_buf = cache_k.shape[2]
    assert seq == WIN_MAX and l_buf == WIN_MAX and seq % (BLK * PATTERNS[-1][1]) == 0
    assert w_in.shape[-1] == 3 * D_ATTN + 2 * d_conv and d_conv == D_ATTN

    seg = jnp.asarray(np.kron(np.eye(N_HEADS), np.ones((HEAD_DIM, HEAD_DIM))), BF16)
    row = lambda v: v.reshape(1, -1)
    to_t = lambda c: jnp.transpose(c, (0, 1, 3, 4, 2)).reshape(depth * n_dec, D_ATTN, l_buf)
    from_t = lambda c, n: jnp.transpose(c.reshape(n, N_HEADS, HEAD_DIM, -1), (0, 3, 1, 2))
    cache_kt, cache_vt = to_t(cache_k), to_t(cache_v)
    state_t = jnp.transpose(state_conv, (0, 2, 1, 3)).reshape(depth * CONV_LEFT, n_dec, d_conv)

    y_p = x_prompt.reshape(batch * seq, d_model)
    y_s = x_sample.reshape(n_dec * t_new, d_model)
    outs = [[] for _ in range(6)]
    for l in range(depth):
        w_in_b = w_in[l].astype(BF16)
        wo_b = w_out[l].astype(BF16)
        wg3, wu3, wd3 = _chunk_major(w_gate[l], w_up[l], w_down[l])
        g_attn, g_ffn = row(attn_norm_g[l]), row(ffn_norm_g[l])
        qg = row(jnp.tile(q_norm_g[l], N_HEADS))
        kg = row(jnp.tile(k_norm_g[l], N_HEADS))
        cb, lg, lb = row(conv_b[l]), row(conv_ln_g[l]), row(conv_ln_b[l])

        q, k, v, u_s = _inproj_sample(y_s, g_attn, w_in_b, qg, kg, seg, TOKEN_TILE)
        c_s, ns = _sample_conv(state_t, u_s, conv_w[l], cb, lg, lb, l, t_new, min(SAMPLE_CONV_BATCH, n_dec))
        qkv, kt, vt, u = _inproj_prompt(y_p, g_attn, w_in_b, qg, kg, seg, batch, seq, TOKEN_TILE)
        attn = {dil: _band_attention(*qkv[dil], w, dil) for w, dil in PATTERNS}
        mix = _mix_prompt(u, conv_w[l], cb, lg, lb, attn, batch, seq, MIX_TIME_TILE)

        as3 = lambda t: t.reshape(n_dec, t_new, D_ATTN)
        y_p, attn_s, nk, nv = _ffn_with_cache(y_p, mix, wo_b, g_ffn, wg3, wu3, wd3,
                                              as3(q), as3(k), as3(v), cache_kt, cache_vt, l, TOKEN_TILE)
        y_s = _outproj_ffn(y_s, [attn_s.reshape(n_dec * t_new, D_ATTN), c_s],
                           wo_b, g_ffn, wg3, wu3, wd3, TOKEN_TILE)
        outs[0].append(from_t(kt, batch))
        outs[1].append(from_t(vt, batch))
        outs[2].append(u.reshape(batch, seq, d_conv)[:, seq - CONV_LEFT:])
        outs[3].append(from_t(nk, n_dec))
        outs[4].append(from_t(nv, n_dec))
        outs[5].append(jnp.transpose(ns, (1, 0, 2)))

    stack = lambda xs: xs[0][None] if len(xs) == 1 else jnp.stack(xs)
    return (y_p.reshape(batch, seq, d_model), y_s.reshape(n_dec, t_new, d_model)) + tuple(
        stack(o) for o in outs)
```

```python
import functools

import numpy as np
import jax
import jax.numpy as jnp
from jax import lax
from jax.experimental import pallas as pl
from jax.experimental.pallas import tpu as pltpu

N_HEADS = 8
HEAD_DIM = 64
D_ATTN = N_HEADS * HEAD_DIM
PATTERNS = ((128, 1), (512, 4), (2048, 16))
WIN_MAX = max(w for w, _ in PATTERNS)
CONV_WIDTH = 31
CONV_LEFT = CONV_WIDTH - 1
BLK = 128
RMS_EPS = 1e-6
LN_EPS = 1e-5
ATTN_SCALE = HEAD_DIM ** -0.5
LOG2E = 1.4426950408889634
NEG_BIG = -1e30

LANES = 128
MXU_COLS = 256
HEADS_PER_LANE_TILE = LANES // HEAD_DIM
VMEM_LIMIT_BYTES = 56 * 1024 * 1024

F32 = jnp.float32
BF16 = jnp.bfloat16


def _params(n_axes):
    return pltpu.CompilerParams(dimension_semantics=("arbitrary",) * n_axes,
                                vmem_limit_bytes=VMEM_LIMIT_BYTES)


def _const_spec(shape):
    return pl.BlockSpec(shape, lambda *_: (0,) * len(shape), pipeline_mode=pl.Buffered(1))


def _sigmoid(x):
    return 1.0 / (1.0 + jnp.exp(-x))


def _rms_rows(x, g):
    ms = jnp.mean(x * x, axis=-1, keepdims=True)
    return x * lax.rsqrt(ms + RMS_EPS) * g


def _class_lanes(c):
    return slice(c * D_ATTN, (c + 1) * D_ATTN)


def _tiles_scratch(rows, width):
    return pltpu.VMEM((width // LANES, rows, LANES), F32)


def _put_tiles(scr, x):
    for lt in range(scr.shape[0]):
        scr[lt] = x[:, lt * LANES:(lt + 1) * LANES]


def _get_tiles(scr):
    return jnp.concatenate([scr[lt] for lt in range(scr.shape[0])], axis=1)


def _get_rows(scr, start, n, stride):
    return jnp.concatenate([scr[lt, pl.ds(start, n, stride=stride), :] for lt in range(scr.shape[0])],
                           axis=1)


def _put_rows(scr, start, stride, x):
    for lt in range(scr.shape[0]):
        scr[lt, pl.ds(start, x.shape[0], stride=stride), :] = x[:, lt * LANES:(lt + 1) * LANES]


def _inproj_kernel(x_ref, g_ref, w_ref, qg_ref, kg_ref, seg_ref, *refs, dilations):
    h = _rms_rows(x_ref[...], g_ref[...]).astype(BF16)

    def proj(col):
        return jnp.dot(h, w_ref[:, col * D_ATTN:(col + 1) * D_ATTN], preferred_element_type=F32)

    def head_rms(t, gain):
        ss = jnp.dot((t * t).astype(BF16), seg_ref[...], preferred_element_type=F32)
        return t * lax.rsqrt(ss * (1.0 / HEAD_DIM) + RMS_EPS) * gain

    make = (lambda: head_rms(proj(0), qg_ref[...]) * (ATTN_SCALE * LOG2E),
            lambda: head_rms(proj(1), kg_ref[...]),
            lambda: proj(2))
    if not dilations:
        for ref, f in zip(refs, make):
            ref[...] = f()
        refs[3][...] = proj(3) * _sigmoid(proj(4))
        return

    n_views = 3 * len(dilations)
    nat = refs[0:3]
    views = refs[3:3 + n_views]
    t_refs = (None,) + refs[3 + n_views:5 + n_views]
    u_ref = refs[5 + n_views]
    scratch = refs[6 + n_views:]
    tm = x_ref.shape[0]
    for i, f in enumerate(make):
        t = f()
        if t_refs[i] is not None:
            t_refs[i][0] = t.T
        nat[i][...] = t.astype(BF16)
        _put_tiles(scratch[i], t)
        for j, dil in enumerate(dilations):
            for c in range(dil):
                views[3 * j + i][0, :, _class_lanes(c)] = (
                    _get_rows(scratch[i], c, tm // dil, dil).astype(BF16))
    u_ref[...] = proj(3) * _sigmoid(proj(4))


def _inproj_sample(x2d, g, w_in_b, qg, kg, seg, tm):
    m, d = x2d.shape
    tm = min(tm, m)
    row = lambda i: (i, 0)
    out_f = jax.ShapeDtypeStruct((m, D_ATTN), F32)
    tile = pl.BlockSpec((tm, D_ATTN), row)
    return pl.pallas_call(
        functools.partial(_inproj_kernel, dilations=()),
        grid=(m // tm,),
        in_specs=[pl.BlockSpec((tm, d), row), _const_spec((1, d)), _const_spec(w_in_b.shape),
                  _const_spec((1, D_ATTN)), _const_spec((1, D_ATTN)), _const_spec((D_ATTN, D_ATTN))],
        out_specs=[tile] * 4,
        out_shape=[out_f] * 4,
        compiler_params=_params(1),
        name="inproj_sample",
    )(x2d, g, w_in_b, qg, kg, seg)


def _inproj_prompt(x2d, g, w_in_b, qg, kg, seg, batch, seq, tm):
    m, d = x2d.shape
    n_t = seq // tm
    dilations = tuple(dil for _, dil in PATTERNS if dil > 1)
    flat = lambda b, t: (b * n_t + t, 0)
    nat_spec = pl.BlockSpec((tm, D_ATTN), flat)
    nat_b = jax.ShapeDtypeStruct((m, D_ATTN), BF16)
    view_specs, view_shapes = [], []
    for dil in dilations:
        view_specs += [pl.BlockSpec((1, tm // dil, dil * D_ATTN), lambda b, t: (b, t, 0))] * 3
        view_shapes += [jax.ShapeDtypeStruct((batch, seq // dil, dil * D_ATTN), BF16)] * 3
    t_spec = pl.BlockSpec((1, D_ATTN, tm), lambda b, t: (b, 0, t))
    t_shape = jax.ShapeDtypeStruct((batch, D_ATTN, seq), F32)
    res = pl.pallas_call(
        functools.partial(_inproj_kernel, dilations=dilations),
        grid=(batch, n_t),
        in_specs=[pl.BlockSpec((tm, d), flat), _const_spec((1, d)), _const_spec(w_in_b.shape),
                  _const_spec((1, D_ATTN)), _const_spec((1, D_ATTN)), _const_spec((D_ATTN, D_ATTN))],
        out_specs=[nat_spec] * 3 + view_specs + [t_spec, t_spec, nat_spec],
        out_shape=[nat_b] * 3 + view_shapes + [t_shape, t_shape, jax.ShapeDtypeStruct((m, D_ATTN), F32)],
        scratch_shapes=[_tiles_scratch(tm, D_ATTN)] * 3,
        compiler_params=_params(2),
        name="inproj_prompt",
    )(x2d, g, w_in_b, qg, kg, seg)
    n_views = 3 * len(dilations)
    qkv = {1: tuple(t.reshape(batch, seq, D_ATTN) for t in res[0:3])}
    for j, dil in enumerate(dilations):
        qkv[dil] = tuple(res[3 + 3 * j:6 + 3 * j])
    kt, vt, u = res[3 + n_views:]
    return qkv, kt, vt, u


def _band_attn_kernel(q_ref, k_ref, v_ref, bias0_ref, bias_ref, o_ref, lse_ref, *, n_blocks, n_classes):
    lane = lax.broadcasted_iota(jnp.int32, (BLK, LANES), 1)
    upper = lane >= HEAD_DIM
    n_pairs = N_HEADS // HEADS_PER_LANE_TILE

    def block(cl, r0, k0, n_keys, bias):
        vps, scores = [], []
        for hp in range(n_pairs):
            lanes = slice(cl * D_ATTN + hp * LANES, cl * D_ATTN + (hp + 1) * LANES)
            qp = q_ref[0, pl.ds(r0, BLK), lanes]
            kp = k_ref[0, pl.ds(k0, n_keys), lanes]
            vps.append(v_ref[0, pl.ds(k0, n_keys), lanes])
            q2 = jnp.concatenate([jnp.where(~upper, qp, jnp.zeros_like(qp)),
                                  jnp.where(upper, qp, jnp.zeros_like(qp))], axis=0)
            s2 = lax.dot_general(q2, kp, (((1,), (1,)), ((), ())), preferred_element_type=F32)
            scores += [s2[hh * BLK:(hh + 1) * BLK] + bias for hh in range(HEADS_PER_LANE_TILE)]
        probs, inv_l = [], []
        lse_tile = jnp.zeros((BLK, LANES), F32)
        for h, s in enumerate(scores):
            m = jnp.max(s, axis=-1, keepdims=True)
            p = jnp.exp2(s - m)
            l = jnp.sum(p, axis=-1, keepdims=True)
            probs.append(p.astype(BF16))
            inv_l.append(1.0 / l)
            lse_tile = jnp.where(lane == h, m + jnp.log(l) * LOG2E, lse_tile)
        lse_ref[0, pl.ds(r0, BLK), cl * LANES:(cl + 1) * LANES] = lse_tile
        for hp in range(n_pairs):
            lanes = slice(cl * D_ATTN + hp * LANES, cl * D_ATTN + (hp + 1) * LANES)
            o2 = jnp.dot(jnp.concatenate(probs[2 * hp:2 * hp + 2], axis=0), vps[hp],
                         preferred_element_type=F32)
            o_lo, o_hi = (o2[hh * BLK:(hh + 1) * BLK] * inv_l[2 * hp + hh]
                          for hh in range(HEADS_PER_LANE_TILE))
            o_ref[0, pl.ds(r0, BLK), lanes] = jnp.where(upper, o_hi, o_lo).astype(o_ref.dtype)

    for cl in range(n_classes):
        block(cl, 0, 0, BLK, bias0_ref[...])
        if n_blocks > 1:
            def body(i, carry, cl=cl):
                r0 = pl.multiple_of(i * BLK, BLK)
                block(cl, r0, pl.multiple_of(r0 - BLK, BLK), 2 * BLK, bias_ref[...])
                return carry

            lax.fori_loop(1, n_blocks, body, 0)


def _band_biases(n_sub):
    qi = np.arange(BLK)[:, None]
    ki = np.arange(2 * BLK)[None, :]
    dist = BLK + qi - ki
    band = (dist >= 0) & (dist <= n_sub)
    d0 = qi - np.arange(BLK)[None, :]
    to_bias = lambda ok: np.where(ok, 0.0, NEG_BIG).astype(np.float32)
    return to_bias((d0 >= 0) & (d0 <= n_sub)), to_bias(band)


BAND_CLASSES_PER_STEP = 4


def _band_attention(q, k, v, window, dilation):
    batch, length, _ = q.shape
    cps = min(BAND_CLASSES_PER_STEP, dilation)
    bias0, bias = _band_biases(window // dilation)
    blk = pl.BlockSpec((1, length, cps * D_ATTN), lambda b, c: (b, 0, c))
    return pl.pallas_call(
        functools.partial(_band_attn_kernel, n_blocks=length // BLK, n_classes=cps),
        grid=(batch, dilation // cps),
        in_specs=[blk, blk, blk, _const_spec((BLK, BLK)), _const_spec((BLK, 2 * BLK))],
        out_specs=[blk, pl.BlockSpec((1, length, cps * LANES), lambda b, c: (b, 0, c))],
        out_shape=[jax.ShapeDtypeStruct(q.shape, BF16),
                   jax.ShapeDtypeStruct((batch, length, dilation * LANES), F32)],
        compiler_params=_params(2),
        name=f"band_attn_d{dilation}",
    )(q, k, v, jnp.asarray(bias0), jnp.asarray(bias))


CONV_PAD = 32
CONV_CHUNK = 64
SUBLANES = 8


def _mix_kernel(u_ref, up_ref, w_ref, b_ref, lg_ref, lb_ref, expand_ref, *refs, tt, dilations):
    n_pat = 1 + len(dilations)
    ins = refs[0:2 * n_pat]
    mix_ref, ext_ref = refs[2 * n_pat:2 * n_pat + 2]
    nat = refs[2 * n_pat + 2:]
    t = pl.program_id(1)
    rows = CONV_PAD + tt
    prev = up_ref[0]
    ext_ref[0, 0:CONV_PAD, :] = jnp.where(t > 0, prev, jnp.zeros_like(prev))
    ext_ref[0, CONV_PAD:rows, :] = u_ref[0]
    for s in range(1, SUBLANES):
        ext_ref[s, 0:rows - SUBLANES, :] = ext_ref[0, pl.ds(s, rows - SUBLANES), :]

    for j, dil in enumerate(dilations):
        for i, width in enumerate((D_ATTN, LANES)):
            src, dst = ins[2 + 2 * j + i], nat[2 * j + i]
            for c in range(dil):
                _put_rows(dst, c, dil, src[0, :, c * width:(c + 1) * width].astype(F32))
    os_ = [ins[0][...].astype(F32)] + [_get_tiles(nat[2 * j]) for j in range(len(dilations))]
    ls_ = [ins[1][...]] + [_get_tiles(nat[2 * j + 1]) for j in range(len(dilations))]
    mx = functools.reduce(jnp.maximum, ls_)
    es = [jnp.exp2(l - mx) for l in ls_]
    inv = 1.0 / sum(es)
    attn = None
    for e, o in zip(es, os_):
        w = e * inv
        hi = w.astype(BF16)
        lo = (w - hi.astype(F32)).astype(BF16)
        wide = (jnp.dot(hi, expand_ref[...], preferred_element_type=F32)
                + jnp.dot(lo, expand_ref[...], preferred_element_type=F32))
        attn = wide * o if attn is None else attn + wide * o
    mix_ref[:, 0:D_ATTN] = attn.astype(mix_ref.dtype)

    d_conv = u_ref.shape[-1]
    for ch in range(tt // CONV_CHUNK):
        acc = None
        for j in range(CONV_WIDTH):
            off = CONV_PAD - CONV_LEFT + j
            s = off % SUBLANES
            term = ext_ref[s, pl.ds(ch * CONV_CHUNK + off - s, CONV_CHUNK), :] * w_ref[j:j + 1, :]
            acc = term if acc is None else acc + term
        c = _ln_swish(acc + b_ref[...], lg_ref[...], lb_ref[...])
        mix_ref[ch * CONV_CHUNK:(ch + 1) * CONV_CHUNK, D_ATTN:D_ATTN + d_conv] = c.astype(mix_ref.dtype)


def _ln_swish(c, g, b):
    mu = jnp.mean(c, axis=-1, keepdims=True)
    cc = c - mu
    var = jnp.mean(cc * cc, axis=-1, keepdims=True)
    y = cc * lax.rsqrt(var + LN_EPS) * g + b
    return y * _sigmoid(y)


def _mix_prompt(u, conv_w, conv_b, ln_g, ln_b, attn, batch, seq, tt):
    d_conv = u.shape[-1]
    u3 = u.reshape(batch, seq, d_conv)
    n_t = seq // tt
    per = tt // CONV_PAD
    dilations = tuple(dil for _, dil in PATTERNS if dil > 1)
    cur = pl.BlockSpec((1, tt, d_conv), lambda b, t: (b, t, 0))
    prev = pl.BlockSpec((1, CONV_PAD, d_conv), lambda b, t: (b, jnp.maximum(t * per - 1, 0), 0))
    flat = lambda width: pl.BlockSpec((tt, width), lambda b, t: (b * n_t + t, 0))
    view = lambda dil, width: pl.BlockSpec((1, tt // dil, dil * width), lambda b, t: (b, t, 0))
    ins = [attn[1][0].reshape(batch * seq, D_ATTN), attn[1][1].reshape(batch * seq, LANES)]
    specs = [flat(D_ATTN), flat(LANES)]
    scratch = [pltpu.VMEM((SUBLANES, CONV_PAD + tt, d_conv), F32)]
    for dil in dilations:
        ins += list(attn[dil])
        specs += [view(dil, D_ATTN), view(dil, LANES)]
        scratch += [_tiles_scratch(tt, D_ATTN), _tiles_scratch(tt, LANES)]
    expand = np.zeros((LANES, D_ATTN), np.float32)
    expand[:N_HEADS] = np.kron(np.eye(N_HEADS), np.ones((1, HEAD_DIM)))
    return pl.pallas_call(
        functools.partial(_mix_kernel, tt=tt, dilations=dilations),
        grid=(batch, n_t),
        in_specs=[cur, prev, _const_spec(conv_w.shape), _const_spec((1, d_conv)),
                  _const_spec((1, d_conv)), _const_spec((1, d_conv)),
                  _const_spec((LANES, D_ATTN))] + specs,
        out_specs=pl.BlockSpec((tt, D_ATTN + d_conv), lambda b, t: (b * n_t + t, 0)),
        out_shape=jax.ShapeDtypeStruct((batch * seq, D_ATTN + d_conv), BF16),
        scratch_shapes=scratch,
        compiler_params=_params(2),
        name="mix_prompt",
    )(u3, u3, conv_w, conv_b, ln_g, ln_b, jnp.asarray(expand, BF16), *ins)


def _sample_conv_kernel(st_ref, u_ref, w_ref, b_ref, lg_ref, lb_ref, c_ref, ns_ref, us_ref, cs_ref,
                        *, nb, t_new):
    _put_tiles(us_ref, u_ref[...])
    new = [_get_rows(us_ref, t, nb, t_new) for t in range(t_new)]

    def ext(tau):
        return st_ref[tau] if tau < CONV_LEFT else new[tau - CONV_LEFT]

    for t in range(t_new):
        acc = None
        for j in range(CONV_WIDTH):
            term = ext(t + j) * w_ref[j:j + 1, :]
            acc = term if acc is None else acc + term
        c = _ln_swish(acc + b_ref[...], lg_ref[...], lb_ref[...])
        _put_rows(cs_ref, t, t_new, c)
    c_ref[...] = _get_tiles(cs_ref)
    for tau in range(CONV_LEFT):
        ns_ref[tau] = ext(tau + t_new)


def _sample_conv(state_t, u_s, conv_w, conv_b, ln_g, ln_b, layer, t_new, nb):
    _, n, d_conv = state_t.shape
    return pl.pallas_call(
        functools.partial(_sample_conv_kernel, nb=nb, t_new=t_new),
        grid=(n // nb,),
        in_specs=[pl.BlockSpec((CONV_LEFT, nb, d_conv), lambda i: (layer, i, 0)),
                  pl.BlockSpec((nb * t_new, d_conv), lambda i: (i, 0)),
                  _const_spec(conv_w.shape), _const_spec((1, d_conv)),
                  _const_spec((1, d_conv)), _const_spec((1, d_conv))],
        out_specs=[pl.BlockSpec((nb * t_new, d_conv), lambda i: (i, 0)),
                   pl.BlockSpec((CONV_LEFT, nb, d_conv), lambda i: (0, i, 0))],
        out_shape=[jax.ShapeDtypeStruct((n * t_new, d_conv), F32),
                   jax.ShapeDtypeStruct((CONV_LEFT, n, d_conv), F32)],
        scratch_shapes=[_tiles_scratch(nb * t_new, d_conv)] * 2,
        compiler_params=_params(1),
        name="sample_conv",
    )(state_t, u_s, conv_w, conv_b, ln_g, ln_b)


def _multiplicity(dist):
    dist = np.asarray(dist)
    c = np.zeros(dist.shape, np.float32)
    for window, dilation in PATTERNS:
        c += ((dist >= 0) & (dist <= window) & (dist % dilation == 0)).astype(np.float32)
    return c


def _shift_append(old, new_rows, out_ref, t_new):
    n_ch, l_buf = old.shape
    n_tiles = l_buf // LANES
    lane = lax.broadcasted_iota(jnp.int32, (n_ch, LANES), 1)
    keep = lane < LANES - t_new
    pad = jnp.concatenate([jnp.zeros((LANES - t_new, n_ch), F32), new_rows], axis=0)
    nxt = pad.T
    for j in reversed(range(n_tiles)):
        cur = pltpu.roll(old[:, j * LANES:(j + 1) * LANES], LANES - t_new, axis=1)
        out_ref[0, :, j * LANES:(j + 1) * LANES] = jnp.where(keep, cur, nxt)
        nxt = cur


CACHE_HEADS_PER_UNIT = 4


def _sample_attend(q_ref, kn_ref, vn_ref, ck, cv, cnt_ref, attn_ref, t_new):
    width = ck.shape[0]
    n_heads = width // HEAD_DIM
    rows = n_heads * t_new
    kn = kn_ref[0]
    vn = vn_ref[0]

    row = lax.broadcasted_iota(jnp.int32, (rows, width), 0)
    lane = lax.broadcasted_iota(jnp.int32, (rows, width), 1)
    own_head = (row // t_new) == (lane // HEAD_DIM)
    q_rep = jnp.concatenate([q_ref[0]] * n_heads, axis=0)
    q_bd = jnp.where(own_head, q_rep, 0.0)

    cnt = cnt_ref[...]
    s_c = jnp.dot(q_bd.astype(BF16), ck.astype(BF16), preferred_element_type=F32)
    s_c = jnp.where(cnt > 0.0, s_c, NEG_BIG)
    m = jnp.max(s_c, axis=-1, keepdims=True)

    t_of_row = lax.broadcasted_iota(jnp.int32, (rows, 1), 0) % t_new
    s_n, c_n = [], []
    for tp in range(t_new):
        d = t_of_row - tp
        c = jnp.zeros((rows, 1), F32)
        for _, dilation in PATTERNS:
            c = c + jnp.where((d >= 0) & (d % dilation == 0), 1.0, 0.0)
        s = jnp.sum(q_bd * kn[tp:tp + 1, :], axis=-1, keepdims=True)
        s = jnp.where(c > 0.0, s, NEG_BIG)
        m = jnp.maximum(m, s)
        s_n.append(s)
        c_n.append(c)

    p_c = cnt * jnp.exp2(s_c - m)
    l = jnp.sum(p_c, axis=-1, keepdims=True)
    acc = lax.dot_general(p_c.astype(BF16), cv.astype(BF16), (((1,), (1,)), ((), ())),
                          preferred_element_type=F32)
    for tp in range(t_new):
        p = c_n[tp] * jnp.exp2(s_n[tp] - m)
        l = l + p
        acc = acc + p * vn[tp:tp + 1, :]
    acc = jnp.where(own_head, acc * (1.0 / l), 0.0)
    out = acc[0:t_new, :]
    for h in range(1, n_heads):
        out = out + acc[h * t_new:(h + 1) * t_new, :]
    attn_ref[0] = out.astype(attn_ref.dtype)


CACHE_RING = 3


def _swiglu_chunk(h, wg_ref, wu_ref, wd_ref, j):
    cols = pl.ds(pl.multiple_of(j * MXU_COLS, MXU_COLS), MXU_COLS)
    gate = jnp.dot(h, wg_ref[:, cols], preferred_element_type=F32)
    up = jnp.dot(h, wu_ref[:, cols], preferred_element_type=F32)
    act = (gate * _sigmoid(gate) * up).astype(BF16)
    return jnp.dot(act, wd_ref[j], preferred_element_type=F32)


def _ffn_cache_kernel(x_ref, mix_ref, wo_ref, g_ref, wg_ref, wu_ref, wd_ref,
                      q_ref, kn_ref, vn_ref, ck_hbm, cv_hbm, cnt_ref,
                      y_ref, attn_ref, ok_ref, ov_ref, h_ref, kbuf, vbuf, sem,
                      *, steps_per_tile, t_new, first_row, units_per_row):
    u = pl.program_id(0)
    n_units = pl.num_programs(0)
    c = u % steps_per_tile
    base, extra = divmod(wd_ref.shape[0], steps_per_tile)
    uw = kbuf.shape[1]

    def fetch(unit, slot):
        row = first_row + unit // units_per_row
        h0 = pl.multiple_of((unit % units_per_row) * uw, uw)
        return (pltpu.make_async_copy(ck_hbm.at[row, pl.ds(h0, uw), :], kbuf.at[slot], sem.at[0, slot]),
                pltpu.make_async_copy(cv_hbm.at[row, pl.ds(h0, uw), :], vbuf.at[slot], sem.at[1, slot]))

    @pl.when(u == 0)
    def _():
        for unit in range(CACHE_RING - 1):
            for cp in fetch(unit, unit):
                cp.start()

    ahead = u + (CACHE_RING - 1)

    @pl.when(ahead < n_units)
    def _():
        for cp in fetch(ahead, ahead % CACHE_RING):
            cp.start()

    @pl.when(c == 0)
    def _():
        x1 = x_ref[...] + jnp.dot(mix_ref[...], wo_ref[...], preferred_element_type=F32)
        y_ref[...] = x1
        h_ref[...] = _rms_rows(x1, g_ref[...]).astype(BF16)

    slot = u % CACHE_RING
    for cp in fetch(u, slot):
        cp.wait()
    ck, cv = kbuf[slot], vbuf[slot]
    _shift_append(ck, kn_ref[0], ok_ref, t_new)
    _shift_append(cv, vn_ref[0], ov_ref, t_new)
    _sample_attend(q_ref, kn_ref, vn_ref, ck, cv, cnt_ref, attn_ref, t_new)
    for i in range(base):
        y_ref[...] += _swiglu_chunk(h_ref[...], wg_ref, wu_ref, wd_ref, c * base + i)
    if extra:
        first = steps_per_tile - extra

        @pl.when(c >= first)
        def _():
            y_ref[...] += _swiglu_chunk(h_ref[...], wg_ref, wu_ref, wd_ref, steps_per_tile * base + c - first)


def _chunk_major(wg, wu, wd):
    d, d_ff = wg.shape
    assert d_ff % MXU_COLS == 0
    n_chunks = d_ff // MXU_COLS
    return wg.astype(BF16), wu.astype(BF16), wd.reshape(n_chunks, MXU_COLS, d).astype(BF16)


def _ffn_with_cache(x2d, mix, wo_b, g, wg3, wu3, wd3, q_s, k_s, v_s, cache_kt, cache_vt, layer, tm):
    m, d = x2d.shape
    n, t_new, _ = q_s.shape
    l_buf = cache_kt.shape[2]
    assert l_buf == WIN_MAX and t_new % SUBLANES == 0
    uw = CACHE_HEADS_PER_UNIT * HEAD_DIM
    per_n = D_ATTN // uw
    n_tiles, n_units = m // tm, n * per_n
    assert n_units % n_tiles == 0
    steps_per_tile = n_units // n_tiles
    assert wd3.shape[0] >= steps_per_tile and n_units >= CACHE_RING
    dist = l_buf + np.arange(t_new)[:, None] - np.arange(l_buf)[None, :]
    cnt = np.tile(_multiplicity(dist), (CACHE_HEADS_PER_UNIT, 1))

    tile = lambda width: pl.BlockSpec((tm, width), lambda u: (u // steps_per_tile, 0))
    new = pl.BlockSpec((1, t_new, uw), lambda u: (u // per_n, 0, u % per_n))
    big = pl.BlockSpec((1, uw, l_buf), lambda u: (u // per_n, u % per_n, 0))
    in_hbm = pl.BlockSpec(memory_space=pl.ANY)
    return pl.pallas_call(
        functools.partial(_ffn_cache_kernel, steps_per_tile=steps_per_tile, t_new=t_new,
                          first_row=layer * n, units_per_row=per_n),
        grid=(n_units,),
        in_specs=[tile(d), tile(mix.shape[1]), _const_spec(wo_b.shape), _const_spec((1, d)),
                  _const_spec(wg3.shape), _const_spec(wu3.shape), _const_spec(wd3.shape),
                  new, new, new, in_hbm, in_hbm, _const_spec(cnt.shape)],
        out_specs=[tile(d), new, big, big],
        out_shape=[jax.ShapeDtypeStruct((m, d), F32),
                   jax.ShapeDtypeStruct((n, t_new, D_ATTN), F32),
                   jax.ShapeDtypeStruct((n, D_ATTN, l_buf), F32),
                   jax.ShapeDtypeStruct((n, D_ATTN, l_buf), F32)],
        scratch_shapes=[pltpu.VMEM((tm, d), BF16),
                        pltpu.VMEM((CACHE_RING, uw, l_buf), F32), pltpu.VMEM((CACHE_RING, uw, l_buf), F32),
                        pltpu.SemaphoreType.DMA((2, CACHE_RING))],
        compiler_params=_params(1),
        name="ffn_cache",
    )(x2d, mix, wo_b, g, wg3, wu3, wd3, q_s, k_s, v_s, cache_kt, cache_vt, jnp.asarray(cnt))


def _ffn_kernel(x_ref, *refs, n_mix):
    mix_refs = refs[0:n_mix]
    wo_ref, g_ref, wg_ref, wu_ref, wd_ref, y_ref = refs[n_mix:]
    x1 = x_ref[...]
    row = 0
    for mr in mix_refs:
        width = mr.shape[1]
        x1 = x1 + jnp.dot(mr[...].astype(BF16), wo_ref[row:row + width, :], preferred_element_type=F32)
        row += width
    h = _rms_rows(x1, g_ref[...]).astype(BF16)
    y_ref[...] = x1
    for j in range(wd_ref.shape[0]):
        y_ref[...] += _swiglu_chunk(h, wg_ref, wu_ref, wd_ref, j)


def _outproj_ffn(x2d, mix_parts, wo_b, g, wg3, wu3, wd3, tm):
    m, d = x2d.shape
    tm = min(tm, m)
    row = lambda i: (i, 0)
    return pl.pallas_call(
        functools.partial(_ffn_kernel, n_mix=len(mix_parts)),
        grid=(m // tm,),
        in_specs=[pl.BlockSpec((tm, d), row)] + [pl.BlockSpec((tm, p.shape[1]), row) for p in mix_parts]
        + [_const_spec(wo_b.shape), _const_spec((1, d)),
           _const_spec(wg3.shape), _const_spec(wu3.shape), _const_spec(wd3.shape)],
        out_specs=pl.BlockSpec((tm, d), row),
        out_shape=jax.ShapeDtypeStruct((m, d), F32),
        compiler_params=_params(1),
        name="outproj_ffn",
    )(x2d, *mix_parts, wo_b, g, wg3, wu3, wd3)


TOKEN_TILE = 512
INPROJ_TILE = 1024
MIX_TIME_TILE = 512
SAMPLE_CONV_BATCH = 32


def kernel(x_prompt, x_sample, cache_k, cache_v, state_conv, attn_norm_g, w_in, q_norm_g, k_norm_g,
           conv_w, conv_b, conv_ln_g, conv_ln_b, w_out, ffn_norm_g, w_gate, w_up, w_down):
    batch, seq, d_model = x_prompt.shape
    n_dec, t_new, _ = x_sample.shape
    depth = w_in.shape[0]
    d_conv = conv_w.shape[-1]
    l_buf = cache_k.shape[2]
    assert seq == WIN_MAX and l_buf == WIN_MAX and seq % (BLK * PATTERNS[-1][1]) == 0
    assert w_in.shape[-1] == 3 * D_ATTN + 2 * d_conv and d_conv == D_ATTN

    seg = jnp.asarray(np.kron(np.eye(N_HEADS), np.ones((HEAD_DIM, HEAD_DIM))), BF16)
    row = lambda v: v.reshape(1, -1)
    to_t = lambda c: jnp.transpose(c, (0, 1, 3, 4, 2)).reshape(depth * n_dec, D_ATTN, l_buf)
    from_t = lambda c, n: jnp.transpose(c.reshape(n, N_HEADS, HEAD_DIM, -1), (0, 3, 1, 2))
    cache_kt, cache_vt = to_t(cache_k), to_t(cache_v)
    state_t = jnp.transpose(state_conv, (0, 2, 1, 3)).reshape(depth * CONV_LEFT, n_dec, d_conv)

    y_p = x_prompt.reshape(batch * seq, d_model)
    y_s = x_sample.reshape(n_dec * t_new, d_model)
    outs = [[] for _ in range(6)]
    for l in range(depth):
        w_in_b = w_in[l].astype(BF16)
        wo_b = w_out[l].astype(BF16)
        wg3, wu3, wd3 = _chunk_major(w_gate[l], w_up[l], w_down[l])
        g_attn, g_ffn = row(attn_norm_g[l]), row(ffn_norm_g[l])
        qg = row(jnp.tile(q_norm_g[l], N_HEADS))
        kg = row(jnp.tile(k_norm_g[l], N_HEADS))
        cb, lg, lb = row(conv_b[l]), row(conv_ln_g[l]), row(conv_ln_b[l])

        q, k, v, u_s = _inproj_sample(y_s, g_attn, w_in_b, qg, kg, seg, TOKEN_TILE)
        c_s, ns = _sample_conv(state_t, u_s, conv_w[l], cb, lg, lb, l, t_new, min(SAMPLE_CONV_BATCH, n_dec))
        qkv, kt, vt, u = _inproj_prompt(y_p, g_attn, w_in_b, qg, kg, seg, batch, seq, INPROJ_TILE)
        attn = {dil: _band_attention(*qkv[dil], w, dil) for w, dil in PATTERNS}
        mix = _mix_prompt(u, conv_w[l], cb, lg, lb, attn, batch, seq, MIX_TIME_TILE)

        as3 = lambda t: t.reshape(n_dec, t_new, D_ATTN)
        y_p, attn_s, nk, nv = _ffn_with_cache(y_p, mix, wo_b, g_ffn, wg3, wu3, wd3,
                                              as3(q), as3(k), as3(v), cache_kt, cache_vt, l, TOKEN_TILE)
        y_s = _outproj_ffn(y_s, [attn_s.reshape(n_dec * t_new, D_ATTN), c_s],
                           wo_b, g_ffn, wg3, wu3, wd3, TOKEN_TILE)
        outs[0].append(from_t(kt, batch))
        outs[1].append(from_t(vt, batch))
        outs[2].append(u.reshape(batch, seq, d_conv)[:, seq - CONV_LEFT:])
        outs[3].append(from_t(nk, n_dec))
        outs[4].append(from_t(nv, n_dec))
        outs[5].append(jnp.transpose(ns, (1, 0, 2)))

    stack = lambda xs: xs[0][None] if len(xs) == 1 else jnp.stack(xs)
    return (y_p.reshape(batch, seq, d_model), y_s.reshape(n_dec, t_new, d_model)) + tuple(
        stack(o) for o in outs)
```

```python
import functools

import numpy as np
import jax
import jax.numpy as jnp
from jax import lax
from jax.experimental import pallas as pl
from jax.experimental.pallas import tpu as pltpu

N_HEADS = 8
HEAD_DIM = 64
D_ATTN = N_HEADS * HEAD_DIM
PATTERNS = ((128, 1), (512, 4), (2048, 16))
WIN_MAX = max(w for w, _ in PATTERNS)
CONV_WIDTH = 31
CONV_LEFT = CONV_WIDTH - 1
BLK = 128
RMS_EPS = 1e-6
LN_EPS = 1e-5
ATTN_SCALE = HEAD_DIM ** -0.5
LOG2E = 1.4426950408889634
NEG_BIG = -1e30

LANES = 128
MXU_COLS = 256
HEADS_PER_LANE_TILE = LANES // HEAD_DIM
VMEM_LIMIT_BYTES = 56 * 1024 * 1024

F32 = jnp.float32
BF16 = jnp.bfloat16


def _params(n_axes):
    return pltpu.CompilerParams(dimension_semantics=("arbitrary",) * n_axes,
                                vmem_limit_bytes=VMEM_LIMIT_BYTES)


def _const_spec(shape):
    return pl.BlockSpec(shape, lambda *_: (0,) * len(shape), pipeline_mode=pl.Buffered(1))


def _sigmoid(x):
    return 1.0 / (1.0 + jnp.exp(-x))


def _rms_rows(x, g):
    ms = jnp.mean(x * x, axis=-1, keepdims=True)
    return x * lax.rsqrt(ms + RMS_EPS) * g


def _class_lanes(c):
    return slice(c * D_ATTN, (c + 1) * D_ATTN)


def _tiles_scratch(rows, width):
    return pltpu.VMEM((width // LANES, rows, LANES), F32)


def _put_tiles(scr, x):
    for lt in range(scr.shape[0]):
        scr[lt] = x[:, lt * LANES:(lt + 1) * LANES]


def _get_tiles(scr):
    return jnp.concatenate([scr[lt] for lt in range(scr.shape[0])], axis=1)


def _get_rows(scr, start, n, stride):
    return jnp.concatenate([scr[lt, pl.ds(start, n, stride=stride), :] for lt in range(scr.shape[0])],
                           axis=1)


def _put_rows(scr, start, stride, x):
    for lt in range(scr.shape[0]):
        scr[lt, pl.ds(start, x.shape[0], stride=stride), :] = x[:, lt * LANES:(lt + 1) * LANES]


def _inproj_kernel(x_ref, g_ref, w_ref, qg_ref, kg_ref, seg_ref, *refs, dilations):
    h = _rms_rows(x_ref[...], g_ref[...]).astype(BF16)

    def proj(col):
        return jnp.dot(h, w_ref[:, col * D_ATTN:(col + 1) * D_ATTN], preferred_element_type=F32)

    def head_rms(t, gain):
        ss = jnp.dot((t * t).astype(BF16), seg_ref[...], preferred_element_type=F32)
        return t * lax.rsqrt(ss * (1.0 / HEAD_DIM) + RMS_EPS) * gain

    make = (lambda: head_rms(proj(0), qg_ref[...]) * (ATTN_SCALE * LOG2E),
            lambda: head_rms(proj(1), kg_ref[...]),
            lambda: proj(2))
    if not dilations:
        for ref, f in zip(refs, make):
            ref[...] = f()
        refs[3][...] = proj(3) * _sigmoid(proj(4))
        return

    n_views = 3 * len(dilations)
    nat = refs[0:3]
    views = refs[3:3 + n_views]
    t_refs = (None,) + refs[3 + n_views:5 + n_views]
    u_ref = refs[5 + n_views]
    scratch = refs[6 + n_views:]
    tm = x_ref.shape[0]
    for i, f in enumerate(make):
        t = f()
        if t_refs[i] is not None:
            t_refs[i][0] = t.T
        nat[i][...] = t.astype(BF16)
        _put_tiles(scratch[i], t)
        for j, dil in enumerate(dilations):
            for c in range(dil):
                views[3 * j + i][0, :, _class_lanes(c)] = (
                    _get_rows(scratch[i], c, tm // dil, dil).astype(BF16))
    u_ref[...] = proj(3) * _sigmoid(proj(4))


def _inproj_sample(x2d, g, w_in_b, qg, kg, seg, tm):
    m, d = x2d.shape
    tm = min(tm, m)
    row = lambda i: (i, 0)
    out_f = jax.ShapeDtypeStruct((m, D_ATTN), F32)
    tile = pl.BlockSpec((tm, D_ATTN), row)
    return pl.pallas_call(
        functools.partial(_inproj_kernel, dilations=()),
        grid=(m // tm,),
        in_specs=[pl.BlockSpec((tm, d), row), _const_spec((1, d)), _const_spec(w_in_b.shape),
                  _const_spec((1, D_ATTN)), _const_spec((1, D_ATTN)), _const_spec((D_ATTN, D_ATTN))],
        out_specs=[tile] * 4,
        out_shape=[out_f] * 4,
        compiler_params=_params(1),
        name="inproj_sample",
    )(x2d, g, w_in_b, qg, kg, seg)


def _inproj_prompt(x2d, g, w_in_b, qg, kg, seg, batch, seq, tm):
    m, d = x2d.shape
    n_t = seq // tm
    dilations = tuple(dil for _, dil in PATTERNS if dil > 1)
    flat = lambda b, t: (b * n_t + t, 0)
    nat_spec = pl.BlockSpec((tm, D_ATTN), flat)
    nat_b = jax.ShapeDtypeStruct((m, D_ATTN), BF16)
    view_specs, view_shapes = [], []
    for dil in dilations:
        view_specs += [pl.BlockSpec((1, tm // dil, dil * D_ATTN), lambda b, t: (b, t, 0))] * 3
        view_shapes += [jax.ShapeDtypeStruct((batch, seq // dil, dil * D_ATTN), BF16)] * 3
    t_spec = pl.BlockSpec((1, D_ATTN, tm), lambda b, t: (b, 0, t))
    t_shape = jax.ShapeDtypeStruct((batch, D_ATTN, seq), F32)
    res = pl.pallas_call(
        functools.partial(_inproj_kernel, dilations=dilations),
        grid=(batch, n_t),
        in_specs=[pl.BlockSpec((tm, d), flat), _const_spec((1, d)), _const_spec(w_in_b.shape),
                  _const_spec((1, D_ATTN)), _const_spec((1, D_ATTN)), _const_spec((D_ATTN, D_ATTN))],
        out_specs=[nat_spec] * 3 + view_specs + [t_spec, t_spec, nat_spec],
        out_shape=[nat_b] * 3 + view_shapes + [t_shape, t_shape, jax.ShapeDtypeStruct((m, D_ATTN), F32)],
        scratch_shapes=[_tiles_scratch(tm, D_ATTN)] * 3,
        compiler_params=_params(2),
        name="inproj_prompt",
    )(x2d, g, w_in_b, qg, kg, seg)
    n_views = 3 * len(dilations)
    qkv = {1: tuple(t.reshape(batch, seq, D_ATTN) for t in res[0:3])}
    for j, dil in enumerate(dilations):
        qkv[dil] = tuple(res[3 + 3 * j:6 + 3 * j])
    kt, vt, u = res[3 + n_views:]
    return qkv, kt, vt, u


def _band_attn_kernel(q_ref, k_ref, v_ref, bias0_ref, bias_ref, o_ref, lse_ref, *, n_blocks, n_classes):
    lane = lax.broadcasted_iota(jnp.int32, (BLK, LANES), 1)
    upper = lane >= HEAD_DIM
    n_pairs = N_HEADS // HEADS_PER_LANE_TILE

    def block(cl, r0, k0, n_keys, bias):
        vps, scores = [], []
        for hp in range(n_pairs):
            lanes = slice(cl * D_ATTN + hp * LANES, cl * D_ATTN + (hp + 1) * LANES)
            qp = q_ref[0, pl.ds(r0, BLK), lanes]
            kp = k_ref[0, pl.ds(k0, n_keys), lanes]
            vps.append(v_ref[0, pl.ds(k0, n_keys), lanes])
            q2 = jnp.concatenate([jnp.where(~upper, qp, jnp.zeros_like(qp)),
                                  jnp.where(upper, qp, jnp.zeros_like(qp))], axis=0)
            s2 = lax.dot_general(q2, kp, (((1,), (1,)), ((), ())), preferred_element_type=F32)
            scores += [s2[hh * BLK:(hh + 1) * BLK] + bias for hh in range(HEADS_PER_LANE_TILE)]
        probs, inv_l = [], []
        lse_tile = jnp.zeros((BLK, LANES), F32)
        for h, s in enumerate(scores):
            m = jnp.max(s, axis=-1, keepdims=True)
            p = jnp.exp2(s - m)
            l = jnp.sum(p, axis=-1, keepdims=True)
            probs.append(p.astype(BF16))
            inv_l.append(1.0 / l)
            lse_tile = jnp.where(lane == h, m + jnp.log(l) * LOG2E, lse_tile)
        lse_ref[0, pl.ds(r0, BLK), cl * LANES:(cl + 1) * LANES] = lse_tile
        for hp in range(n_pairs):
            lanes = slice(cl * D_ATTN + hp * LANES, cl * D_ATTN + (hp + 1) * LANES)
            o2 = jnp.dot(jnp.concatenate(probs[2 * hp:2 * hp + 2], axis=0), vps[hp],
                         preferred_element_type=F32)
            o_lo, o_hi = (o2[hh * BLK:(hh + 1) * BLK] * inv_l[2 * hp + hh]
                          for hh in range(HEADS_PER_LANE_TILE))
            o_ref[0, pl.ds(r0, BLK), lanes] = jnp.where(upper, o_hi, o_lo).astype(o_ref.dtype)

    for cl in range(n_classes):
        block(cl, 0, 0, BLK, bias0_ref[...])
        if n_blocks > 1:
            def body(i, carry, cl=cl):
                r0 = pl.multiple_of(i * BLK, BLK)
                block(cl, r0, pl.multiple_of(r0 - BLK, BLK), 2 * BLK, bias_ref[...])
                return carry

            lax.fori_loop(1, n_blocks, body, 0)


def _band_biases(n_sub):
    qi = np.arange(BLK)[:, None]
    ki = np.arange(2 * BLK)[None, :]
    dist = BLK + qi - ki
    band = (dist >= 0) & (dist <= n_sub)
    d0 = qi - np.arange(BLK)[None, :]
    to_bias = lambda ok: np.where(ok, 0.0, NEG_BIG).astype(np.float32)
    return to_bias((d0 >= 0) & (d0 <= n_sub)), to_bias(band)


BAND_CLASSES_PER_STEP = 8


def _band_attention(q, k, v, window, dilation):
    batch, length, _ = q.shape
    cps = min(BAND_CLASSES_PER_STEP, dilation)
    bias0, bias = _band_biases(window // dilation)
    blk = pl.BlockSpec((1, length, cps * D_ATTN), lambda b, c: (b, 0, c))
    return pl.pallas_call(
        functools.partial(_band_attn_kernel, n_blocks=length // BLK, n_classes=cps),
        grid=(batch, dilation // cps),
        in_specs=[blk, blk, blk, _const_spec((BLK, BLK)), _const_spec((BLK, 2 * BLK))],
        out_specs=[blk, pl.BlockSpec((1, length, cps * LANES), lambda b, c: (b, 0, c))],
        out_shape=[jax.ShapeDtypeStruct(q.shape, BF16),
                   jax.ShapeDtypeStruct((batch, length, dilation * LANES), F32)],
        compiler_params=_params(2),
        name=f"band_attn_d{dilation}",
    )(q, k, v, jnp.asarray(bias0), jnp.asarray(bias))


CONV_PAD = 32
CONV_CHUNK = 64
SUBLANES = 8


def _mix_kernel(u_ref, up_ref, w_ref, b_ref, lg_ref, lb_ref, expand_ref, *refs, tt, dilations):
    n_pat = 1 + len(dilations)
    ins = refs[0:2 * n_pat]
    mix_ref, ext_ref = refs[2 * n_pat:2 * n_pat + 2]
    nat = refs[2 * n_pat + 2:]
    t = pl.program_id(1)
    rows = CONV_PAD + tt
    prev = up_ref[0]
    ext_ref[0, 0:CONV_PAD, :] = jnp.where(t > 0, prev, jnp.zeros_like(prev))
    ext_ref[0, CONV_PAD:rows, :] = u_ref[0]
    for s in range(1, SUBLANES):
        ext_ref[s, 0:rows - SUBLANES, :] = ext_ref[0, pl.ds(s, rows - SUBLANES), :]

    for j, dil in enumerate(dilations):
        for i, width in enumerate((D_ATTN, LANES)):
            src, dst = ins[2 + 2 * j + i], nat[2 * j + i]
            for c in range(dil):
                _put_rows(dst, c, dil, src[0, :, c * width:(c + 1) * width].astype(F32))
    os_ = [ins[0][...].astype(F32)] + [_get_tiles(nat[2 * j]) for j in range(len(dilations))]
    ls_ = [ins[1][...]] + [_get_tiles(nat[2 * j + 1]) for j in range(len(dilations))]
    mx = functools.reduce(jnp.maximum, ls_)
    es = [jnp.exp2(l - mx) for l in ls_]
    inv = 1.0 / sum(es)
    attn = None
    for e, o in zip(es, os_):
        w = e * inv
        hi = w.astype(BF16)
        lo = (w - hi.astype(F32)).astype(BF16)
        wide = (jnp.dot(hi, expand_ref[...], preferred_element_type=F32)
                + jnp.dot(lo, expand_ref[...], preferred_element_type=F32))
        attn = wide * o if attn is None else attn + wide * o
    mix_ref[:, 0:D_ATTN] = attn.astype(mix_ref.dtype)

    d_conv = u_ref.shape[-1]
    for ch in range(tt // CONV_CHUNK):
        acc = None
        for j in range(CONV_WIDTH):
            off = CONV_PAD - CONV_LEFT + j
            s = off % SUBLANES
            term = ext_ref[s, pl.ds(ch * CONV_CHUNK + off - s, CONV_CHUNK), :] * w_ref[j:j + 1, :]
            acc = term if acc is None else acc + term
        c = _ln_swish(acc + b_ref[...], lg_ref[...], lb_ref[...])
        mix_ref[ch * CONV_CHUNK:(ch + 1) * CONV_CHUNK, D_ATTN:D_ATTN + d_conv] = c.astype(mix_ref.dtype)


def _ln_swish(c, g, b):
    mu = jnp.mean(c, axis=-1, keepdims=True)
    cc = c - mu
    var = jnp.mean(cc * cc, axis=-1, keepdims=True)
    y = cc * lax.rsqrt(var + LN_EPS) * g + b
    return y * _sigmoid(y)


def _mix_prompt(u, conv_w, conv_b, ln_g, ln_b, attn, batch, seq, tt):
    d_conv = u.shape[-1]
    u3 = u.reshape(batch, seq, d_conv)
    n_t = seq // tt
    per = tt // CONV_PAD
    dilations = tuple(dil for _, dil in PATTERNS if dil > 1)
    cur = pl.BlockSpec((1, tt, d_conv), lambda b, t: (b, t, 0))
    prev = pl.BlockSpec((1, CONV_PAD, d_conv), lambda b, t: (b, jnp.maximum(t * per - 1, 0), 0))
    flat = lambda width: pl.BlockSpec((tt, width), lambda b, t: (b * n_t + t, 0))
    view = lambda dil, width: pl.BlockSpec((1, tt // dil, dil * width), lambda b, t: (b, t, 0))
    ins = [attn[1][0].reshape(batch * seq, D_ATTN), attn[1][1].reshape(batch * seq, LANES)]
    specs = [flat(D_ATTN), flat(LANES)]
    scratch = [pltpu.VMEM((SUBLANES, CONV_PAD + tt, d_conv), F32)]
    for dil in dilations:
        ins += list(attn[dil])
        specs += [view(dil, D_ATTN), view(dil, LANES)]
        scratch += [_tiles_scratch(tt, D_ATTN), _tiles_scratch(tt, LANES)]
    expand = np.zeros((LANES, D_ATTN), np.float32)
    expand[:N_HEADS] = np.kron(np.eye(N_HEADS), np.ones((1, HEAD_DIM)))
    return pl.pallas_call(
        functools.partial(_mix_kernel, tt=tt, dilations=dilations),
        grid=(batch, n_t),
        in_specs=[cur, prev, _const_spec(conv_w.shape), _const_spec((1, d_conv)),
                  _const_spec((1, d_conv)), _const_spec((1, d_conv)),
                  _const_spec((LANES, D_ATTN))] + specs,
        out_specs=pl.BlockSpec((tt, D_ATTN + d_conv), lambda b, t: (b * n_t + t, 0)),
        out_shape=jax.ShapeDtypeStruct((batch * seq, D_ATTN + d_conv), BF16),
        scratch_shapes=scratch,
        compiler_params=_params(2),
        name="mix_prompt",
    )(u3, u3, conv_w, conv_b, ln_g, ln_b, jnp.asarray(expand, BF16), *ins)


def _sample_conv_kernel(st_ref, u_ref, w_ref, b_ref, lg_ref, lb_ref, c_ref, ns_ref, us_ref, cs_ref,
                        *, nb, t_new):
    _put_tiles(us_ref, u_ref[...])
    new = [_get_rows(us_ref, t, nb, t_new) for t in range(t_new)]

    def ext(tau):
        return st_ref[tau] if tau < CONV_LEFT else new[tau - CONV_LEFT]

    for t in range(t_new):
        acc = None
        for j in range(CONV_WIDTH):
            term = ext(t + j) * w_ref[j:j + 1, :]
            acc = term if acc is None else acc + term
        c = _ln_swish(acc + b_ref[...], lg_ref[...], lb_ref[...])
        _put_rows(cs_ref, t, t_new, c)
    c_ref[...] = _get_tiles(cs_ref)
    for tau in range(CONV_LEFT):
        ns_ref[tau] = ext(tau + t_new)


def _sample_conv(state_t, u_s, conv_w, conv_b, ln_g, ln_b, layer, t_new, nb):
    _, n, d_conv = state_t.shape
    return pl.pallas_call(
        functools.partial(_sample_conv_kernel, nb=nb, t_new=t_new),
        grid=(n // nb,),
        in_specs=[pl.BlockSpec((CONV_LEFT, nb, d_conv), lambda i: (layer, i, 0)),
                  pl.BlockSpec((nb * t_new, d_conv), lambda i: (i, 0)),
                  _const_spec(conv_w.shape), _const_spec((1, d_conv)),
                  _const_spec((1, d_conv)), _const_spec((1, d_conv))],
        out_specs=[pl.BlockSpec((nb * t_new, d_conv), lambda i: (i, 0)),
                   pl.BlockSpec((CONV_LEFT, nb, d_conv), lambda i: (0, i, 0))],
        out_shape=[jax.ShapeDtypeStruct((n * t_new, d_conv), F32),
                   jax.ShapeDtypeStruct((CONV_LEFT, n, d_conv), F32)],
        scratch_shapes=[_tiles_scratch(nb * t_new, d_conv)] * 2,
        compiler_params=_params(1),
        name="sample_conv",
    )(state_t, u_s, conv_w, conv_b, ln_g, ln_b)


def _multiplicity(dist):
    dist = np.asarray(dist)
    c = np.zeros(dist.shape, np.float32)
    for window, dilation in PATTERNS:
        c += ((dist >= 0) & (dist <= window) & (dist % dilation == 0)).astype(np.float32)
    return c


def _shift_append(old, new_rows, out_ref, t_new):
    n_ch, l_buf = old.shape
    n_tiles = l_buf // LANES
    lane = lax.broadcasted_iota(jnp.int32, (n_ch, LANES), 1)
    keep = lane < LANES - t_new
    pad = jnp.concatenate([jnp.zeros((LANES - t_new, n_ch), F32), new_rows], axis=0)
    nxt = pad.T
    for j in reversed(range(n_tiles)):
        cur = pltpu.roll(old[:, j * LANES:(j + 1) * LANES], LANES - t_new, axis=1)
        out_ref[0, :, j * LANES:(j + 1) * LANES] = jnp.where(keep, cur, nxt)
        nxt = cur


CACHE_HEADS_PER_UNIT = 4


def _sample_attend(q_ref, kn_ref, vn_ref, ck, cv, cnt_ref, attn_ref, t_new):
    width = ck.shape[0]
    n_heads = width // HEAD_DIM
    rows = n_heads * t_new
    kn = kn_ref[0]
    vn = vn_ref[0]

    row = lax.broadcasted_iota(jnp.int32, (rows, width), 0)
    lane = lax.broadcasted_iota(jnp.int32, (rows, width), 1)
    own_head = (row // t_new) == (lane // HEAD_DIM)
    q_rep = jnp.concatenate([q_ref[0]] * n_heads, axis=0)
    q_bd = jnp.where(own_head, q_rep, 0.0)

    cnt = cnt_ref[...]
    s_c = jnp.dot(q_bd.astype(BF16), ck.astype(BF16), preferred_element_type=F32)
    s_c = jnp.where(cnt > 0.0, s_c, NEG_BIG)
    m = jnp.max(s_c, axis=-1, keepdims=True)

    t_of_row = lax.broadcasted_iota(jnp.int32, (rows, 1), 0) % t_new
    s_n, c_n = [], []
    for tp in range(t_new):
        d = t_of_row - tp
        c = jnp.zeros((rows, 1), F32)
        for _, dilation in PATTERNS:
            c = c + jnp.where((d >= 0) & (d % dilation == 0), 1.0, 0.0)
        s = jnp.sum(q_bd * kn[tp:tp + 1, :], axis=-1, keepdims=True)
        s = jnp.where(c > 0.0, s, NEG_BIG)
        m = jnp.maximum(m, s)
        s_n.append(s)
        c_n.append(c)

    p_c = cnt * jnp.exp2(s_c - m)
    l = jnp.sum(p_c, axis=-1, keepdims=True)
    acc = lax.dot_general(p_c.astype(BF16), cv.astype(BF16), (((1,), (1,)), ((), ())),
                          preferred_element_type=F32)
    for tp in range(t_new):
        p = c_n[tp] * jnp.exp2(s_n[tp] - m)
        l = l + p
        acc = acc + p * vn[tp:tp + 1, :]
    acc = jnp.where(own_head, acc * (1.0 / l), 0.0)
    out = acc[0:t_new, :]
    for h in range(1, n_heads):
        out = out + acc[h * t_new:(h + 1) * t_new, :]
    attn_ref[0] = out.astype(attn_ref.dtype)


CACHE_RING = 3


def _swiglu_chunk(h, wg_ref, wu_ref, wd_ref, j):
    cols = pl.ds(pl.multiple_of(j * MXU_COLS, MXU_COLS), MXU_COLS)
    gate = jnp.dot(h, wg_ref[:, cols], preferred_element_type=F32)
    up = jnp.dot(h, wu_ref[:, cols], preferred_element_type=F32)
    act = (gate * _sigmoid(gate) * up).astype(BF16)
    return jnp.dot(act, wd_ref[j], preferred_element_type=F32)


def _ffn_cache_kernel(x_ref, mix_ref, wo_ref, g_ref, wg_ref, wu_ref, wd_ref,
                      q_ref, kn_ref, vn_ref, ck_hbm, cv_hbm, cnt_ref,
                      y_ref, attn_ref, ok_ref, ov_ref, h_ref, kbuf, vbuf, sem,
                      *, steps_per_tile, t_new, first_row, units_per_row):
    u = pl.program_id(0)
    n_units = pl.num_programs(0)
    c = u % steps_per_tile
    base, extra = divmod(wd_ref.shape[0], steps_per_tile)
    uw = kbuf.shape[1]

    def fetch(unit, slot):
        row = first_row + unit // units_per_row
        h0 = pl.multiple_of((unit % units_per_row) * uw, uw)
        return (pltpu.make_async_copy(ck_hbm.at[row, pl.ds(h0, uw), :], kbuf.at[slot], sem.at[0, slot]),
                pltpu.make_async_copy(cv_hbm.at[row, pl.ds(h0, uw), :], vbuf.at[slot], sem.at[1, slot]))

    @pl.when(u == 0)
    def _():
        for unit in range(CACHE_RING - 1):
            for cp in fetch(unit, unit):
                cp.start()

    ahead = u + (CACHE_RING - 1)

    @pl.when(ahead < n_units)
    def _():
        for cp in fetch(ahead, ahead % CACHE_RING):
            cp.start()

    @pl.when(c == 0)
    def _():
        x1 = x_ref[...] + jnp.dot(mix_ref[...], wo_ref[...], preferred_element_type=F32)
        y_ref[...] = x1
        h_ref[...] = _rms_rows(x1, g_ref[...]).astype(BF16)

    slot = u % CACHE_RING
    for cp in fetch(u, slot):
        cp.wait()
    ck, cv = kbuf[slot], vbuf[slot]
    _shift_append(ck, kn_ref[0], ok_ref, t_new)
    _shift_append(cv, vn_ref[0], ov_ref, t_new)
    _sample_attend(q_ref, kn_ref, vn_ref, ck, cv, cnt_ref, attn_ref, t_new)
    for i in range(base):
        y_ref[...] += _swiglu_chunk(h_ref[...], wg_ref, wu_ref, wd_ref, c * base + i)
    if extra:
        first = steps_per_tile - extra

        @pl.when(c >= first)
        def _():
            y_ref[...] += _swiglu_chunk(h_ref[...], wg_ref, wu_ref, wd_ref, steps_per_tile * base + c - first)


def _chunk_major(wg, wu, wd):
    d, d_ff = wg.shape
    assert d_ff % MXU_COLS == 0
    n_chunks = d_ff // MXU_COLS
    return wg.astype(BF16), wu.astype(BF16), wd.reshape(n_chunks, MXU_COLS, d).astype(BF16)


def _ffn_with_cache(x2d, mix, wo_b, g, wg3, wu3, wd3, q_s, k_s, v_s, cache_kt, cache_vt, layer, tm):
    m, d = x2d.shape
    n, t_new, _ = q_s.shape
    l_buf = cache_kt.shape[2]
    assert l_buf == WIN_MAX and t_new % SUBLANES == 0
    uw = CACHE_HEADS_PER_UNIT * HEAD_DIM
    per_n = D_ATTN // uw
    n_tiles, n_units = m // tm, n * per_n
    assert n_units % n_tiles == 0
    steps_per_tile = n_units // n_tiles
    assert wd3.shape[0] >= steps_per_tile and n_units >= CACHE_RING
    dist = l_buf + np.arange(t_new)[:, None] - np.arange(l_buf)[None, :]
    cnt = np.tile(_multiplicity(dist), (CACHE_HEADS_PER_UNIT, 1))

    tile = lambda width: pl.BlockSpec((tm, width), lambda u: (u // steps_per_tile, 0))
    new = pl.BlockSpec((1, t_new, uw), lambda u: (u // per_n, 0, u % per_n))
    big = pl.BlockSpec((1, uw, l_buf), lambda u: (u // per_n, u % per_n, 0))
    in_hbm = pl.BlockSpec(memory_space=pl.ANY)
    return pl.pallas_call(
        functools.partial(_ffn_cache_kernel, steps_per_tile=steps_per_tile, t_new=t_new,
                          first_row=layer * n, units_per_row=per_n),
        grid=(n_units,),
        in_specs=[tile(d), tile(mix.shape[1]), _const_spec(wo_b.shape), _const_spec((1, d)),
                  _const_spec(wg3.shape), _const_spec(wu3.shape), _const_spec(wd3.shape),
                  new, new, new, in_hbm, in_hbm, _const_spec(cnt.shape)],
        out_specs=[tile(d), new, big, big],
        out_shape=[jax.ShapeDtypeStruct((m, d), F32),
                   jax.ShapeDtypeStruct((n, t_new, D_ATTN), F32),
                   jax.ShapeDtypeStruct((n, D_ATTN, l_buf), F32),
                   jax.ShapeDtypeStruct((n, D_ATTN, l_buf), F32)],
        scratch_shapes=[pltpu.VMEM((tm, d), BF16),
                        pltpu.VMEM((CACHE_RING, uw, l_buf), F32), pltpu.VMEM((CACHE_RING, uw, l_buf), F32),
                        pltpu.SemaphoreType.DMA((2, CACHE_RING))],
        compiler_params=_params(1),
        name="ffn_cache",
    )(x2d, mix, wo_b, g, wg3, wu3, wd3, q_s, k_s, v_s, cache_kt, cache_vt, jnp.asarray(cnt))


def _ffn_kernel(x_ref, *refs, n_mix):
    mix_refs = refs[0:n_mix]
    wo_ref, g_ref, wg_ref, wu_ref, wd_ref, y_ref = refs[n_mix:]
    x1 = x_ref[...]
    row = 0
    for mr in mix_refs:
        width = mr.shape[1]
        x1 = x1 + jnp.dot(mr[...].astype(BF16), wo_ref[row:row + width, :], preferred_element_type=F32)
        row += width
    h = _rms_rows(x1, g_ref[...]).astype(BF16)
    y_ref[...] = x1
    for j in range(wd_ref.shape[0]):
        y_ref[...] += _swiglu_chunk(h, wg_ref, wu_ref, wd_ref, j)


def _outproj_ffn(x2d, mix_parts, wo_b, g, wg3, wu3, wd3, tm):
    m, d = x2d.shape
    tm = min(tm, m)
    row = lambda i: (i, 0)
    return pl.pallas_call(
        functools.partial(_ffn_kernel, n_mix=len(mix_parts)),
        grid=(m // tm,),
        in_specs=[pl.BlockSpec((tm, d), row)] + [pl.BlockSpec((tm, p.shape[1]), row) for p in mix_parts]
        + [_const_spec(wo_b.shape), _const_spec((1, d)),
           _const_spec(wg3.shape), _const_spec(wu3.shape), _const_spec(wd3.shape)],
        out_specs=pl.BlockSpec((tm, d), row),
        out_shape=jax.ShapeDtypeStruct((m, d), F32),
        compiler_params=_params(1),
        name="outproj_ffn",
    )(x2d, *mix_parts, wo_b, g, wg3, wu3, wd3)


TOKEN_TILE = 512
MIX_TIME_TILE = 512
SAMPLE_CONV_BATCH = 32


def kernel(x_prompt, x_sample, cache_k, cache_v, state_conv, attn_norm_g, w_in, q_norm_g, k_norm_g,
           conv_w, conv_b, conv_ln_g, conv_ln_b, w_out, ffn_norm_g, w_gate, w_up, w_down):
    batch, seq, d_model = x_prompt.shape
    n_dec, t_new, _ = x_sample.shape
    depth = w_in.shape[0]
    d_conv = conv_w.shape[-1]
    l_buf = cache_k.shape[2]
    assert seq == WIN_MAX and l_buf == WIN_MAX and seq % (BLK * PATTERNS[-1][1]) == 0
    assert w_in.shape[-1] == 3 * D_ATTN + 2 * d_conv and d_conv == D_ATTN

    seg = jnp.asarray(np.kron(np.eye(N_HEADS), np.ones((HEAD_DIM, HEAD_DIM))), BF16)
    row = lambda v: v.reshape(1, -1)
    to_t = lambda c: jnp.transpose(c, (0, 1, 3, 4, 2)).reshape(depth * n_dec, D_ATTN, l_buf)
    from_t = lambda c, n: jnp.transpose(c.reshape(n, N_HEADS, HEAD_DIM, -1), (0, 3, 1, 2))
    cache_kt, cache_vt = to_t(cache_k), to_t(cache_v)
    state_t = jnp.transpose(state_conv, (0, 2, 1, 3)).reshape(depth * CONV_LEFT, n_dec, d_conv)

    y_p = x_prompt.reshape(batch * seq, d_model)
    y_s = x_sample.reshape(n_dec * t_new, d_model)
    outs = [[] for _ in range(6)]
    for l in range(depth):
        w_in_b = w_in[l].astype(BF16)
        wo_b = w_out[l].astype(BF16)
        wg3, wu3, wd3 = _chunk_major(w_gate[l], w_up[l], w_down[l])
        g_attn, g_ffn = row(attn_norm_g[l]), row(ffn_norm_g[l])
        qg = row(jnp.tile(q_norm_g[l], N_HEADS))
        kg = row(jnp.tile(k_norm_g[l], N_HEADS))
        cb, lg, lb = row(conv_b[l]), row(conv_ln_g[l]), row(conv_ln_b[l])

        q, k, v, u_s = _inproj_sample(y_s, g_attn, w_in_b, qg, kg, seg, TOKEN_TILE)
        c_s, ns = _sample_conv(state_t, u_s, conv_w[l], cb, lg, lb, l, t_new, min(SAMPLE_CONV_BATCH, n_dec))
        qkv, kt, vt, u = _inproj_prompt(y_p, g_attn, w_in_b, qg, kg, seg, batch, seq, TOKEN_TILE)
        attn = {dil: _band_attention(*qkv[dil], w, dil) for w, dil in PATTERNS}
        mix = _mix_prompt(u, conv_w[l], cb, lg, lb, attn, batch, seq, MIX_TIME_TILE)

        as3 = lambda t: t.reshape(n_dec, t_new, D_ATTN)
        y_p, attn_s, nk, nv = _ffn_with_cache(y_p, mix, wo_b, g_ffn, wg3, wu3, wd3,
                                              as3(q), as3(k), as3(v), cache_kt, cache_vt, l, TOKEN_TILE)
        y_s = _outproj_ffn(y_s, [attn_s.reshape(n_dec * t_new, D_ATTN), c_s],
                           wo_b, g_ffn, wg3, wu3, wd3, TOKEN_TILE)
        outs[0].append(from_t(kt, batch))
        outs[1].append(from_t(vt, batch))
        outs[2].append(u.reshape(batch, seq, d_conv)[:, seq - CONV_LEFT:])
        outs[3].append(from_t(nk, n_dec))
        outs[4].append(from_t(nv, n_dec))
        outs[5].append(jnp.transpose(ns, (1, 0, 2)))

    stack = lambda xs: xs[0][None] if len(xs) == 1 else jnp.stack(xs)
    return (y_p.reshape(batch, seq, d_model), y_s.reshape(n_dec, t_new, d_model)) + tuple(
        stack(o) for o in outs)
```

```python
import functools

import numpy as np
import jax
import jax.numpy as jnp
from jax import lax
from jax.experimental import pallas as pl
from jax.experimental.pallas import tpu as pltpu

N_HEADS = 8
HEAD_DIM = 64
D_ATTN = N_HEADS * HEAD_DIM
PATTERNS = ((128, 1), (512, 4), (2048, 16))
WIN_MAX = max(w for w, _ in PATTERNS)
CONV_WIDTH = 31
CONV_LEFT = CONV_WIDTH - 1
BLK = 128
RMS_EPS = 1e-6
LN_EPS = 1e-5
ATTN_SCALE = HEAD_DIM ** -0.5
LOG2E = 1.4426950408889634
NEG_BIG = -1e30

LANES = 128
MXU_COLS = 256
HEADS_PER_LANE_TILE = LANES // HEAD_DIM
VMEM_LIMIT_BYTES = 56 * 1024 * 1024

F32 = jnp.float32
BF16 = jnp.bfloat16


def _params(n_axes):
    return pltpu.CompilerParams(dimension_semantics=("arbitrary",) * n_axes,
                                vmem_limit_bytes=VMEM_LIMIT_BYTES)


def _const_spec(shape):
    return pl.BlockSpec(shape, lambda *_: (0,) * len(shape), pipeline_mode=pl.Buffered(1))


def _sigmoid(x):
    return 1.0 / (1.0 + jnp.exp(-x))


def _rms_rows(x, g):
    ms = jnp.mean(x * x, axis=-1, keepdims=True)
    return x * lax.rsqrt(ms + RMS_EPS) * g


def _class_lanes(c):
    return slice(c * D_ATTN, (c + 1) * D_ATTN)


def _tiles_scratch(rows, width):
    return pltpu.VMEM((width // LANES, rows, LANES), F32)


def _put_tiles(scr, x):
    for lt in range(scr.shape[0]):
        scr[lt] = x[:, lt * LANES:(lt + 1) * LANES]


def _get_tiles(scr):
    return jnp.concatenate([scr[lt] for lt in range(scr.shape[0])], axis=1)


def _get_rows(scr, start, n, stride):
    return jnp.concatenate([scr[lt, pl.ds(start, n, stride=stride), :] for lt in range(scr.shape[0])],
                           axis=1)


def _put_rows(scr, start, stride, x):
    for lt in range(scr.shape[0]):
        scr[lt, pl.ds(start, x.shape[0], stride=stride), :] = x[:, lt * LANES:(lt + 1) * LANES]


def _inproj_kernel(x_ref, g_ref, w_ref, qg_ref, kg_ref, seg_ref, *refs, dilations):
    h = _rms_rows(x_ref[...], g_ref[...]).astype(BF16)

    def proj(col):
        return jnp.dot(h, w_ref[:, col * D_ATTN:(col + 1) * D_ATTN], preferred_element_type=F32)

    def head_rms(t, gain):
        ss = jnp.dot((t * t).astype(BF16), seg_ref[...], preferred_element_type=F32)
        return t * lax.rsqrt(ss * (1.0 / HEAD_DIM) + RMS_EPS) * gain

    make = (lambda: head_rms(proj(0), qg_ref[...]) * (ATTN_SCALE * LOG2E),
            lambda: head_rms(proj(1), kg_ref[...]),
            lambda: proj(2))
    if not dilations:
        for ref, f in zip(refs, make):
            ref[...] = f()
        refs[3][...] = proj(3) * _sigmoid(proj(4))
        return

    n_views = 3 * len(dilations)
    nat = refs[0:3]
    views = refs[3:3 + n_views]
    t_refs = (None,) + refs[3 + n_views:5 + n_views]
    u_ref = refs[5 + n_views]
    scratch = refs[6 + n_views:]
    tm = x_ref.shape[0]
    for i, f in enumerate(make):
        t = f()
        if t_refs[i] is not None:
            t_refs[i][0] = t.T
        nat[i][...] = t.astype(BF16)
        _put_tiles(scratch[i], t)
        for j, dil in enumerate(dilations):
            for c in range(dil):
                views[3 * j + i][0, :, _class_lanes(c)] = (
                    _get_rows(scratch[i], c, tm // dil, dil).astype(BF16))
    u_ref[...] = proj(3) * _sigmoid(proj(4))


def _inproj_sample(x2d, g, w_in_b, qg, kg, seg, tm):
    m, d = x2d.shape
    tm = min(tm, m)
    row = lambda i: (i, 0)
    out_f = jax.ShapeDtypeStruct((m, D_ATTN), F32)
    tile = pl.BlockSpec((tm, D_ATTN), row)
    return pl.pallas_call(
        functools.partial(_inproj_kernel, dilations=()),
        grid=(m // tm,),
        in_specs=[pl.BlockSpec((tm, d), row), _const_spec((1, d)), _const_spec(w_in_b.shape),
                  _const_spec((1, D_ATTN)), _const_spec((1, D_ATTN)), _const_spec((D_ATTN, D_ATTN))],
        out_specs=[tile] * 4,
        out_shape=[out_f] * 4,
        compiler_params=_params(1),
        name="inproj_sample",
    )(x2d, g, w_in_b, qg, kg, seg)


def _inproj_prompt(x2d, g, w_in_b, qg, kg, seg, batch, seq, tm):
    m, d = x2d.shape
    n_t = seq // tm
    dilations = tuple(dil for _, dil in PATTERNS if dil > 1)
    flat = lambda b, t: (b * n_t + t, 0)
    nat_spec = pl.BlockSpec((tm, D_ATTN), flat)
    nat_b = jax.ShapeDtypeStruct((m, D_ATTN), BF16)
    view_specs, view_shapes = [], []
    for dil in dilations:
        view_specs += [pl.BlockSpec((1, tm // dil, dil * D_ATTN), lambda b, t: (b, t, 0))] * 3
        view_shapes += [jax.ShapeDtypeStruct((batch, seq // dil, dil * D_ATTN), BF16)] * 3
    t_spec = pl.BlockSpec((1, D_ATTN, tm), lambda b, t: (b, 0, t))
    t_shape = jax.ShapeDtypeStruct((batch, D_ATTN, seq), F32)
    res = pl.pallas_call(
        functools.partial(_inproj_kernel, dilations=dilations),
        grid=(batch, n_t),
        in_specs=[pl.BlockSpec((tm, d), flat), _const_spec((1, d)), _const_spec(w_in_b.shape),
                  _const_spec((1, D_ATTN)), _const_spec((1, D_ATTN)), _const_spec((D_ATTN, D_ATTN))],
        out_specs=[nat_spec] * 3 + view_specs + [t_spec, t_spec, nat_spec],
        out_shape=[nat_b] * 3 + view_shapes + [t_shape, t_shape, jax.ShapeDtypeStruct((m, D_ATTN), F32)],
        scratch_shapes=[_tiles_scratch(tm, D_ATTN)] * 3,
        compiler_params=_params(2),
        name="inproj_prompt",
    )(x2d, g, w_in_b, qg, kg, seg)
    n_views = 3 * len(dilations)
    qkv = {1: tuple(t.reshape(batch, seq, D_ATTN) for t in res[0:3])}
    for j, dil in enumerate(dilations):
        qkv[dil] = tuple(res[3 + 3 * j:6 + 3 * j])
    kt, vt, u = res[3 + n_views:]
    return qkv, kt, vt, u


def _band_attn_kernel(q_ref, k_ref, v_ref, bias0_ref, bias_ref, o_ref, lse_ref, *, n_blocks, n_classes):
    lane = lax.broadcasted_iota(jnp.int32, (BLK, LANES), 1)
    upper = lane >= HEAD_DIM
    n_pairs = N_HEADS // HEADS_PER_LANE_TILE

    def block(cl, r0, k0, n_keys, bias):
        vps, scores = [], []
        for hp in range(n_pairs):
            lanes = slice(cl * D_ATTN + hp * LANES, cl * D_ATTN + (hp + 1) * LANES)
            qp = q_ref[0, pl.ds(r0, BLK), lanes]
            kp = k_ref[0, pl.ds(k0, n_keys), lanes]
            vps.append(v_ref[0, pl.ds(k0, n_keys), lanes])
            q2 = jnp.concatenate([jnp.where(~upper, qp, jnp.zeros_like(qp)),
                                  jnp.where(upper, qp, jnp.zeros_like(qp))], axis=0)
            s2 = lax.dot_general(q2, kp, (((1,), (1,)), ((), ())), preferred_element_type=F32)
            scores += [s2[hh * BLK:(hh + 1) * BLK] + bias for hh in range(HEADS_PER_LANE_TILE)]
        probs, inv_l = [], []
        lse_tile = jnp.zeros((BLK, LANES), F32)
        for h, s in enumerate(scores):
            m = jnp.max(s, axis=-1, keepdims=True)
            p = jnp.exp2(s - m)
            l = jnp.sum(p, axis=-1, keepdims=True)
            probs.append(p.astype(BF16))
            inv_l.append(1.0 / l)
            lse_tile = jnp.where(lane == h, m + jnp.log(l) * LOG2E, lse_tile)
        lse_ref[0, pl.ds(r0, BLK), cl * LANES:(cl + 1) * LANES] = lse_tile
        for hp in range(n_pairs):
            lanes = slice(cl * D_ATTN + hp * LANES, cl * D_ATTN + (hp + 1) * LANES)
            o2 = jnp.dot(jnp.concatenate(probs[2 * hp:2 * hp + 2], axis=0), vps[hp],
                         preferred_element_type=F32)
            o_lo, o_hi = (o2[hh * BLK:(hh + 1) * BLK] * inv_l[2 * hp + hh]
                          for hh in range(HEADS_PER_LANE_TILE))
            o_ref[0, pl.ds(r0, BLK), lanes] = jnp.where(upper, o_hi, o_lo).astype(o_ref.dtype)

    for cl in range(n_classes):
        block(cl, 0, 0, BLK, bias0_ref[...])
        if n_blocks > 1:
            def body(i, carry, cl=cl):
                r0 = pl.multiple_of(i * BLK, BLK)
                block(cl, r0, pl.multiple_of(r0 - BLK, BLK), 2 * BLK, bias_ref[...])
                return carry

            lax.fori_loop(1, n_blocks, body, 0)


def _band_biases(n_sub):
    qi = np.arange(BLK)[:, None]
    ki = np.arange(2 * BLK)[None, :]
    dist = BLK + qi - ki
    band = (dist >= 0) & (dist <= n_sub)
    d0 = qi - np.arange(BLK)[None, :]
    to_bias = lambda ok: np.where(ok, 0.0, NEG_BIG).astype(np.float32)
    return to_bias((d0 >= 0) & (d0 <= n_sub)), to_bias(band)


BAND_CLASSES_PER_STEP = 8


def _band_attention(q, k, v, window, dilation):
    batch, length, _ = q.shape
    cps = min(BAND_CLASSES_PER_STEP, dilation)
    bias0, bias = _band_biases(window // dilation)
    blk = pl.BlockSpec((1, length, cps * D_ATTN), lambda b, c: (b, 0, c))
    return pl.pallas_call(
        functools.partial(_band_attn_kernel, n_blocks=length // BLK, n_classes=cps),
        grid=(batch, dilation // cps),
        in_specs=[blk, blk, blk, _const_spec((BLK, BLK)), _const_spec((BLK, 2 * BLK))],
        out_specs=[blk, pl.BlockSpec((1, length, cps * LANES), lambda b, c: (b, 0, c))],
        out_shape=[jax.ShapeDtypeStruct(q.shape, BF16),
                   jax.ShapeDtypeStruct((batch, length, dilation * LANES), F32)],
        compiler_params=_params(2),
        name=f"band_attn_d{dilation}",
    )(q, k, v, jnp.asarray(bias0), jnp.asarray(bias))


CONV_PAD = 32
CONV_CHUNK = 64
SUBLANES = 8


def _mix_kernel(u_ref, up_ref, w_ref, b_ref, lg_ref, lb_ref, expand_ref, *refs, tt, dilations, n_cast):
    n_pat = 1 + len(dilations)
    ins = refs[0:2 * n_pat]
    cast_in = refs[2 * n_pat:2 * n_pat + n_cast]
    mix_ref = refs[2 * n_pat + n_cast]
    cast_out = refs[2 * n_pat + n_cast + 1:2 * n_pat + 2 * n_cast + 1]
    ext_ref = refs[2 * n_pat + 2 * n_cast + 1]
    nat = refs[2 * n_pat + 2 * n_cast + 2:]
    for src, dst in zip(cast_in, cast_out):
        dst[...] = src[...].astype(dst.dtype)
    t = pl.program_id(1)
    rows = CONV_PAD + tt
    prev = up_ref[0]
    ext_ref[0, 0:CONV_PAD, :] = jnp.where(t > 0, prev, jnp.zeros_like(prev))
    ext_ref[0, CONV_PAD:rows, :] = u_ref[0]
    for s in range(1, SUBLANES):
        ext_ref[s, 0:rows - SUBLANES, :] = ext_ref[0, pl.ds(s, rows - SUBLANES), :]

    for j, dil in enumerate(dilations):
        for i, width in enumerate((D_ATTN, LANES)):
            src, dst = ins[2 + 2 * j + i], nat[2 * j + i]
            for c in range(dil):
                _put_rows(dst, c, dil, src[0, :, c * width:(c + 1) * width].astype(F32))
    os_ = [ins[0][...].astype(F32)] + [_get_tiles(nat[2 * j]) for j in range(len(dilations))]
    ls_ = [ins[1][...]] + [_get_tiles(nat[2 * j + 1]) for j in range(len(dilations))]
    mx = functools.reduce(jnp.maximum, ls_)
    es = [jnp.exp2(l - mx) for l in ls_]
    inv = 1.0 / sum(es)
    attn = None
    for e, o in zip(es, os_):
        w = e * inv
        hi = w.astype(BF16)
        lo = (w - hi.astype(F32)).astype(BF16)
        wide = (jnp.dot(hi, expand_ref[...], preferred_element_type=F32)
                + jnp.dot(lo, expand_ref[...], preferred_element_type=F32))
        attn = wide * o if attn is None else attn + wide * o
    mix_ref[:, 0:D_ATTN] = attn.astype(mix_ref.dtype)

    d_conv = u_ref.shape[-1]
    for ch in range(tt // CONV_CHUNK):
        acc = None
        for j in range(CONV_WIDTH):
            off = CONV_PAD - CONV_LEFT + j
            s = off % SUBLANES
            term = ext_ref[s, pl.ds(ch * CONV_CHUNK + off - s, CONV_CHUNK), :] * w_ref[j:j + 1, :]
            acc = term if acc is None else acc + term
        c = _ln_swish(acc + b_ref[...], lg_ref[...], lb_ref[...])
        mix_ref[ch * CONV_CHUNK:(ch + 1) * CONV_CHUNK, D_ATTN:D_ATTN + d_conv] = c.astype(mix_ref.dtype)


def _ln_swish(c, g, b):
    mu = jnp.mean(c, axis=-1, keepdims=True)
    cc = c - mu
    var = jnp.mean(cc * cc, axis=-1, keepdims=True)
    y = cc * lax.rsqrt(var + LN_EPS) * g + b
    return y * _sigmoid(y)


def _cast_blocking(rows, n_steps):
    for n_blocks in range(n_steps, 0, -1):
        if n_steps % n_blocks == 0 and rows % (n_blocks * 2 * SUBLANES) == 0:
            return rows // n_blocks, n_steps // n_blocks
    raise ValueError((rows, n_steps))


def _mix_prompt(u, conv_w, conv_b, ln_g, ln_b, attn, weights, batch, seq, tt):
    d_conv = u.shape[-1]
    u3 = u.reshape(batch, seq, d_conv)
    n_t = seq // tt
    per = tt // CONV_PAD
    dilations = tuple(dil for _, dil in PATTERNS if dil > 1)
    cur = pl.BlockSpec((1, tt, d_conv), lambda b, t: (b, t, 0))
    prev = pl.BlockSpec((1, CONV_PAD, d_conv), lambda b, t: (b, jnp.maximum(t * per - 1, 0), 0))
    flat = lambda width: pl.BlockSpec((tt, width), lambda b, t: (b * n_t + t, 0))
    view = lambda dil, width: pl.BlockSpec((1, tt // dil, dil * width), lambda b, t: (b, t, 0))
    ins = [attn[1][0].reshape(batch * seq, D_ATTN), attn[1][1].reshape(batch * seq, LANES)]
    specs = [flat(D_ATTN), flat(LANES)]
    scratch = [pltpu.VMEM((SUBLANES, CONV_PAD + tt, d_conv), F32)]
    for dil in dilations:
        ins += list(attn[dil])
        specs += [view(dil, D_ATTN), view(dil, LANES)]
        scratch += [_tiles_scratch(tt, D_ATTN), _tiles_scratch(tt, LANES)]
    expand = np.zeros((LANES, D_ATTN), np.float32)
    expand[:N_HEADS] = np.kron(np.eye(N_HEADS), np.ones((1, HEAD_DIM)))
    cast_specs = []
    for w in weights:
        rows, hold = _cast_blocking(w.shape[0], batch * n_t)
        cast_specs.append(pl.BlockSpec((rows, w.shape[1]), lambda b, t, hold=hold: ((b * n_t + t) // hold, 0)))
    res = pl.pallas_call(
        functools.partial(_mix_kernel, tt=tt, dilations=dilations, n_cast=len(weights)),
        grid=(batch, n_t),
        in_specs=[cur, prev, _const_spec(conv_w.shape), _const_spec((1, d_conv)),
                  _const_spec((1, d_conv)), _const_spec((1, d_conv)),
                  _const_spec((LANES, D_ATTN))] + specs + cast_specs,
        out_specs=[pl.BlockSpec((tt, D_ATTN + d_conv), lambda b, t: (b * n_t + t, 0))] + cast_specs,
        out_shape=[jax.ShapeDtypeStruct((batch * seq, D_ATTN + d_conv), BF16)]
        + [jax.ShapeDtypeStruct(w.shape, BF16) for w in weights],
        scratch_shapes=scratch,
        compiler_params=_params(2),
        name="mix_prompt",
    )(u3, u3, conv_w, conv_b, ln_g, ln_b, jnp.asarray(expand, BF16), *ins, *weights)
    return res[0], res[1:]


def _sample_conv_kernel(st_ref, u_ref, w_ref, b_ref, lg_ref, lb_ref, c_ref, ns_ref, us_ref, cs_ref,
                        *, nb, t_new):
    _put_tiles(us_ref, u_ref[...])
    new = [_get_rows(us_ref, t, nb, t_new) for t in range(t_new)]

    def ext(tau):
        return st_ref[tau] if tau < CONV_LEFT else new[tau - CONV_LEFT]

    for t in range(t_new):
        acc = None
        for j in range(CONV_WIDTH):
            term = ext(t + j) * w_ref[j:j + 1, :]
            acc = term if acc is None else acc + term
        c = _ln_swish(acc + b_ref[...], lg_ref[...], lb_ref[...])
        _put_rows(cs_ref, t, t_new, c)
    c_ref[...] = _get_tiles(cs_ref)
    for tau in range(CONV_LEFT):
        ns_ref[tau] = ext(tau + t_new)


def _sample_conv(state_t, u_s, conv_w, conv_b, ln_g, ln_b, layer, t_new, nb):
    _, n, d_conv = state_t.shape
    return pl.pallas_call(
        functools.partial(_sample_conv_kernel, nb=nb, t_new=t_new),
        grid=(n // nb,),
        in_specs=[pl.BlockSpec((CONV_LEFT, nb, d_conv), lambda i: (layer, i, 0)),
                  pl.BlockSpec((nb * t_new, d_conv), lambda i: (i, 0)),
                  _const_spec(conv_w.shape), _const_spec((1, d_conv)),
                  _const_spec((1, d_conv)), _const_spec((1, d_conv))],
        out_specs=[pl.BlockSpec((nb * t_new, d_conv), lambda i: (i, 0)),
                   pl.BlockSpec((CONV_LEFT, nb, d_conv), lambda i: (0, i, 0))],
        out_shape=[jax.ShapeDtypeStruct((n * t_new, d_conv), F32),
                   jax.ShapeDtypeStruct((CONV_LEFT, n, d_conv), F32)],
        scratch_shapes=[_tiles_scratch(nb * t_new, d_conv)] * 2,
        compiler_params=_params(1),
        name="sample_conv",
    )(state_t, u_s, conv_w, conv_b, ln_g, ln_b)


def _multiplicity(dist):
    dist = np.asarray(dist)
    c = np.zeros(dist.shape, np.float32)
    for window, dilation in PATTERNS:
        c += ((dist >= 0) & (dist <= window) & (dist % dilation == 0)).astype(np.float32)
    return c


def _shift_append(old, new_rows, out_ref, t_new):
    n_ch, l_buf = old.shape
    n_tiles = l_buf // LANES
    lane = lax.broadcasted_iota(jnp.int32, (n_ch, LANES), 1)
    keep = lane < LANES - t_new
    pad = jnp.concatenate([jnp.zeros((LANES - t_new, n_ch), F32), new_rows], axis=0)
    nxt = pad.T
    for j in reversed(range(n_tiles)):
        cur = pltpu.roll(old[:, j * LANES:(j + 1) * LANES], LANES - t_new, axis=1)
        out_ref[0, :, j * LANES:(j + 1) * LANES] = jnp.where(keep, cur, nxt)
        nxt = cur


CACHE_HEADS_PER_UNIT = 4


def _sample_attend(q_ref, kn_ref, vn_ref, ck, cv, cnt_ref, attn_ref, t_new):
    width = ck.shape[0]
    n_heads = width // HEAD_DIM
    rows = n_heads * t_new
    kn = kn_ref[0]
    vn = vn_ref[0]

    row = lax.broadcasted_iota(jnp.int32, (rows, width), 0)
    lane = lax.broadcasted_iota(jnp.int32, (rows, width), 1)
    own_head = (row // t_new) == (lane // HEAD_DIM)
    q_rep = jnp.concatenate([q_ref[0]] * n_heads, axis=0)
    q_bd = jnp.where(own_head, q_rep, 0.0)

    cnt = cnt_ref[...]
    s_c = jnp.dot(q_bd.astype(BF16), ck.astype(BF16), preferred_element_type=F32)
    s_c = jnp.where(cnt > 0.0, s_c, NEG_BIG)
    m = jnp.max(s_c, axis=-1, keepdims=True)

    t_of_row = lax.broadcasted_iota(jnp.int32, (rows, 1), 0) % t_new
    s_n, c_n = [], []
    for tp in range(t_new):
        d = t_of_row - tp
        c = jnp.zeros((rows, 1), F32)
        for _, dilation in PATTERNS:
            c = c + jnp.where((d >= 0) & (d % dilation == 0), 1.0, 0.0)
        s = jnp.sum(q_bd * kn[tp:tp + 1, :], axis=-1, keepdims=True)
        s = jnp.where(c > 0.0, s, NEG_BIG)
        m = jnp.maximum(m, s)
        s_n.append(s)
        c_n.append(c)

    p_c = cnt * jnp.exp2(s_c - m)
    l = jnp.sum(p_c, axis=-1, keepdims=True)
    acc = lax.dot_general(p_c.astype(BF16), cv.astype(BF16), (((1,), (1,)), ((), ())),
                          preferred_element_type=F32)
    for tp in range(t_new):
        p = c_n[tp] * jnp.exp2(s_n[tp] - m)
        l = l + p
        acc = acc + p * vn[tp:tp + 1, :]
    acc = jnp.where(own_head, acc * (1.0 / l), 0.0)
    out = acc[0:t_new, :]
    for h in range(1, n_heads):
        out = out + acc[h * t_new:(h + 1) * t_new, :]
    attn_ref[0] = out.astype(attn_ref.dtype)


CACHE_RING = 3


def _swiglu_chunk(h, wg_ref, wu_ref, wd_ref, j):
    cols = pl.ds(pl.multiple_of(j * MXU_COLS, MXU_COLS), MXU_COLS)
    gate = jnp.dot(h, wg_ref[:, cols], preferred_element_type=F32)
    up = jnp.dot(h, wu_ref[:, cols], preferred_element_type=F32)
    act = (gate * _sigmoid(gate) * up).astype(BF16)
    return jnp.dot(act, wd_ref[j], preferred_element_type=F32)


def _ffn_cache_kernel(x_ref, mix_ref, wo_ref, g_ref, wg_ref, wu_ref, wd_ref,
                      q_ref, kn_ref, vn_ref, ck_hbm, cv_hbm, cnt_ref,
                      y_ref, attn_ref, ok_ref, ov_ref, h_ref, kbuf, vbuf, sem,
                      *, steps_per_tile, t_new, first_row, units_per_row):
    u = pl.program_id(0)
    n_units = pl.num_programs(0)
    c = u % steps_per_tile
    base, extra = divmod(wd_ref.shape[0], steps_per_tile)
    uw = kbuf.shape[1]

    def fetch(unit, slot):
        row = first_row + unit // units_per_row
        h0 = pl.multiple_of((unit % units_per_row) * uw, uw)
        return (pltpu.make_async_copy(ck_hbm.at[row, pl.ds(h0, uw), :], kbuf.at[slot], sem.at[0, slot]),
                pltpu.make_async_copy(cv_hbm.at[row, pl.ds(h0, uw), :], vbuf.at[slot], sem.at[1, slot]))

    @pl.when(u == 0)
    def _():
        for unit in range(CACHE_RING - 1):
            for cp in fetch(unit, unit):
                cp.start()

    ahead = u + (CACHE_RING - 1)

    @pl.when(ahead < n_units)
    def _():
        for cp in fetch(ahead, ahead % CACHE_RING):
            cp.start()

    @pl.when(c == 0)
    def _():
        x1 = x_ref[...] + jnp.dot(mix_ref[...], wo_ref[...], preferred_element_type=F32)
        y_ref[...] = x1
        h_ref[...] = _rms_rows(x1, g_ref[...]).astype(BF16)

    slot = u % CACHE_RING
    for cp in fetch(u, slot):
        cp.wait()
    ck, cv = kbuf[slot], vbuf[slot]
    _shift_append(ck, kn_ref[0], ok_ref, t_new)
    _shift_append(cv, vn_ref[0], ov_ref, t_new)
    _sample_attend(q_ref, kn_ref, vn_ref, ck, cv, cnt_ref, attn_ref, t_new)
    for i in range(base):
        y_ref[...] += _swiglu_chunk(h_ref[...], wg_ref, wu_ref, wd_ref, c * base + i)
    if extra:
        first = steps_per_tile - extra

        @pl.when(c >= first)
        def _():
            y_ref[...] += _swiglu_chunk(h_ref[...], wg_ref, wu_ref, wd_ref, steps_per_tile * base + c - first)


def _ffn_with_cache(x2d, mix, wo_b, g, wg3, wu3, wd3, q_s, k_s, v_s, cache_kt, cache_vt, layer, tm):
    m, d = x2d.shape
    n, t_new, _ = q_s.shape
    l_buf = cache_kt.shape[2]
    assert l_buf == WIN_MAX and t_new % SUBLANES == 0
    uw = CACHE_HEADS_PER_UNIT * HEAD_DIM
    per_n = D_ATTN // uw
    n_tiles, n_units = m // tm, n * per_n
    assert n_units % n_tiles == 0
    steps_per_tile = n_units // n_tiles
    assert wd3.shape[0] >= steps_per_tile and n_units >= CACHE_RING
    dist = l_buf + np.arange(t_new)[:, None] - np.arange(l_buf)[None, :]
    cnt = np.tile(_multiplicity(dist), (CACHE_HEADS_PER_UNIT, 1))

    tile = lambda width: pl.BlockSpec((tm, width), lambda u: (u // steps_per_tile, 0))
    new = pl.BlockSpec((1, t_new, uw), lambda u: (u // per_n, 0, u % per_n))
    big = pl.BlockSpec((1, uw, l_buf), lambda u: (u // per_n, u % per_n, 0))
    in_hbm = pl.BlockSpec(memory_space=pl.ANY)
    return pl.pallas_call(
        functools.partial(_ffn_cache_kernel, steps_per_tile=steps_per_tile, t_new=t_new,
                          first_row=layer * n, units_per_row=per_n),
        grid=(n_units,),
        in_specs=[tile(d), tile(mix.shape[1]), _const_spec(wo_b.shape), _const_spec((1, d)),
                  _const_spec(wg3.shape), _const_spec(wu3.shape), _const_spec(wd3.shape),
                  new, new, new, in_hbm, in_hbm, _const_spec(cnt.shape)],
        out_specs=[tile(d), new, big, big],
        out_shape=[jax.ShapeDtypeStruct((m, d), F32),
                   jax.ShapeDtypeStruct((n, t_new, D_ATTN), F32),
                   jax.ShapeDtypeStruct((n, D_ATTN, l_buf), F32),
                   jax.ShapeDtypeStruct((n, D_ATTN, l_buf), F32)],
        scratch_shapes=[pltpu.VMEM((tm, d), BF16),
                        pltpu.VMEM((CACHE_RING, uw, l_buf), F32), pltpu.VMEM((CACHE_RING, uw, l_buf), F32),
                        pltpu.SemaphoreType.DMA((2, CACHE_RING))],
        compiler_params=_params(1),
        name="ffn_cache",
    )(x2d, mix, wo_b, g, wg3, wu3, wd3, q_s, k_s, v_s, cache_kt, cache_vt, jnp.asarray(cnt))


def _ffn_kernel(x_ref, *refs, n_mix):
    mix_refs = refs[0:n_mix]
    wo_ref, g_ref, wg_ref, wu_ref, wd_ref, y_ref = refs[n_mix:]
    x1 = x_ref[...]
    row = 0
    for mr in mix_refs:
        width = mr.shape[1]
        x1 = x1 + jnp.dot(mr[...].astype(BF16), wo_ref[row:row + width, :], preferred_element_type=F32)
        row += width
    h = _rms_rows(x1, g_ref[...]).astype(BF16)
    y_ref[...] = x1
    for j in range(wd_ref.shape[0]):
        y_ref[...] += _swiglu_chunk(h, wg_ref, wu_ref, wd_ref, j)


def _outproj_ffn(x2d, mix_parts, wo_b, g, wg3, wu3, wd3, tm):
    m, d = x2d.shape
    tm = min(tm, m)
    row = lambda i: (i, 0)
    return pl.pallas_call(
        functools.partial(_ffn_kernel, n_mix=len(mix_parts)),
        grid=(m // tm,),
        in_specs=[pl.BlockSpec((tm, d), row)] + [pl.BlockSpec((tm, p.shape[1]), row) for p in mix_parts]
        + [_const_spec(wo_b.shape), _const_spec((1, d)),
           _const_spec(wg3.shape), _const_spec(wu3.shape), _const_spec(wd3.shape)],
        out_specs=pl.BlockSpec((tm, d), row),
        out_shape=jax.ShapeDtypeStruct((m, d), F32),
        compiler_params=_params(1),
        name="outproj_ffn",
    )(x2d, *mix_parts, wo_b, g, wg3, wu3, wd3)


TOKEN_TILE = 512
MIX_TIME_TILE = 512
SAMPLE_CONV_BATCH = 32


def kernel(x_prompt, x_sample, cache_k, cache_v, state_conv, attn_norm_g, w_in, q_norm_g, k_norm_g,
           conv_w, conv_b, conv_ln_g, conv_ln_b, w_out, ffn_norm_g, w_gate, w_up, w_down):
    batch, seq, d_model = x_prompt.shape
    n_dec, t_new, _ = x_sample.shape
    depth = w_in.shape[0]
    d_conv = conv_w.shape[-1]
    l_buf = cache_k.shape[2]
    assert seq == WIN_MAX and l_buf == WIN_MAX and seq % (BLK * PATTERNS[-1][1]) == 0
    assert w_in.shape[-1] == 3 * D_ATTN + 2 * d_conv and d_conv == D_ATTN

    seg = jnp.asarray(np.kron(np.eye(N_HEADS), np.ones((HEAD_DIM, HEAD_DIM))), BF16)
    row = lambda v: v.reshape(1, -1)
    to_t = lambda c: jnp.transpose(c, (0, 1, 3, 4, 2)).reshape(depth * n_dec, D_ATTN, l_buf)
    from_t = lambda c, n: jnp.transpose(c.reshape(n, N_HEADS, HEAD_DIM, -1), (0, 3, 1, 2))
    cache_kt, cache_vt = to_t(cache_k), to_t(cache_v)
    state_t = jnp.transpose(state_conv, (0, 2, 1, 3)).reshape(depth * CONV_LEFT, n_dec, d_conv)

    y_p = x_prompt.reshape(batch * seq, d_model)
    y_s = x_sample.reshape(n_dec * t_new, d_model)
    outs = [[] for _ in range(6)]
    for l in range(depth):
        w_in_b = w_in[l].astype(BF16)
        g_attn, g_ffn = row(attn_norm_g[l]), row(ffn_norm_g[l])
        qg = row(jnp.tile(q_norm_g[l], N_HEADS))
        kg = row(jnp.tile(k_norm_g[l], N_HEADS))
        cb, lg, lb = row(conv_b[l]), row(conv_ln_g[l]), row(conv_ln_b[l])

        q, k, v, u_s = _inproj_sample(y_s, g_attn, w_in_b, qg, kg, seg, TOKEN_TILE)
        c_s, ns = _sample_conv(state_t, u_s, conv_w[l], cb, lg, lb, l, t_new, min(SAMPLE_CONV_BATCH, n_dec))
        qkv, kt, vt, u = _inproj_prompt(y_p, g_attn, w_in_b, qg, kg, seg, batch, seq, TOKEN_TILE)
        attn = {dil: _band_attention(*qkv[dil], w, dil) for w, dil in PATTERNS}
        mix, (wo_b, wg3, wu3, wd_b) = _mix_prompt(u, conv_w[l], cb, lg, lb, attn,
                                                   (w_out[l], w_gate[l], w_up[l], w_down[l]),
                                                   batch, seq, MIX_TIME_TILE)
        wd3 = wd_b.reshape(wd_b.shape[0] // MXU_COLS, MXU_COLS, d_model)

        as3 = lambda t: t.reshape(n_dec, t_new, D_ATTN)
        y_p, attn_s, nk, nv = _ffn_with_cache(y_p, mix, wo_b, g_ffn, wg3, wu3, wd3,
                                              as3(q), as3(k), as3(v), cache_kt, cache_vt, l, TOKEN_TILE)
        y_s = _outproj_ffn(y_s, [attn_s.reshape(n_dec * t_new, D_ATTN), c_s],
                           wo_b, g_ffn, wg3, wu3, wd3, TOKEN_TILE)
        outs[0].append(from_t(kt, batch))
        outs[1].append(from_t(vt, batch))
        outs[2].append(u.reshape(batch, seq, d_conv)[:, seq - CONV_LEFT:])
        outs[3].append(from_t(nk, n_dec))
        outs[4].append(from_t(nv, n_dec))
        outs[5].append(jnp.transpose(ns, (1, 0, 2)))

    stack = lambda xs: xs[0][None] if len(xs) == 1 else jnp.stack(xs)
    return (y_p.reshape(batch, seq, d_model), y_s.reshape(n_dec, t_new, d_model)) + tuple(
        stack(o) for o in outs)
```

```python
import functools

import numpy as np
import jax
import jax.numpy as jnp
from jax import lax
from jax.experimental import pallas as pl
from jax.experimental.pallas import tpu as pltpu

N_HEADS = 8
HEAD_DIM = 64
D_ATTN = N_HEADS * HEAD_DIM
PATTERNS = ((128, 1), (512, 4), (2048, 16))
WIN_MAX = max(w for w, _ in PATTERNS)
CONV_WIDTH = 31
CONV_LEFT = CONV_WIDTH - 1
BLK = 128
RMS_EPS = 1e-6
LN_EPS = 1e-5
ATTN_SCALE = HEAD_DIM ** -0.5
LOG2E = 1.4426950408889634
NEG_BIG = -1e30

LANES = 128
MXU_COLS = 256
HEADS_PER_LANE_TILE = LANES // HEAD_DIM
VMEM_LIMIT_BYTES = 56 * 1024 * 1024

F32 = jnp.float32
BF16 = jnp.bfloat16


def _params(n_axes):
    return pltpu.CompilerParams(dimension_semantics=("arbitrary",) * n_axes,
                                vmem_limit_bytes=VMEM_LIMIT_BYTES)


def _const_spec(shape):
    return pl.BlockSpec(shape, lambda *_: (0,) * len(shape), pipeline_mode=pl.Buffered(1))


def _sigmoid(x):
    return 1.0 / (1.0 + jnp.exp(-x))


def _rms_rows(x, g):
    ms = jnp.mean(x * x, axis=-1, keepdims=True)
    return x * lax.rsqrt(ms + RMS_EPS) * g


def _class_lanes(c):
    return slice(c * D_ATTN, (c + 1) * D_ATTN)


def _tiles_scratch(rows, width):
    return pltpu.VMEM((width // LANES, rows, LANES), F32)


def _put_tiles(scr, x):
    for lt in range(scr.shape[0]):
        scr[lt] = x[:, lt * LANES:(lt + 1) * LANES]


def _get_tiles(scr):
    return jnp.concatenate([scr[lt] for lt in range(scr.shape[0])], axis=1)


def _get_rows(scr, start, n, stride):
    return jnp.concatenate([scr[lt, pl.ds(start, n, stride=stride), :] for lt in range(scr.shape[0])],
                           axis=1)


def _put_rows(scr, start, stride, x):
    for lt in range(scr.shape[0]):
        scr[lt, pl.ds(start, x.shape[0], stride=stride), :] = x[:, lt * LANES:(lt + 1) * LANES]


def _inproj_kernel(x_ref, g_ref, w_ref, qg_ref, kg_ref, seg_ref, *refs, dilations):
    h = _rms_rows(x_ref[...], g_ref[...]).astype(BF16)

    def proj(col):
        return jnp.dot(h, w_ref[:, col * D_ATTN:(col + 1) * D_ATTN], preferred_element_type=F32)

    def head_rms(t, gain):
        ss = jnp.dot((t * t).astype(BF16), seg_ref[...], preferred_element_type=F32)
        return t * lax.rsqrt(ss * (1.0 / HEAD_DIM) + RMS_EPS) * gain

    make = (lambda: head_rms(proj(0), qg_ref[...]) * (ATTN_SCALE * LOG2E),
            lambda: head_rms(proj(1), kg_ref[...]),
            lambda: proj(2))
    if not dilations:
        for ref, f in zip(refs, make):
            ref[...] = f()
        refs[3][...] = proj(3) * _sigmoid(proj(4))
        return

    n_views = 3 * len(dilations)
    nat = refs[0:3]
    views = refs[3:3 + n_views]
    t_refs = (None,) + refs[3 + n_views:5 + n_views]
    u_ref = refs[5 + n_views]
    scratch = refs[6 + n_views:]
    tm = x_ref.shape[0]
    for i, f in enumerate(make):
        t = f()
        if t_refs[i] is not None:
            t_refs[i][0] = t.T
        nat[i][...] = t.astype(BF16)
        _put_tiles(scratch[i], t)
        for j, dil in enumerate(dilations):
            for c in range(dil):
                views[3 * j + i][0, :, _class_lanes(c)] = (
                    _get_rows(scratch[i], c, tm // dil, dil).astype(BF16))
    u_ref[...] = proj(3) * _sigmoid(proj(4))


def _inproj_sample(x2d, g, w_in_b, qg, kg, seg, tm):
    m, d = x2d.shape
    tm = min(tm, m)
    row = lambda i: (i, 0)
    out_f = jax.ShapeDtypeStruct((m, D_ATTN), F32)
    tile = pl.BlockSpec((tm, D_ATTN), row)
    return pl.pallas_call(
        functools.partial(_inproj_kernel, dilations=()),
        grid=(m // tm,),
        in_specs=[pl.BlockSpec((tm, d), row), _const_spec((1, d)), _const_spec(w_in_b.shape),
                  _const_spec((1, D_ATTN)), _const_spec((1, D_ATTN)), _const_spec((D_ATTN, D_ATTN))],
        out_specs=[tile] * 4,
        out_shape=[out_f] * 4,
        compiler_params=_params(1),
        name="inproj_sample",
    )(x2d, g, w_in_b, qg, kg, seg)


def _inproj_prompt(x2d, g, w_in_b, qg, kg, seg, batch, seq, tm):
    m, d = x2d.shape
    n_t = seq // tm
    dilations = tuple(dil for _, dil in PATTERNS if dil > 1)
    flat = lambda b, t: (b * n_t + t, 0)
    nat_spec = pl.BlockSpec((tm, D_ATTN), flat)
    nat_b = jax.ShapeDtypeStruct((m, D_ATTN), BF16)
    view_specs, view_shapes = [], []
    for dil in dilations:
        view_specs += [pl.BlockSpec((1, tm // dil, dil * D_ATTN), lambda b, t: (b, t, 0))] * 3
        view_shapes += [jax.ShapeDtypeStruct((batch, seq // dil, dil * D_ATTN), BF16)] * 3
    t_spec = pl.BlockSpec((1, D_ATTN, tm), lambda b, t: (b, 0, t))
    t_shape = jax.ShapeDtypeStruct((batch, D_ATTN, seq), F32)
    res = pl.pallas_call(
        functools.partial(_inproj_kernel, dilations=dilations),
        grid=(batch, n_t),
        in_specs=[pl.BlockSpec((tm, d), flat), _const_spec((1, d)), _const_spec(w_in_b.shape),
                  _const_spec((1, D_ATTN)), _const_spec((1, D_ATTN)), _const_spec((D_ATTN, D_ATTN))],
        out_specs=[nat_spec] * 3 + view_specs + [t_spec, t_spec, nat_spec],
        out_shape=[nat_b] * 3 + view_shapes + [t_shape, t_shape, jax.ShapeDtypeStruct((m, D_ATTN), F32)],
        scratch_shapes=[_tiles_scratch(tm, D_ATTN)] * 3,
        compiler_params=_params(2),
        name="inproj_prompt",
    )(x2d, g, w_in_b, qg, kg, seg)
    n_views = 3 * len(dilations)
    qkv = {1: tuple(t.reshape(batch, seq, D_ATTN) for t in res[0:3])}
    for j, dil in enumerate(dilations):
        qkv[dil] = tuple(res[3 + 3 * j:6 + 3 * j])
    kt, vt, u = res[3 + n_views:]
    return qkv, kt, vt, u


def _band_attn_kernel(q_ref, k_ref, v_ref, bias0_ref, bias_ref, o_ref, lse_ref, *, n_blocks, n_classes):
    lane = lax.broadcasted_iota(jnp.int32, (BLK, LANES), 1)
    upper = lane >= HEAD_DIM
    n_pairs = N_HEADS // HEADS_PER_LANE_TILE

    def block(cl, r0, k0, n_keys, bias):
        vps, scores = [], []
        for hp in range(n_pairs):
            lanes = slice(cl * D_ATTN + hp * LANES, cl * D_ATTN + (hp + 1) * LANES)
            qp = q_ref[0, pl.ds(r0, BLK), lanes]
            kp = k_ref[0, pl.ds(k0, n_keys), lanes]
            vps.append(v_ref[0, pl.ds(k0, n_keys), lanes])
            q2 = jnp.concatenate([jnp.where(~upper, qp, jnp.zeros_like(qp)),
                                  jnp.where(upper, qp, jnp.zeros_like(qp))], axis=0)
            s2 = lax.dot_general(q2, kp, (((1,), (1,)), ((), ())), preferred_element_type=F32)
            scores += [s2[hh * BLK:(hh + 1) * BLK] + bias for hh in range(HEADS_PER_LANE_TILE)]
        probs, inv_l = [], []
        lse_tile = jnp.zeros((BLK, LANES), F32)
        for h, s in enumerate(scores):
            m = jnp.max(s, axis=-1, keepdims=True)
            p = jnp.exp2(s - m)
            l = jnp.sum(p, axis=-1, keepdims=True)
            probs.append(p.astype(BF16))
            inv_l.append(1.0 / l)
            lse_tile = jnp.where(lane == h, m + jnp.log(l) * LOG2E, lse_tile)
        lse_ref[0, pl.ds(r0, BLK), cl * LANES:(cl + 1) * LANES] = lse_tile
        for hp in range(n_pairs):
            lanes = slice(cl * D_ATTN + hp * LANES, cl * D_ATTN + (hp + 1) * LANES)
            o2 = jnp.dot(jnp.concatenate(probs[2 * hp:2 * hp + 2], axis=0), vps[hp],
                         preferred_element_type=F32)
            o_lo, o_hi = (o2[hh * BLK:(hh + 1) * BLK] * inv_l[2 * hp + hh]
                          for hh in range(HEADS_PER_LANE_TILE))
            o_ref[0, pl.ds(r0, BLK), lanes] = jnp.where(upper, o_hi, o_lo).astype(o_ref.dtype)

    for cl in range(n_classes):
        block(cl, 0, 0, BLK, bias0_ref[...])
        if n_blocks > 1:
            def body(i, carry, cl=cl):
                r0 = pl.multiple_of(i * BLK, BLK)
                block(cl, r0, pl.multiple_of(r0 - BLK, BLK), 2 * BLK, bias_ref[...])
                return carry

            lax.fori_loop(1, n_blocks, body, 0)


def _band_biases(n_sub):
    qi = np.arange(BLK)[:, None]
    ki = np.arange(2 * BLK)[None, :]
    dist = BLK + qi - ki
    band = (dist >= 0) & (dist <= n_sub)
    d0 = qi - np.arange(BLK)[None, :]
    to_bias = lambda ok: np.where(ok, 0.0, NEG_BIG).astype(np.float32)
    return to_bias((d0 >= 0) & (d0 <= n_sub)), to_bias(band)


BAND_CLASSES_PER_STEP = 8


def _band_attention(q, k, v, window, dilation):
    batch, length, _ = q.shape
    cps = min(BAND_CLASSES_PER_STEP, dilation)
    bias0, bias = _band_biases(window // dilation)
    blk = pl.BlockSpec((1, length, cps * D_ATTN), lambda b, c: (b, 0, c))
    return pl.pallas_call(
        functools.partial(_band_attn_kernel, n_blocks=length // BLK, n_classes=cps),
        grid=(batch, dilation // cps),
        in_specs=[blk, blk, blk, _const_spec((BLK, BLK)), _const_spec((BLK, 2 * BLK))],
        out_specs=[blk, pl.BlockSpec((1, length, cps * LANES), lambda b, c: (b, 0, c))],
        out_shape=[jax.ShapeDtypeStruct(q.shape, BF16),
                   jax.ShapeDtypeStruct((batch, length, dilation * LANES), F32)],
        compiler_params=_params(2),
        name=f"band_attn_d{dilation}",
    )(q, k, v, jnp.asarray(bias0), jnp.asarray(bias))


CONV_PAD = 32
SUBLANES = 8


def _mix_kernel(u_ref, up_ref, w_ref, b_ref, lg_ref, lb_ref, expand_ref, *refs, tt, dilations, n_cast):
    n_pat = 1 + len(dilations)
    ins = refs[0:2 * n_pat]
    cast_in = refs[2 * n_pat:2 * n_pat + n_cast]
    mix_ref = refs[2 * n_pat + n_cast]
    cast_out = refs[2 * n_pat + n_cast + 1:2 * n_pat + 2 * n_cast + 1]
    ext_ref = refs[2 * n_pat + 2 * n_cast + 1]
    nat = refs[2 * n_pat + 2 * n_cast + 2:]
    for src, dst in zip(cast_in, cast_out):
        dst[...] = src[...].astype(dst.dtype)
    t = pl.program_id(1)
    rows = CONV_PAD + tt
    prev = up_ref[0]
    ext_ref[0, 0:CONV_PAD, :] = jnp.where(t > 0, prev, jnp.zeros_like(prev))
    ext_ref[0, CONV_PAD:rows, :] = u_ref[0]
    for s in range(1, SUBLANES):
        ext_ref[s, 0:rows - SUBLANES, :] = ext_ref[0, pl.ds(s, rows - SUBLANES), :]

    for j, dil in enumerate(dilations):
        for i, width in enumerate((D_ATTN, LANES)):
            src, dst = ins[2 + 2 * j + i], nat[2 * j + i]
            for c in range(dil):
                _put_rows(dst, c, dil, src[0, :, c * width:(c + 1) * width].astype(F32))
    os_ = [ins[0][...].astype(F32)] + [_get_tiles(nat[2 * j]) for j in range(len(dilations))]
    ls_ = [ins[1][...]] + [_get_tiles(nat[2 * j + 1]) for j in range(len(dilations))]
    mx = functools.reduce(jnp.maximum, ls_)
    es = [jnp.exp2(l - mx) for l in ls_]
    inv = 1.0 / sum(es)
    attn = None
    for e, o in zip(es, os_):
        w = e * inv
        hi = w.astype(BF16)
        lo = (w - hi.astype(F32)).astype(BF16)
        wide = (jnp.dot(hi, expand_ref[...], preferred_element_type=F32)
                + jnp.dot(lo, expand_ref[...], preferred_element_type=F32))
        attn = wide * o if attn is None else attn + wide * o
    mix_ref[:, 0:D_ATTN] = attn.astype(mix_ref.dtype)

    d_conv = u_ref.shape[-1]
    acc = None
    for j in range(CONV_WIDTH):
        off = CONV_PAD - CONV_LEFT + j
        s = off % SUBLANES
        term = ext_ref[s, pl.ds(off - s, tt), :] * w_ref[j:j + 1, :]
        acc = term if acc is None else acc + term
    c = _ln_swish(acc + b_ref[...], lg_ref[...], lb_ref[...])
    mix_ref[:, D_ATTN:D_ATTN + d_conv] = c.astype(mix_ref.dtype)


def _ln_swish(c, g, b):
    mu = jnp.mean(c, axis=-1, keepdims=True)
    cc = c - mu
    var = jnp.mean(cc * cc, axis=-1, keepdims=True)
    y = cc * lax.rsqrt(var + LN_EPS) * g + b
    return y * _sigmoid(y)


def _cast_blocking(rows, n_steps):
    for n_blocks in range(n_steps, 0, -1):
        if n_steps % n_blocks == 0 and rows % (n_blocks * 2 * SUBLANES) == 0:
            return rows // n_blocks, n_steps // n_blocks
    raise ValueError((rows, n_steps))


def _mix_prompt(u, conv_w, conv_b, ln_g, ln_b, attn, weights, batch, seq, tt):
    d_conv = u.shape[-1]
    u3 = u.reshape(batch, seq, d_conv)
    n_t = seq // tt
    per = tt // CONV_PAD
    dilations = tuple(dil for _, dil in PATTERNS if dil > 1)
    cur = pl.BlockSpec((1, tt, d_conv), lambda b, t: (b, t, 0))
    prev = pl.BlockSpec((1, CONV_PAD, d_conv), lambda b, t: (b, jnp.maximum(t * per - 1, 0), 0))
    flat = lambda width: pl.BlockSpec((tt, width), lambda b, t: (b * n_t + t, 0))
    view = lambda dil, width: pl.BlockSpec((1, tt // dil, dil * width), lambda b, t: (b, t, 0))
    ins = [attn[1][0].reshape(batch * seq, D_ATTN), attn[1][1].reshape(batch * seq, LANES)]
    specs = [flat(D_ATTN), flat(LANES)]
    scratch = [pltpu.VMEM((SUBLANES, CONV_PAD + tt, d_conv), F32)]
    for dil in dilations:
        ins += list(attn[dil])
        specs += [view(dil, D_ATTN), view(dil, LANES)]
        scratch += [_tiles_scratch(tt, D_ATTN), _tiles_scratch(tt, LANES)]
    expand = np.zeros((LANES, D_ATTN), np.float32)
    expand[:N_HEADS] = np.kron(np.eye(N_HEADS), np.ones((1, HEAD_DIM)))
    cast_specs = []
    for w in weights:
        rows, hold = _cast_blocking(w.shape[0], batch * n_t)
        cast_specs.append(pl.BlockSpec((rows, w.shape[1]), lambda b, t, hold=hold: ((b * n_t + t) // hold, 0)))
    res = pl.pallas_call(
        functools.partial(_mix_kernel, tt=tt, dilations=dilations, n_cast=len(weights)),
        grid=(batch, n_t),
        in_specs=[cur, prev, _const_spec(conv_w.shape), _const_spec((1, d_conv)),
                  _const_spec((1, d_conv)), _const_spec((1, d_conv)),
                  _const_spec((LANES, D_ATTN))] + specs + cast_specs,
        out_specs=[pl.BlockSpec((tt, D_ATTN + d_conv), lambda b, t: (b * n_t + t, 0))] + cast_specs,
        out_shape=[jax.ShapeDtypeStruct((batch * seq, D_ATTN + d_conv), BF16)]
        + [jax.ShapeDtypeStruct(w.shape, BF16) for w in weights],
        scratch_shapes=scratch,
        compiler_params=_params(2),
        name="mix_prompt",
    )(u3, u3, conv_w, conv_b, ln_g, ln_b, jnp.asarray(expand, BF16), *ins, *weights)
    return res[0], res[1:]


def _sample_conv_kernel(st_ref, u_ref, w_ref, b_ref, lg_ref, lb_ref, c_ref, ns_ref, us_ref, cs_ref,
                        *, nb, t_new):
    _put_tiles(us_ref, u_ref[...])
    new = [_get_rows(us_ref, t, nb, t_new) for t in range(t_new)]

    def ext(tau):
        return st_ref[tau] if tau < CONV_LEFT else new[tau - CONV_LEFT]

    for t in range(t_new):
        acc = None
        for j in range(CONV_WIDTH):
            term = ext(t + j) * w_ref[j:j + 1, :]
            acc = term if acc is None else acc + term
        c = _ln_swish(acc + b_ref[...], lg_ref[...], lb_ref[...])
        _put_rows(cs_ref, t, t_new, c)
    c_ref[...] = _get_tiles(cs_ref)
    for tau in range(CONV_LEFT):
        ns_ref[tau] = ext(tau + t_new)


def _sample_conv(state_t, u_s, conv_w, conv_b, ln_g, ln_b, layer, t_new, nb):
    _, n, d_conv = state_t.shape
    return pl.pallas_call(
        functools.partial(_sample_conv_kernel, nb=nb, t_new=t_new),
        grid=(n // nb,),
        in_specs=[pl.BlockSpec((CONV_LEFT, nb, d_conv), lambda i: (layer, i, 0)),
                  pl.BlockSpec((nb * t_new, d_conv), lambda i: (i, 0)),
                  _const_spec(conv_w.shape), _const_spec((1, d_conv)),
                  _const_spec((1, d_conv)), _const_spec((1, d_conv))],
        out_specs=[pl.BlockSpec((nb * t_new, d_conv), lambda i: (i, 0)),
                   pl.BlockSpec((CONV_LEFT, nb, d_conv), lambda i: (0, i, 0))],
        out_shape=[jax.ShapeDtypeStruct((n * t_new, d_conv), F32),
                   jax.ShapeDtypeStruct((CONV_LEFT, n, d_conv), F32)],
        scratch_shapes=[_tiles_scratch(nb * t_new, d_conv)] * 2,
        compiler_params=_params(1),
        name="sample_conv",
    )(state_t, u_s, conv_w, conv_b, ln_g, ln_b)


def _multiplicity(dist):
    dist = np.asarray(dist)
    c = np.zeros(dist.shape, np.float32)
    for window, dilation in PATTERNS:
        c += ((dist >= 0) & (dist <= window) & (dist % dilation == 0)).astype(np.float32)
    return c


def _shift_append(old, new_rows, out_ref, t_new):
    n_ch, l_buf = old.shape
    n_tiles = l_buf // LANES
    lane = lax.broadcasted_iota(jnp.int32, (n_ch, LANES), 1)
    keep = lane < LANES - t_new
    pad = jnp.concatenate([jnp.zeros((LANES - t_new, n_ch), F32), new_rows], axis=0)
    nxt = pad.T
    for j in reversed(range(n_tiles)):
        cur = pltpu.roll(old[:, j * LANES:(j + 1) * LANES], LANES - t_new, axis=1)
        out_ref[0, :, j * LANES:(j + 1) * LANES] = jnp.where(keep, cur, nxt)
        nxt = cur


CACHE_HEADS_PER_UNIT = 4


def _sample_attend(q_ref, kn_ref, vn_ref, ck, cv, cnt_ref, attn_ref, t_new):
    width = ck.shape[0]
    n_heads = width // HEAD_DIM
    rows = n_heads * t_new
    kn = kn_ref[0]
    vn = vn_ref[0]

    row = lax.broadcasted_iota(jnp.int32, (rows, width), 0)
    lane = lax.broadcasted_iota(jnp.int32, (rows, width), 1)
    own_head = (row // t_new) == (lane // HEAD_DIM)
    q_rep = jnp.concatenate([q_ref[0]] * n_heads, axis=0)
    q_bd = jnp.where(own_head, q_rep, 0.0)

    cnt = cnt_ref[...]
    s_c = jnp.dot(q_bd.astype(BF16), ck.astype(BF16), preferred_element_type=F32)
    s_c = jnp.where(cnt > 0.0, s_c, NEG_BIG)
    m = jnp.max(s_c, axis=-1, keepdims=True)

    t_of_row = lax.broadcasted_iota(jnp.int32, (rows, 1), 0) % t_new
    s_n, c_n = [], []
    for tp in range(t_new):
        d = t_of_row - tp
        c = jnp.zeros((rows, 1), F32)
        for _, dilation in PATTERNS:
            c = c + jnp.where((d >= 0) & (d % dilation == 0), 1.0, 0.0)
        s = jnp.sum(q_bd * kn[tp:tp + 1, :], axis=-1, keepdims=True)
        s = jnp.where(c > 0.0, s, NEG_BIG)
        m = jnp.maximum(m, s)
        s_n.append(s)
        c_n.append(c)

    p_c = cnt * jnp.exp2(s_c - m)
    l = jnp.sum(p_c, axis=-1, keepdims=True)
    acc = lax.dot_general(p_c.astype(BF16), cv.astype(BF16), (((1,), (1,)), ((), ())),
                          preferred_element_type=F32)
    for tp in range(t_new):
        p = c_n[tp] * jnp.exp2(s_n[tp] - m)
        l = l + p
        acc = acc + p * vn[tp:tp + 1, :]
    acc = jnp.where(own_head, acc * (1.0 / l), 0.0)
    out = acc[0:t_new, :]
    for h in range(1, n_heads):
        out = out + acc[h * t_new:(h + 1) * t_new, :]
    attn_ref[0] = out.astype(attn_ref.dtype)


CACHE_RING = 3


def _swiglu_chunk(h, wg_ref, wu_ref, wd_ref, j):
    cols = pl.ds(pl.multiple_of(j * MXU_COLS, MXU_COLS), MXU_COLS)
    gate = jnp.dot(h, wg_ref[:, cols], preferred_element_type=F32)
    up = jnp.dot(h, wu_ref[:, cols], preferred_element_type=F32)
    act = (gate * _sigmoid(gate) * up).astype(BF16)
    return jnp.dot(act, wd_ref[j], preferred_element_type=F32)


def _ffn_cache_kernel(x_ref, mix_ref, wo_ref, g_ref, wg_ref, wu_ref, wd_ref,
                      q_ref, kn_ref, vn_ref, ck_hbm, cv_hbm, cnt_ref,
                      y_ref, attn_ref, ok_ref, ov_ref, h_ref, kbuf, vbuf, sem,
                      *, steps_per_tile, t_new, first_row, units_per_row):
    u = pl.program_id(0)
    n_units = pl.num_programs(0)
    c = u % steps_per_tile
    base, extra = divmod(wd_ref.shape[0], steps_per_tile)
    uw = kbuf.shape[1]

    def fetch(unit, slot):
        row = first_row + unit // units_per_row
        h0 = pl.multiple_of((unit % units_per_row) * uw, uw)
        return (pltpu.make_async_copy(ck_hbm.at[row, pl.ds(h0, uw), :], kbuf.at[slot], sem.at[0, slot]),
                pltpu.make_async_copy(cv_hbm.at[row, pl.ds(h0, uw), :], vbuf.at[slot], sem.at[1, slot]))

    @pl.when(u == 0)
    def _():
        for unit in range(CACHE_RING - 1):
            for cp in fetch(unit, unit):
                cp.start()

    ahead = u + (CACHE_RING - 1)

    @pl.when(ahead < n_units)
    def _():
        for cp in fetch(ahead, ahead % CACHE_RING):
            cp.start()

    @pl.when(c == 0)
    def _():
        x1 = x_ref[...] + jnp.dot(mix_ref[...], wo_ref[...], preferred_element_type=F32)
        y_ref[...] = x1
        h_ref[...] = _rms_rows(x1, g_ref[...]).astype(BF16)

    slot = u % CACHE_RING
    for cp in fetch(u, slot):
        cp.wait()
    ck, cv = kbuf[slot], vbuf[slot]
    _shift_append(ck, kn_ref[0], ok_ref, t_new)
    _shift_append(cv, vn_ref[0], ov_ref, t_new)
    _sample_attend(q_ref, kn_ref, vn_ref, ck, cv, cnt_ref, attn_ref, t_new)
    for i in range(base):
        y_ref[...] += _swiglu_chunk(h_ref[...], wg_ref, wu_ref, wd_ref, c * base + i)
    if extra:
        first = steps_per_tile - extra

        @pl.when(c >= first)
        def _():
            y_ref[...] += _swiglu_chunk(h_ref[...], wg_ref, wu_ref, wd_ref, steps_per_tile * base + c - first)


def _ffn_with_cache(x2d, mix, wo_b, g, wg3, wu3, wd3, q_s, k_s, v_s, cache_kt, cache_vt, layer, tm):
    m, d = x2d.shape
    n, t_new, _ = q_s.shape
    l_buf = cache_kt.shape[2]
    assert l_buf == WIN_MAX and t_new % SUBLANES == 0
    uw = CACHE_HEADS_PER_UNIT * HEAD_DIM
    per_n = D_ATTN // uw
    n_tiles, n_units = m // tm, n * per_n
    assert n_units % n_tiles == 0
    steps_per_tile = n_units // n_tiles
    assert wd3.shape[0] >= steps_per_tile and n_units >= CACHE_RING
    dist = l_buf + np.arange(t_new)[:, None] - np.arange(l_buf)[None, :]
    cnt = np.tile(_multiplicity(dist), (CACHE_HEADS_PER_UNIT, 1))

    tile = lambda width: pl.BlockSpec((tm, width), lambda u: (u // steps_per_tile, 0))
    new = pl.BlockSpec((1, t_new, uw), lambda u: (u // per_n, 0, u % per_n))
    big = pl.BlockSpec((1, uw, l_buf), lambda u: (u // per_n, u % per_n, 0))
    in_hbm = pl.BlockSpec(memory_space=pl.ANY)
    return pl.pallas_call(
        functools.partial(_ffn_cache_kernel, steps_per_tile=steps_per_tile, t_new=t_new,
                          first_row=layer * n, units_per_row=per_n),
        grid=(n_units,),
        in_specs=[tile(d), tile(mix.shape[1]), _const_spec(wo_b.shape), _const_spec((1, d)),
                  _const_spec(wg3.shape), _const_spec(wu3.shape), _const_spec(wd3.shape),
                  new, new, new, in_hbm, in_hbm, _const_spec(cnt.shape)],
        out_specs=[tile(d), new, big, big],
        out_shape=[jax.ShapeDtypeStruct((m, d), F32),
                   jax.ShapeDtypeStruct((n, t_new, D_ATTN), F32),
                   jax.ShapeDtypeStruct((n, D_ATTN, l_buf), F32),
                   jax.ShapeDtypeStruct((n, D_ATTN, l_buf), F32)],
        scratch_shapes=[pltpu.VMEM((tm, d), BF16),
                        pltpu.VMEM((CACHE_RING, uw, l_buf), F32), pltpu.VMEM((CACHE_RING, uw, l_buf), F32),
                        pltpu.SemaphoreType.DMA((2, CACHE_RING))],
        compiler_params=_params(1),
        name="ffn_cache",
    )(x2d, mix, wo_b, g, wg3, wu3, wd3, q_s, k_s, v_s, cache_kt, cache_vt, jnp.asarray(cnt))


def _ffn_kernel(x_ref, *refs, n_mix):
    mix_refs = refs[0:n_mix]
    wo_ref, g_ref, wg_ref, wu_ref, wd_ref, y_ref = refs[n_mix:]
    x1 = x_ref[...]
    row = 0
    for mr in mix_refs:
        width = mr.shape[1]
        x1 = x1 + jnp.dot(mr[...].astype(BF16), wo_ref[row:row + width, :], preferred_element_type=F32)
        row += width
    h = _rms_rows(x1, g_ref[...]).astype(BF16)
    y_ref[...] = x1
    for j in range(wd_ref.shape[0]):
        y_ref[...] += _swiglu_chunk(h, wg_ref, wu_ref, wd_ref, j)


def _outproj_ffn(x2d, mix_parts, wo_b, g, wg3, wu3, wd3, tm):
    m, d = x2d.shape
    tm = min(tm, m)
    row = lambda i: (i, 0)
    return pl.pallas_call(
        functools.partial(_ffn_kernel, n_mix=len(mix_parts)),
        grid=(m // tm,),
        in_specs=[pl.BlockSpec((tm, d), row)] + [pl.BlockSpec((tm, p.shape[1]), row) for p in mix_parts]
        + [_const_spec(wo_b.shape), _const_spec((1, d)),
           _const_spec(wg3.shape), _const_spec(wu3.shape), _const_spec(wd3.shape)],
        out_specs=pl.BlockSpec((tm, d), row),
        out_shape=jax.ShapeDtypeStruct((m, d), F32),
        compiler_params=_params(1),
        name="outproj_ffn",
    )(x2d, *mix_parts, wo_b, g, wg3, wu3, wd3)


TOKEN_TILE = 512
MIX_TIME_TILE = 512
SAMPLE_CONV_BATCH = 32


def kernel(x_prompt, x_sample, cache_k, cache_v, state_conv, attn_norm_g, w_in, q_norm_g, k_norm_g,
           conv_w, conv_b, conv_ln_g, conv_ln_b, w_out, ffn_norm_g, w_gate, w_up, w_down):
    batch, seq, d_model = x_prompt.shape
    n_dec, t_new, _ = x_sample.shape
    depth = w_in.shape[0]
    d_conv = conv_w.shape[-1]
    l_buf = cache_k.shape[2]
    assert seq == WIN_MAX and l_buf == WIN_MAX and seq % (BLK * PATTERNS[-1][1]) == 0
    assert w_in.shape[-1] == 3 * D_ATTN + 2 * d_conv and d_conv == D_ATTN

    seg = jnp.asarray(np.kron(np.eye(N_HEADS), np.ones((HEAD_DIM, HEAD_DIM))), BF16)
    row = lambda v: v.reshape(1, -1)
    to_t = lambda c: jnp.transpose(c, (0, 1, 3, 4, 2)).reshape(depth * n_dec, D_ATTN, l_buf)
    from_t = lambda c, n: jnp.transpose(c.reshape(n, N_HEADS, HEAD_DIM, -1), (0, 3, 1, 2))
    cache_kt, cache_vt = to_t(cache_k), to_t(cache_v)
    state_t = jnp.transpose(state_conv, (0, 2, 1, 3)).reshape(depth * CONV_LEFT, n_dec, d_conv)

    y_p = x_prompt.reshape(batch * seq, d_model)
    y_s = x_sample.reshape(n_dec * t_new, d_model)
    outs = [[] for _ in range(6)]
    for l in range(depth):
        w_in_b = w_in[l].astype(BF16)
        g_attn, g_ffn = row(attn_norm_g[l]), row(ffn_norm_g[l])
        qg = row(jnp.tile(q_norm_g[l], N_HEADS))
        kg = row(jnp.tile(k_norm_g[l], N_HEADS))
        cb, lg, lb = row(conv_b[l]), row(conv_ln_g[l]), row(conv_ln_b[l])

        q, k, v, u_s = _inproj_sample(y_s, g_attn, w_in_b, qg, kg, seg, TOKEN_TILE)
        c_s, ns = _sample_conv(state_t, u_s, conv_w[l], cb, lg, lb, l, t_new, min(SAMPLE_CONV_BATCH, n_dec))
        qkv, kt, vt, u = _inproj_prompt(y_p, g_attn, w_in_b, qg, kg, seg, batch, seq, TOKEN_TILE)
        attn = {dil: _band_attention(*qkv[dil], w, dil) for w, dil in PATTERNS}
        mix, (wo_b, wg3, wu3, wd_b) = _mix_prompt(u, conv_w[l], cb, lg, lb, attn,
                                                   (w_out[l], w_gate[l], w_up[l], w_down[l]),
                                                   batch, seq, MIX_TIME_TILE)
        wd3 = wd_b.reshape(wd_b.shape[0] // MXU_COLS, MXU_COLS, d_model)

        as3 = lambda t: t.reshape(n_dec, t_new, D_ATTN)
        y_p, attn_s, nk, nv = _ffn_with_cache(y_p, mix, wo_b, g_ffn, wg3, wu3, wd3,
                                              as3(q), as3(k), as3(v), cache_kt, cache_vt, l, TOKEN_TILE)
        y_s = _outproj_ffn(y_s, [attn_s.reshape(n_dec * t_new, D_ATTN), c_s],
                           wo_b, g_ffn, wg3, wu3, wd3, TOKEN_TILE)
        outs[0].append(from_t(kt, batch))
        outs[1].append(from_t(vt, batch))
        outs[2].append(u.reshape(batch, seq, d_conv)[:, seq - CONV_LEFT:])
        outs[3].append(from_t(nk, n_dec))
        outs[4].append(from_t(nv, n_dec))
        outs[5].append(jnp.transpose(ns, (1, 0, 2)))

    stack = lambda xs: xs[0][None] if len(xs) == 1 else jnp.stack(xs)
    return (y_p.reshape(batch, seq, d_model), y_s.reshape(n_dec, t_new, d_model)) + tuple(
        stack(o) for o in outs)
```

```python
import functools

import numpy as np
import jax
import jax.numpy as jnp
from jax import lax
from jax.experimental import pallas as pl
from jax.experimental.pallas import tpu as pltpu

N_HEADS = 8
HEAD_DIM = 64
D_ATTN = N_HEADS * HEAD_DIM
PATTERNS = ((128, 1), (512, 4), (2048, 16))
WIN_MAX = max(w for w, _ in PATTERNS)
CONV_WIDTH = 31
CONV_LEFT = CONV_WIDTH - 1
BLK = 128
RMS_EPS = 1e-6
LN_EPS = 1e-5
ATTN_SCALE = HEAD_DIM ** -0.5
LOG2E = 1.4426950408889634
NEG_BIG = -1e30

LANES = 128
MXU_COLS = 256
HEADS_PER_LANE_TILE = LANES // HEAD_DIM
VMEM_LIMIT_BYTES = 56 * 1024 * 1024

F32 = jnp.float32
BF16 = jnp.bfloat16


def _params(n_axes):
    return pltpu.CompilerParams(dimension_semantics=("arbitrary",) * n_axes,
                                vmem_limit_bytes=VMEM_LIMIT_BYTES)


def _const_spec(shape):
    return pl.BlockSpec(shape, lambda *_: (0,) * len(shape), pipeline_mode=pl.Buffered(1))


def _sigmoid(x):
    return 1.0 / (1.0 + jnp.exp(-x))


def _rms_rows(x, g):
    ms = jnp.mean(x * x, axis=-1, keepdims=True)
    return x * lax.rsqrt(ms + RMS_EPS) * g


def _class_lanes(c):
    return slice(c * D_ATTN, (c + 1) * D_ATTN)


def _tiles_scratch(rows, width):
    return pltpu.VMEM((width // LANES, rows, LANES), F32)


def _put_tiles(scr, x):
    for lt in range(scr.shape[0]):
        scr[lt] = x[:, lt * LANES:(lt + 1) * LANES]


def _get_tiles(scr):
    return jnp.concatenate([scr[lt] for lt in range(scr.shape[0])], axis=1)


def _get_rows(scr, start, n, stride):
    return jnp.concatenate([scr[lt, pl.ds(start, n, stride=stride), :] for lt in range(scr.shape[0])],
                           axis=1)


def _put_rows(scr, start, stride, x):
    for lt in range(scr.shape[0]):
        scr[lt, pl.ds(start, x.shape[0], stride=stride), :] = x[:, lt * LANES:(lt + 1) * LANES]


def _inproj_kernel(x_ref, g_ref, w_ref, qg_ref, kg_ref, seg_ref, *refs, dilations):
    h = _rms_rows(x_ref[...], g_ref[...]).astype(BF16)

    def proj(col):
        return jnp.dot(h, w_ref[:, col * D_ATTN:(col + 1) * D_ATTN], preferred_element_type=F32)

    def head_rms(t, gain):
        ss = jnp.dot((t * t).astype(BF16), seg_ref[...], preferred_element_type=F32)
        return t * lax.rsqrt(ss * (1.0 / HEAD_DIM) + RMS_EPS) * gain

    make = (lambda: head_rms(proj(0), qg_ref[...]) * (ATTN_SCALE * LOG2E),
            lambda: head_rms(proj(1), kg_ref[...]),
            lambda: proj(2))
    if not dilations:
        for ref, f in zip(refs, make):
            ref[...] = f()
        refs[3][...] = proj(3) * _sigmoid(proj(4))
        return

    n_views = 3 * len(dilations)
    nat = refs[0:3]
    views = refs[3:3 + n_views]
    t_refs = (None,) + refs[3 + n_views:5 + n_views]
    u_ref = refs[5 + n_views]
    scratch = refs[6 + n_views:]
    tm = x_ref.shape[0]
    for i, f in enumerate(make):
        t = f()
        if t_refs[i] is not None:
            t_refs[i][0] = t.T
        nat[i][...] = t.astype(BF16)
        _put_tiles(scratch[i], t)
        for j, dil in enumerate(dilations):
            for c in range(dil):
                views[3 * j + i][0, :, _class_lanes(c)] = (
                    _get_rows(scratch[i], c, tm // dil, dil).astype(BF16))
    u_ref[...] = proj(3) * _sigmoid(proj(4))


def _inproj_sample(x2d, g, w_in_b, qg, kg, seg, tm):
    m, d = x2d.shape
    tm = min(tm, m)
    row = lambda i: (i, 0)
    out_f = jax.ShapeDtypeStruct((m, D_ATTN), F32)
    tile = pl.BlockSpec((tm, D_ATTN), row)
    return pl.pallas_call(
        functools.partial(_inproj_kernel, dilations=()),
        grid=(m // tm,),
        in_specs=[pl.BlockSpec((tm, d), row), _const_spec((1, d)), _const_spec(w_in_b.shape),
                  _const_spec((1, D_ATTN)), _const_spec((1, D_ATTN)), _const_spec((D_ATTN, D_ATTN))],
        out_specs=[tile] * 4,
        out_shape=[out_f] * 4,
        compiler_params=_params(1),
        name="inproj_sample",
    )(x2d, g, w_in_b, qg, kg, seg)


def _inproj_prompt(x2d, g, w_in_b, qg, kg, seg, batch, seq, tm):
    m, d = x2d.shape
    n_t = seq // tm
    dilations = tuple(dil for _, dil in PATTERNS if dil > 1)
    flat = lambda b, t: (b * n_t + t, 0)
    nat_spec = pl.BlockSpec((tm, D_ATTN), flat)
    nat_b = jax.ShapeDtypeStruct((m, D_ATTN), BF16)
    view_specs, view_shapes = [], []
    for dil in dilations:
        view_specs += [pl.BlockSpec((1, tm // dil, dil * D_ATTN), lambda b, t: (b, t, 0))] * 3
        view_shapes += [jax.ShapeDtypeStruct((batch, seq // dil, dil * D_ATTN), BF16)] * 3
    t_spec = pl.BlockSpec((1, D_ATTN, tm), lambda b, t: (b, 0, t))
    t_shape = jax.ShapeDtypeStruct((batch, D_ATTN, seq), F32)
    res = pl.pallas_call(
        functools.partial(_inproj_kernel, dilations=dilations),
        grid=(batch, n_t),
        in_specs=[pl.BlockSpec((tm, d), flat), _const_spec((1, d)), _const_spec(w_in_b.shape),
                  _const_spec((1, D_ATTN)), _const_spec((1, D_ATTN)), _const_spec((D_ATTN, D_ATTN))],
        out_specs=[nat_spec] * 3 + view_specs + [t_spec, t_spec, nat_spec],
        out_shape=[nat_b] * 3 + view_shapes + [t_shape, t_shape, jax.ShapeDtypeStruct((m, D_ATTN), F32)],
        scratch_shapes=[_tiles_scratch(tm, D_ATTN)] * 3,
        compiler_params=_params(2),
        name="inproj_prompt",
    )(x2d, g, w_in_b, qg, kg, seg)
    n_views = 3 * len(dilations)
    qkv = {1: tuple(t.reshape(batch, seq, D_ATTN) for t in res[0:3])}
    for j, dil in enumerate(dilations):
        qkv[dil] = tuple(res[3 + 3 * j:6 + 3 * j])
    kt, vt, u = res[3 + n_views:]
    return qkv, kt, vt, u


def _band_attn_kernel(q_ref, k_ref, v_ref, bias0_ref, bias_ref, o_ref, lse_ref, *, n_blocks, n_classes):
    lane = lax.broadcasted_iota(jnp.int32, (BLK, LANES), 1)
    upper = lane >= HEAD_DIM
    n_pairs = N_HEADS // HEADS_PER_LANE_TILE

    def block(cl, r0, k0, n_keys, bias):
        vps, scores = [], []
        for hp in range(n_pairs):
            lanes = slice(cl * D_ATTN + hp * LANES, cl * D_ATTN + (hp + 1) * LANES)
            qp = q_ref[0, pl.ds(r0, BLK), lanes]
            kp = k_ref[0, pl.ds(k0, n_keys), lanes]
            vps.append(v_ref[0, pl.ds(k0, n_keys), lanes])
            q2 = jnp.concatenate([jnp.where(~upper, qp, jnp.zeros_like(qp)),
                                  jnp.where(upper, qp, jnp.zeros_like(qp))], axis=0)
            s2 = lax.dot_general(q2, kp, (((1,), (1,)), ((), ())), preferred_element_type=F32)
            scores += [s2[hh * BLK:(hh + 1) * BLK] + bias for hh in range(HEADS_PER_LANE_TILE)]
        probs, inv_l = [], []
        lse_tile = jnp.zeros((BLK, LANES), F32)
        for h, s in enumerate(scores):
            m = jnp.max(s, axis=-1, keepdims=True)
            p = jnp.exp2(s - m)
            l = jnp.sum(p, axis=-1, keepdims=True)
            probs.append(p.astype(BF16))
            inv_l.append(1.0 / l)
            lse_tile = jnp.where(lane == h, m + jnp.log(l) * LOG2E, lse_tile)
        lse_ref[0, pl.ds(r0, BLK), cl * LANES:(cl + 1) * LANES] = lse_tile
        for hp in range(n_pairs):
            lanes = slice(cl * D_ATTN + hp * LANES, cl * D_ATTN + (hp + 1) * LANES)
            o2 = jnp.dot(jnp.concatenate(probs[2 * hp:2 * hp + 2], axis=0), vps[hp],
                         preferred_element_type=F32)
            o_lo, o_hi = (o2[hh * BLK:(hh + 1) * BLK] * inv_l[2 * hp + hh]
                          for hh in range(HEADS_PER_LANE_TILE))
            o_ref[0, pl.ds(r0, BLK), lanes] = jnp.where(upper, o_hi, o_lo).astype(o_ref.dtype)

    for cl in range(n_classes):
        block(cl, 0, 0, BLK, bias0_ref[...])
        if n_blocks > 1:
            def body(i, carry, cl=cl):
                r0 = pl.multiple_of(i * BLK, BLK)
                block(cl, r0, pl.multiple_of(r0 - BLK, BLK), 2 * BLK, bias_ref[...])
                return carry

            lax.fori_loop(1, n_blocks, body, 0)


def _band_biases(n_sub):
    qi = np.arange(BLK)[:, None]
    ki = np.arange(2 * BLK)[None, :]
    dist = BLK + qi - ki
    band = (dist >= 0) & (dist <= n_sub)
    d0 = qi - np.arange(BLK)[None, :]
    to_bias = lambda ok: np.where(ok, 0.0, NEG_BIG).astype(np.float32)
    return to_bias((d0 >= 0) & (d0 <= n_sub)), to_bias(band)


BAND_CLASSES_PER_STEP = 8


def _band_attention(q, k, v, window, dilation):
    batch, length, _ = q.shape
    cps = min(BAND_CLASSES_PER_STEP, dilation)
    bias0, bias = _band_biases(window // dilation)
    blk = pl.BlockSpec((1, length, cps * D_ATTN), lambda b, c: (b, 0, c))
    return pl.pallas_call(
        functools.partial(_band_attn_kernel, n_blocks=length // BLK, n_classes=cps),
        grid=(batch, dilation // cps),
        in_specs=[blk, blk, blk, _const_spec((BLK, BLK)), _const_spec((BLK, 2 * BLK))],
        out_specs=[blk, pl.BlockSpec((1, length, cps * LANES), lambda b, c: (b, 0, c))],
        out_shape=[jax.ShapeDtypeStruct(q.shape, BF16),
                   jax.ShapeDtypeStruct((batch, length, dilation * LANES), F32)],
        compiler_params=_params(2),
        name=f"band_attn_d{dilation}",
    )(q, k, v, jnp.asarray(bias0), jnp.asarray(bias))


CONV_PAD = 32
SUBLANES = 8


def _mix_kernel(u_ref, up_ref, w_ref, b_ref, lg_ref, lb_ref, expand_ref, *refs, tt, dilations, n_cast):
    n_pat = 1 + len(dilations)
    ins = refs[0:2 * n_pat]
    cast_in = refs[2 * n_pat:2 * n_pat + n_cast]
    mix_ref = refs[2 * n_pat + n_cast]
    cast_out = refs[2 * n_pat + n_cast + 1:2 * n_pat + 2 * n_cast + 1]
    ext_ref = refs[2 * n_pat + 2 * n_cast + 1]
    nat = refs[2 * n_pat + 2 * n_cast + 2:]
    for src, dst in zip(cast_in, cast_out):
        dst[...] = src[...].astype(dst.dtype)
    t = pl.program_id(1)
    rows = CONV_PAD + tt
    prev = up_ref[0]
    ext_ref[0, 0:CONV_PAD, :] = jnp.where(t > 0, prev, jnp.zeros_like(prev))
    ext_ref[0, CONV_PAD:rows, :] = u_ref[0]
    for s in range(1, SUBLANES):
        ext_ref[s, 0:rows - SUBLANES, :] = ext_ref[0, pl.ds(s, rows - SUBLANES), :]

    for j, dil in enumerate(dilations):
        for i, width in enumerate((D_ATTN, LANES)):
            src, dst = ins[2 + 2 * j + i], nat[2 * j + i]
            for c in range(dil):
                _put_rows(dst, c, dil, src[0, :, c * width:(c + 1) * width].astype(F32))
    os_ = [ins[0][...].astype(F32)] + [_get_tiles(nat[2 * j]) for j in range(len(dilations))]
    ls_ = [ins[1][...]] + [_get_tiles(nat[2 * j + 1]) for j in range(len(dilations))]
    mx = functools.reduce(jnp.maximum, ls_)
    es = [jnp.exp2(l - mx) for l in ls_]
    inv = 1.0 / sum(es)
    attn = None
    for e, o in zip(es, os_):
        w = e * inv
        hi = w.astype(BF16)
        lo = (w - hi.astype(F32)).astype(BF16)
        wide = (jnp.dot(hi, expand_ref[...], preferred_element_type=F32)
                + jnp.dot(lo, expand_ref[...], preferred_element_type=F32))
        attn = wide * o if attn is None else attn + wide * o
    mix_ref[:, 0:D_ATTN] = attn.astype(mix_ref.dtype)

    d_conv = u_ref.shape[-1]
    acc = None
    for j in range(CONV_WIDTH):
        off = CONV_PAD - CONV_LEFT + j
        s = off % SUBLANES
        term = ext_ref[s, pl.ds(off - s, tt), :] * w_ref[j:j + 1, :]
        acc = term if acc is None else acc + term
    c = _ln_swish(acc + b_ref[...], lg_ref[...], lb_ref[...])
    mix_ref[:, D_ATTN:D_ATTN + d_conv] = c.astype(mix_ref.dtype)


def _ln_swish(c, g, b):
    mu = jnp.mean(c, axis=-1, keepdims=True)
    cc = c - mu
    var = jnp.mean(cc * cc, axis=-1, keepdims=True)
    y = cc * lax.rsqrt(var + LN_EPS) * g + b
    return y * _sigmoid(y)


def _cast_blocking(rows, n_steps):
    for n_blocks in range(n_steps, 0, -1):
        if n_steps % n_blocks == 0 and rows % (n_blocks * 2 * SUBLANES) == 0:
            return rows // n_blocks, n_steps // n_blocks
    raise ValueError((rows, n_steps))


def _mix_prompt(u, conv_w, conv_b, ln_g, ln_b, attn, weights, batch, seq, tt):
    d_conv = u.shape[-1]
    u3 = u.reshape(batch, seq, d_conv)
    n_t = seq // tt
    per = tt // CONV_PAD
    dilations = tuple(dil for _, dil in PATTERNS if dil > 1)
    cur = pl.BlockSpec((1, tt, d_conv), lambda b, t: (b, t, 0))
    prev = pl.BlockSpec((1, CONV_PAD, d_conv), lambda b, t: (b, jnp.maximum(t * per - 1, 0), 0))
    flat = lambda width: pl.BlockSpec((tt, width), lambda b, t: (b * n_t + t, 0))
    view = lambda dil, width: pl.BlockSpec((1, tt // dil, dil * width), lambda b, t: (b, t, 0))
    ins = [attn[1][0].reshape(batch * seq, D_ATTN), attn[1][1].reshape(batch * seq, LANES)]
    specs = [flat(D_ATTN), flat(LANES)]
    scratch = [pltpu.VMEM((SUBLANES, CONV_PAD + tt, d_conv), F32)]
    for dil in dilations:
        ins += list(attn[dil])
        specs += [view(dil, D_ATTN), view(dil, LANES)]
        scratch += [_tiles_scratch(tt, D_ATTN), _tiles_scratch(tt, LANES)]
    expand = np.zeros((LANES, D_ATTN), np.float32)
    expand[:N_HEADS] = np.kron(np.eye(N_HEADS), np.ones((1, HEAD_DIM)))
    cast_specs = []
    for w in weights:
        rows, hold = _cast_blocking(w.shape[0], batch * n_t)
        cast_specs.append(pl.BlockSpec((rows, w.shape[1]), lambda b, t, hold=hold: ((b * n_t + t) // hold, 0)))
    res = pl.pallas_call(
        functools.partial(_mix_kernel, tt=tt, dilations=dilations, n_cast=len(weights)),
        grid=(batch, n_t),
        in_specs=[cur, prev, _const_spec(conv_w.shape), _const_spec((1, d_conv)),
                  _const_spec((1, d_conv)), _const_spec((1, d_conv)),
                  _const_spec((LANES, D_ATTN))] + specs + cast_specs,
        out_specs=[pl.BlockSpec((tt, D_ATTN + d_conv), lambda b, t: (b * n_t + t, 0))] + cast_specs,
        out_shape=[jax.ShapeDtypeStruct((batch * seq, D_ATTN + d_conv), BF16)]
        + [jax.ShapeDtypeStruct(w.shape, BF16) for w in weights],
        scratch_shapes=scratch,
        compiler_params=_params(2),
        name="mix_prompt",
    )(u3, u3, conv_w, conv_b, ln_g, ln_b, jnp.asarray(expand, BF16), *ins, *weights)
    return res[0], res[1:]


def _sample_conv_kernel(st_ref, u_ref, w_ref, b_ref, lg_ref, lb_ref, c_ref, ns_ref, us_ref, cs_ref,
                        *, nb, t_new):
    _put_tiles(us_ref, u_ref[...])
    new = [_get_rows(us_ref, t, nb, t_new) for t in range(t_new)]

    def ext(tau):
        return st_ref[tau] if tau < CONV_LEFT else new[tau - CONV_LEFT]

    for t in range(t_new):
        acc = None
        for j in range(CONV_WIDTH):
            term = ext(t + j) * w_ref[j:j + 1, :]
            acc = term if acc is None else acc + term
        c = _ln_swish(acc + b_ref[...], lg_ref[...], lb_ref[...])
        _put_rows(cs_ref, t, t_new, c)
    c_ref[...] = _get_tiles(cs_ref)
    for tau in range(CONV_LEFT):
        ns_ref[tau] = ext(tau + t_new)


def _sample_conv(state_t, u_s, conv_w, conv_b, ln_g, ln_b, layer, t_new, nb):
    _, n, d_conv = state_t.shape
    return pl.pallas_call(
        functools.partial(_sample_conv_kernel, nb=nb, t_new=t_new),
        grid=(n // nb,),
        in_specs=[pl.BlockSpec((CONV_LEFT, nb, d_conv), lambda i: (layer, i, 0)),
                  pl.BlockSpec((nb * t_new, d_conv), lambda i: (i, 0)),
                  _const_spec(conv_w.shape), _const_spec((1, d_conv)),
                  _const_spec((1, d_conv)), _const_spec((1, d_conv))],
        out_specs=[pl.BlockSpec((nb * t_new, d_conv), lambda i: (i, 0)),
                   pl.BlockSpec((CONV_LEFT, nb, d_conv), lambda i: (0, i, 0))],
        out_shape=[jax.ShapeDtypeStruct((n * t_new, d_conv), F32),
                   jax.ShapeDtypeStruct((CONV_LEFT, n, d_conv), F32)],
        scratch_shapes=[_tiles_scratch(nb * t_new, d_conv)] * 2,
        compiler_params=_params(1),
        name="sample_conv",
    )(state_t, u_s, conv_w, conv_b, ln_g, ln_b)


def _multiplicity(dist):
    dist = np.asarray(dist)
    c = np.zeros(dist.shape, np.float32)
    for window, dilation in PATTERNS:
        c += ((dist >= 0) & (dist <= window) & (dist % dilation == 0)).astype(np.float32)
    return c


def _shift_append(old, new_rows, out_ref, t_new):
    n_ch, l_buf = old.shape
    n_tiles = l_buf // LANES
    lane = lax.broadcasted_iota(jnp.int32, (n_ch, LANES), 1)
    keep = lane < LANES - t_new
    pad = jnp.concatenate([jnp.zeros((LANES - t_new, n_ch), F32), new_rows], axis=0)
    nxt = pad.T
    for j in reversed(range(n_tiles)):
        cur = pltpu.roll(old[:, j * LANES:(j + 1) * LANES], LANES - t_new, axis=1)
        out_ref[0, :, j * LANES:(j + 1) * LANES] = jnp.where(keep, cur, nxt)
        nxt = cur


CACHE_HEADS_PER_UNIT = 4


def _sample_attend(q_ref, kn_ref, vn_ref, ck, cv, cnt_ref, attn_ref, t_new):
    width = ck.shape[0]
    n_heads = width // HEAD_DIM
    rows = n_heads * t_new
    kn = kn_ref[0]
    vn = vn_ref[0]

    row = lax.broadcasted_iota(jnp.int32, (rows, width), 0)
    lane = lax.broadcasted_iota(jnp.int32, (rows, width), 1)
    own_head = (row // t_new) == (lane // HEAD_DIM)
    q_rep = jnp.concatenate([q_ref[0]] * n_heads, axis=0)
    q_bd = jnp.where(own_head, q_rep, 0.0)

    cnt = cnt_ref[...]
    s_c = jnp.dot(q_bd.astype(BF16), ck.astype(BF16), preferred_element_type=F32)
    s_c = jnp.where(cnt > 0.0, s_c, NEG_BIG)
    m = jnp.max(s_c, axis=-1, keepdims=True)

    t_of_row = lax.broadcasted_iota(jnp.int32, (rows, 1), 0) % t_new
    s_n, c_n = [], []
    for tp in range(t_new):
        d = t_of_row - tp
        c = jnp.zeros((rows, 1), F32)
        for _, dilation in PATTERNS:
            c = c + jnp.where((d >= 0) & (d % dilation == 0), 1.0, 0.0)
        s = jnp.sum(q_bd * kn[tp:tp + 1, :], axis=-1, keepdims=True)
        s = jnp.where(c > 0.0, s, NEG_BIG)
        m = jnp.maximum(m, s)
        s_n.append(s)
        c_n.append(c)

    p_c = cnt * jnp.exp2(s_c - m)
    l = jnp.sum(p_c, axis=-1, keepdims=True)
    acc = lax.dot_general(p_c.astype(BF16), cv.astype(BF16), (((1,), (1,)), ((), ())),
                          preferred_element_type=F32)
    for tp in range(t_new):
        p = c_n[tp] * jnp.exp2(s_n[tp] - m)
        l = l + p
        acc = acc + p * vn[tp:tp + 1, :]
    acc = jnp.where(own_head, acc * (1.0 / l), 0.0)
    out = acc[0:t_new, :]
    for h in range(1, n_heads):
        out = out + acc[h * t_new:(h + 1) * t_new, :]
    attn_ref[0] = out.astype(attn_ref.dtype)


CACHE_RING = 3


def _swiglu_chunk(h, wg_ref, wu_ref, wd_ref, j):
    cols = pl.ds(pl.multiple_of(j * MXU_COLS, MXU_COLS), MXU_COLS)
    gate = jnp.dot(h, wg_ref[:, cols], preferred_element_type=F32)
    up = jnp.dot(h, wu_ref[:, cols], preferred_element_type=F32)
    act = (gate * _sigmoid(gate) * up).astype(BF16)
    return jnp.dot(act, wd_ref[j], preferred_element_type=F32)


def _ffn_cache_kernel(x_ref, mix_ref, wo_ref, g_ref, wg_ref, wu_ref, wd_ref,
                      q_ref, kn_ref, vn_ref, ck_hbm, cv_hbm, cnt_ref,
                      y_ref, attn_ref, ok_ref, ov_ref, h_ref, kbuf, vbuf, sem,
                      *, steps_per_tile, t_new, first_row, units_per_row):
    u = pl.program_id(0)
    n_units = pl.num_programs(0)
    c = u % steps_per_tile
    base, extra = divmod(wd_ref.shape[0], steps_per_tile)
    uw = kbuf.shape[1]

    def fetch(unit, slot):
        row = first_row + unit // units_per_row
        h0 = pl.multiple_of((unit % units_per_row) * uw, uw)
        return (pltpu.make_async_copy(ck_hbm.at[row, pl.ds(h0, uw), :], kbuf.at[slot], sem.at[0, slot]),
                pltpu.make_async_copy(cv_hbm.at[row, pl.ds(h0, uw), :], vbuf.at[slot], sem.at[1, slot]))

    @pl.when(u == 0)
    def _():
        for unit in range(CACHE_RING - 1):
            for cp in fetch(unit, unit):
                cp.start()

    ahead = u + (CACHE_RING - 1)

    @pl.when(ahead < n_units)
    def _():
        for cp in fetch(ahead, ahead % CACHE_RING):
            cp.start()

    @pl.when(c == 0)
    def _():
        x1 = x_ref[...] + jnp.dot(mix_ref[...], wo_ref[...], preferred_element_type=F32)
        y_ref[...] = x1
        h_ref[...] = _rms_rows(x1, g_ref[...]).astype(BF16)

    slot = u % CACHE_RING
    for cp in fetch(u, slot):
        cp.wait()
    ck, cv = kbuf[slot], vbuf[slot]
    _shift_append(ck, kn_ref[0], ok_ref, t_new)
    _shift_append(cv, vn_ref[0], ov_ref, t_new)
    _sample_attend(q_ref, kn_ref, vn_ref, ck, cv, cnt_ref, attn_ref, t_new)
    for i in range(base):
        y_ref[...] += _swiglu_chunk(h_ref[...], wg_ref, wu_ref, wd_ref, c * base + i)
    if extra:
        first = steps_per_tile - extra

        @pl.when(c >= first)
        def _():
            y_ref[...] += _swiglu_chunk(h_ref[...], wg_ref, wu_ref, wd_ref, steps_per_tile * base + c - first)


def _ffn_with_cache(x2d, mix, wo_b, g, wg3, wu3, wd3, q_s, k_s, v_s, cache_kt, cache_vt, layer, tm):
    m, d = x2d.shape
    n, t_new, _ = q_s.shape
    l_buf = cache_kt.shape[2]
    assert l_buf == WIN_MAX and t_new % SUBLANES == 0
    uw = CACHE_HEADS_PER_UNIT * HEAD_DIM
    per_n = D_ATTN // uw
    n_tiles, n_units = m // tm, n * per_n
    assert n_units % n_tiles == 0
    steps_per_tile = n_units // n_tiles
    assert wd3.shape[0] >= steps_per_tile and n_units >= CACHE_RING
    dist = l_buf + np.arange(t_new)[:, None] - np.arange(l_buf)[None, :]
    cnt = np.tile(_multiplicity(dist), (CACHE_HEADS_PER_UNIT, 1))

    tile = lambda width: pl.BlockSpec((tm, width), lambda u: (u // steps_per_tile, 0))
    new = pl.BlockSpec((1, t_new, uw), lambda u: (u // per_n, 0, u % per_n))
    big = pl.BlockSpec((1, uw, l_buf), lambda u: (u // per_n, u % per_n, 0))
    in_hbm = pl.BlockSpec(memory_space=pl.ANY)
    return pl.pallas_call(
        functools.partial(_ffn_cache_kernel, steps_per_tile=steps_per_tile, t_new=t_new,
                          first_row=layer * n, units_per_row=per_n),
        grid=(n_units,),
        in_specs=[tile(d), tile(mix.shape[1]), _const_spec(wo_b.shape), _const_spec((1, d)),
                  _const_spec(wg3.shape), _const_spec(wu3.shape), _const_spec(wd3.shape),
                  new, new, new, in_hbm, in_hbm, _const_spec(cnt.shape)],
        out_specs=[tile(d), new, big, big],
        out_shape=[jax.ShapeDtypeStruct((m, d), F32),
                   jax.ShapeDtypeStruct((n, t_new, D_ATTN), F32),
                   jax.ShapeDtypeStruct((n, D_ATTN, l_buf), F32),
                   jax.ShapeDtypeStruct((n, D_ATTN, l_buf), F32)],
        scratch_shapes=[pltpu.VMEM((tm, d), BF16),
                        pltpu.VMEM((CACHE_RING, uw, l_buf), F32), pltpu.VMEM((CACHE_RING, uw, l_buf), F32),
                        pltpu.SemaphoreType.DMA((2, CACHE_RING))],
        compiler_params=_params(1),
        name="ffn_cache",
    )(x2d, mix, wo_b, g, wg3, wu3, wd3, q_s, k_s, v_s, cache_kt, cache_vt, jnp.asarray(cnt))


def _ffn_kernel(x_ref, *refs, n_mix):
    mix_refs = refs[0:n_mix]
    wo_ref, g_ref, wg_ref, wu_ref, wd_ref, y_ref = refs[n_mix:]
    x1 = x_ref[...]
    row = 0
    for mr in mix_refs:
        width = mr.shape[1]
        x1 = x1 + jnp.dot(mr[...].astype(BF16), wo_ref[row:row + width, :], preferred_element_type=F32)
        row += width
    h = _rms_rows(x1, g_ref[...]).astype(BF16)
    y_ref[...] = x1
    for j in range(wd_ref.shape[0]):
        y_ref[...] += _swiglu_chunk(h, wg_ref, wu_ref, wd_ref, j)


def _outproj_ffn(x2d, mix_parts, wo_b, g, wg3, wu3, wd3, tm):
    m, d = x2d.shape
    tm = min(tm, m)
    row = lambda i: (i, 0)
    return pl.pallas_call(
        functools.partial(_ffn_kernel, n_mix=len(mix_parts)),
        grid=(m // tm,),
        in_specs=[pl.BlockSpec((tm, d), row)] + [pl.BlockSpec((tm, p.shape[1]), row) for p in mix_parts]
        + [_const_spec(wo_b.shape), _const_spec((1, d)),
           _const_spec(wg3.shape), _const_spec(wu3.shape), _const_spec(wd3.shape)],
        out_specs=pl.BlockSpec((tm, d), row),
        out_shape=jax.ShapeDtypeStruct((m, d), F32),
        compiler_params=_params(1),
        name="outproj_ffn",
    )(x2d, *mix_parts, wo_b, g, wg3, wu3, wd3)


TOKEN_TILE = 512
INPROJ_TILE = 1024
MIX_TIME_TILE = 1024
SAMPLE_CONV_BATCH = 32


def kernel(x_prompt, x_sample, cache_k, cache_v, state_conv, attn_norm_g, w_in, q_norm_g, k_norm_g,
           conv_w, conv_b, conv_ln_g, conv_ln_b, w_out, ffn_norm_g, w_gate, w_up, w_down):
    batch, seq, d_model = x_prompt.shape
    n_dec, t_new, _ = x_sample.shape
    depth = w_in.shape[0]
    d_conv = conv_w.shape[-1]
    l_buf = cache_k.shape[2]
    assert seq == WIN_MAX and l_buf == WIN_MAX and seq % (BLK * PATTERNS[-1][1]) == 0
    assert w_in.shape[-1] == 3 * D_ATTN + 2 * d_conv and d_conv == D_ATTN

    seg = jnp.asarray(np.kron(np.eye(N_HEADS), np.ones((HEAD_DIM, HEAD_DIM))), BF16)
    row = lambda v: v.reshape(1, -1)
    to_t = lambda c: jnp.transpose(c, (0, 1, 3, 4, 2)).reshape(depth * n_dec, D_ATTN, l_buf)
    from_t = lambda c, n: jnp.transpose(c.reshape(n, N_HEADS, HEAD_DIM, -1), (0, 3, 1, 2))
    cache_kt, cache_vt = to_t(cache_k), to_t(cache_v)
    state_t = jnp.transpose(state_conv, (0, 2, 1, 3)).reshape(depth * CONV_LEFT, n_dec, d_conv)

    y_p = x_prompt.reshape(batch * seq, d_model)
    y_s = x_sample.reshape(n_dec * t_new, d_model)
    outs = [[] for _ in range(6)]
    for l in range(depth):
        w_in_b = w_in[l].astype(BF16)
        g_attn, g_ffn = row(attn_norm_g[l]), row(ffn_norm_g[l])
        qg = row(jnp.tile(q_norm_g[l], N_HEADS))
        kg = row(jnp.tile(k_norm_g[l], N_HEADS))
        cb, lg, lb = row(conv_b[l]), row(conv_ln_g[l]), row(conv_ln_b[l])

        q, k, v, u_s = _inproj_sample(y_s, g_attn, w_in_b, qg, kg, seg, TOKEN_TILE)
        c_s, ns = _sample_conv(state_t, u_s, conv_w[l], cb, lg, lb, l, t_new, min(SAMPLE_CONV_BATCH, n_dec))
        qkv, kt, vt, u = _inproj_prompt(y_p, g_attn, w_in_b, qg, kg, seg, batch, seq, INPROJ_TILE)
        attn = {dil: _band_attention(*qkv[dil], w, dil) for w, dil in PATTERNS}
        mix, (wo_b, wg3, wu3, wd_b) = _mix_prompt(u, conv_w[l], cb, lg, lb, attn,
                                                   (w_out[l], w_gate[l], w_up[l], w_down[l]),
                                                   batch, seq, MIX_TIME_TILE)
        wd3 = wd_b.reshape(wd_b.shape[0] // MXU_COLS, MXU_COLS, d_model)

        as3 = lambda t: t.reshape(n_dec, t_new, D_ATTN)
        y_p, attn_s, nk, nv = _ffn_with_cache(y_p, mix, wo_b, g_ffn, wg3, wu3, wd3,
                                              as3(q), as3(k), as3(v), cache_kt, cache_vt, l, TOKEN_TILE)
        y_s = _outproj_ffn(y_s, [attn_s.reshape(n_dec * t_new, D_ATTN), c_s],
                           wo_b, g_ffn, wg3, wu3, wd3, TOKEN_TILE)
        outs[0].append(from_t(kt, batch))
        outs[1].append(from_t(vt, batch))
        outs[2].append(u.reshape(batch, seq, d_conv)[:, seq - CONV_LEFT:])
        outs[3].append(from_t(nk, n_dec))
        outs[4].append(from_t(nv, n_dec))
        outs[5].append(jnp.transpose(ns, (1, 0, 2)))

    stack = lambda xs: xs[0][None] if len(xs) == 1 else jnp.stack(xs)
    return (y_p.reshape(batch, seq, d_model), y_s.reshape(n_dec, t_new, d_model)) + tuple(
        stack(o) for o in outs)
```

```python
import functools

import numpy as np
import jax
import jax.numpy as jnp
from jax import lax
from jax.experimental import pallas as pl
from jax.experimental.pallas import tpu as pltpu

N_HEADS = 8
HEAD_DIM = 64
D_ATTN = N_HEADS * HEAD_DIM
PATTERNS = ((128, 1), (512, 4), (2048, 16))
WIN_MAX = max(w for w, _ in PATTERNS)
CONV_WIDTH = 31
CONV_LEFT = CONV_WIDTH - 1
BLK = 128
RMS_EPS = 1e-6
LN_EPS = 1e-5
ATTN_SCALE = HEAD_DIM ** -0.5
LOG2E = 1.4426950408889634
NEG_BIG = -1e30

LANES = 128
MXU_COLS = 256
HEADS_PER_LANE_TILE = LANES // HEAD_DIM
VMEM_LIMIT_BYTES = 56 * 1024 * 1024

F32 = jnp.float32
BF16 = jnp.bfloat16


def _params(n_axes):
    return pltpu.CompilerParams(dimension_semantics=("arbitrary",) * n_axes,
                                vmem_limit_bytes=VMEM_LIMIT_BYTES)


def _const_spec(shape):
    return pl.BlockSpec(shape, lambda *_: (0,) * len(shape), pipeline_mode=pl.Buffered(1))


def _sigmoid(x):
    return 1.0 / (1.0 + jnp.exp(-x))


def _rms_rows(x, g):
    ms = jnp.mean(x * x, axis=-1, keepdims=True)
    return x * lax.rsqrt(ms + RMS_EPS) * g


def _class_lanes(c):
    return slice(c * D_ATTN, (c + 1) * D_ATTN)


def _tiles_scratch(rows, width):
    return pltpu.VMEM((width // LANES, rows, LANES), F32)


def _put_tiles(scr, x):
    for lt in range(scr.shape[0]):
        scr[lt] = x[:, lt * LANES:(lt + 1) * LANES]


def _get_tiles(scr):
    return jnp.concatenate([scr[lt] for lt in range(scr.shape[0])], axis=1)


def _get_rows(scr, start, n, stride):
    return jnp.concatenate([scr[lt, pl.ds(start, n, stride=stride), :] for lt in range(scr.shape[0])],
                           axis=1)


def _put_rows(scr, start, stride, x):
    for lt in range(scr.shape[0]):
        scr[lt, pl.ds(start, x.shape[0], stride=stride), :] = x[:, lt * LANES:(lt + 1) * LANES]


def _inproj_kernel(x_ref, g_ref, w_ref, qg_ref, kg_ref, seg_ref, *refs, dilations):
    h = _rms_rows(x_ref[...], g_ref[...]).astype(BF16)

    def proj(col):
        return jnp.dot(h, w_ref[:, col * D_ATTN:(col + 1) * D_ATTN], preferred_element_type=F32)

    def head_rms(t, gain):
        ss = jnp.dot((t * t).astype(BF16), seg_ref[...], preferred_element_type=F32)
        return t * lax.rsqrt(ss * (1.0 / HEAD_DIM) + RMS_EPS) * gain

    make = (lambda: head_rms(proj(0), qg_ref[...]) * (ATTN_SCALE * LOG2E),
            lambda: head_rms(proj(1), kg_ref[...]),
            lambda: proj(2))
    if not dilations:
        for ref, f in zip(refs, make):
            ref[...] = f()
        refs[3][...] = proj(3) * _sigmoid(proj(4))
        return

    n_views = 3 * len(dilations)
    nat = refs[0:3]
    views = refs[3:3 + n_views]
    t_refs = (None,) + refs[3 + n_views:5 + n_views]
    u_ref = refs[5 + n_views]
    scratch = refs[6 + n_views:]
    tm = x_ref.shape[0]
    for i, f in enumerate(make):
        t = f()
        if t_refs[i] is not None:
            t_refs[i][0] = t.T
        nat[i][...] = t.astype(BF16)
        _put_tiles(scratch[i], t)
        for j, dil in enumerate(dilations):
            for c in range(dil):
                views[3 * j + i][0, :, _class_lanes(c)] = (
                    _get_rows(scratch[i], c, tm // dil, dil).astype(BF16))
    u_ref[...] = proj(3) * _sigmoid(proj(4))


def _inproj_sample(x2d, g, w_in_b, qg, kg, seg, tm):
    m, d = x2d.shape
    tm = min(tm, m)
    row = lambda i: (i, 0)
    out_f = jax.ShapeDtypeStruct((m, D_ATTN), F32)
    tile = pl.BlockSpec((tm, D_ATTN), row)
    return pl.pallas_call(
        functools.partial(_inproj_kernel, dilations=()),
        grid=(m // tm,),
        in_specs=[pl.BlockSpec((tm, d), row), _const_spec((1, d)), _const_spec(w_in_b.shape),
                  _const_spec((1, D_ATTN)), _const_spec((1, D_ATTN)), _const_spec((D_ATTN, D_ATTN))],
        out_specs=[tile] * 4,
        out_shape=[out_f] * 4,
        compiler_params=_params(1),
        name="inproj_sample",
    )(x2d, g, w_in_b, qg, kg, seg)


def _inproj_prompt(x2d, g, w_in_b, qg, kg, seg, batch, seq, tm):
    m, d = x2d.shape
    n_t = seq // tm
    dilations = tuple(dil for _, dil in PATTERNS if dil > 1)
    flat = lambda b, t: (b * n_t + t, 0)
    nat_spec = pl.BlockSpec((tm, D_ATTN), flat)
    nat_b = jax.ShapeDtypeStruct((m, D_ATTN), BF16)
    view_specs, view_shapes = [], []
    for dil in dilations:
        view_specs += [pl.BlockSpec((1, tm // dil, dil * D_ATTN), lambda b, t: (b, t, 0))] * 3
        view_shapes += [jax.ShapeDtypeStruct((batch, seq // dil, dil * D_ATTN), BF16)] * 3
    t_spec = pl.BlockSpec((1, D_ATTN, tm), lambda b, t: (b, 0, t))
    t_shape = jax.ShapeDtypeStruct((batch, D_ATTN, seq), F32)
    res = pl.pallas_call(
        functools.partial(_inproj_kernel, dilations=dilations),
        grid=(batch, n_t),
        in_specs=[pl.BlockSpec((tm, d), flat), _const_spec((1, d)), _const_spec(w_in_b.shape),
                  _const_spec((1, D_ATTN)), _const_spec((1, D_ATTN)), _const_spec((D_ATTN, D_ATTN))],
        out_specs=[nat_spec] * 3 + view_specs + [t_spec, t_spec, nat_spec],
        out_shape=[nat_b] * 3 + view_shapes + [t_shape, t_shape, jax.ShapeDtypeStruct((m, D_ATTN), F32)],
        scratch_shapes=[_tiles_scratch(tm, D_ATTN)] * 3,
        compiler_params=_params(2),
        name="inproj_prompt",
    )(x2d, g, w_in_b, qg, kg, seg)
    n_views = 3 * len(dilations)
    qkv = {1: tuple(t.reshape(batch, seq, D_ATTN) for t in res[0:3])}
    for j, dil in enumerate(dilations):
        qkv[dil] = tuple(res[3 + 3 * j:6 + 3 * j])
    kt, vt, u = res[3 + n_views:]
    return qkv, kt, vt, u


def _band_attn_kernel(q_ref, k_ref, v_ref, bias0_ref, bias_ref, o_ref, lse_ref, *, n_blocks, n_classes):
    lane = lax.broadcasted_iota(jnp.int32, (BLK, LANES), 1)
    upper = lane >= HEAD_DIM
    n_pairs = N_HEADS // HEADS_PER_LANE_TILE

    def block(cl, r0, k0, n_keys, bias):
        vps, scores = [], []
        for hp in range(n_pairs):
            lanes = slice(cl * D_ATTN + hp * LANES, cl * D_ATTN + (hp + 1) * LANES)
            qp = q_ref[0, pl.ds(r0, BLK), lanes]
            kp = k_ref[0, pl.ds(k0, n_keys), lanes]
            vps.append(v_ref[0, pl.ds(k0, n_keys), lanes])
            q2 = jnp.concatenate([jnp.where(~upper, qp, jnp.zeros_like(qp)),
                                  jnp.where(upper, qp, jnp.zeros_like(qp))], axis=0)
            s2 = lax.dot_general(q2, kp, (((1,), (1,)), ((), ())), preferred_element_type=F32)
            scores += [s2[hh * BLK:(hh + 1) * BLK] + bias for hh in range(HEADS_PER_LANE_TILE)]
        probs, inv_l = [], []
        lse_tile = jnp.zeros((BLK, LANES), F32)
        for h, s in enumerate(scores):
            m = jnp.max(s, axis=-1, keepdims=True)
            p = jnp.exp2(s - m)
            l = jnp.sum(p, axis=-1, keepdims=True)
            probs.append(p.astype(BF16))
            inv_l.append(1.0 / l)
            lse_tile = jnp.where(lane == h, m + jnp.log(l) * LOG2E, lse_tile)
        lse_ref[0, pl.ds(r0, BLK), cl * LANES:(cl + 1) * LANES] = lse_tile
        for hp in range(n_pairs):
            lanes = slice(cl * D_ATTN + hp * LANES, cl * D_ATTN + (hp + 1) * LANES)
            o2 = jnp.dot(jnp.concatenate(probs[2 * hp:2 * hp + 2], axis=0), vps[hp],
                         preferred_element_type=F32)
            o_lo, o_hi = (o2[hh * BLK:(hh + 1) * BLK] * inv_l[2 * hp + hh]
                          for hh in range(HEADS_PER_LANE_TILE))
            o_ref[0, pl.ds(r0, BLK), lanes] = jnp.where(upper, o_hi, o_lo).astype(o_ref.dtype)

    for cl in range(n_classes):
        block(cl, 0, 0, BLK, bias0_ref[...])
        if n_blocks > 1:
            def body(i, carry, cl=cl):
                r0 = pl.multiple_of(i * BLK, BLK)
                block(cl, r0, pl.multiple_of(r0 - BLK, BLK), 2 * BLK, bias_ref[...])
                return carry

            lax.fori_loop(1, n_blocks, body, 0)


def _band_biases(n_sub):
    qi = np.arange(BLK)[:, None]
    ki = np.arange(2 * BLK)[None, :]
    dist = BLK + qi - ki
    band = (dist >= 0) & (dist <= n_sub)
    d0 = qi - np.arange(BLK)[None, :]
    to_bias = lambda ok: np.where(ok, 0.0, NEG_BIG).astype(np.float32)
    return to_bias((d0 >= 0) & (d0 <= n_sub)), to_bias(band)


BAND_CLASSES_PER_STEP = 8


def _band_attention(q, k, v, window, dilation):
    batch, length, _ = q.shape
    cps = min(BAND_CLASSES_PER_STEP, dilation)
    bias0, bias = _band_biases(window // dilation)
    blk = pl.BlockSpec((1, length, cps * D_ATTN), lambda b, c: (b, 0, c))
    return pl.pallas_call(
        functools.partial(_band_attn_kernel, n_blocks=length // BLK, n_classes=cps),
        grid=(batch, dilation // cps),
        in_specs=[blk, blk, blk, _const_spec((BLK, BLK)), _const_spec((BLK, 2 * BLK))],
        out_specs=[blk, pl.BlockSpec((1, length, cps * LANES), lambda b, c: (b, 0, c))],
        out_shape=[jax.ShapeDtypeStruct(q.shape, BF16),
                   jax.ShapeDtypeStruct((batch, length, dilation * LANES), F32)],
        compiler_params=_params(2),
        name=f"band_attn_d{dilation}",
    )(q, k, v, jnp.asarray(bias0), jnp.asarray(bias))


CONV_PAD = 32
SUBLANES = 8


def _mix_kernel(u_ref, up_ref, w_ref, b_ref, lg_ref, lb_ref, expand_ref, *refs, tt, dilations, n_cast):
    n_pat = 1 + len(dilations)
    ins = refs[0:2 * n_pat]
    cast_in = refs[2 * n_pat:2 * n_pat + n_cast]
    mix_ref = refs[2 * n_pat + n_cast]
    cast_out = refs[2 * n_pat + n_cast + 1:2 * n_pat + 2 * n_cast + 1]
    ext_ref = refs[2 * n_pat + 2 * n_cast + 1]
    nat = refs[2 * n_pat + 2 * n_cast + 2:]
    for src, dst in zip(cast_in, cast_out):
        dst[...] = src[...].astype(dst.dtype)
    t = pl.program_id(1)
    rows = CONV_PAD + tt
    prev = up_ref[0]
    ext_ref[0, 0:CONV_PAD, :] = jnp.where(t > 0, prev, jnp.zeros_like(prev))
    ext_ref[0, CONV_PAD:rows, :] = u_ref[0]
    for s in range(1, SUBLANES):
        ext_ref[s, 0:rows - SUBLANES, :] = ext_ref[0, pl.ds(s, rows - SUBLANES), :]

    for j, dil in enumerate(dilations):
        for i, width in enumerate((D_ATTN, LANES)):
            src, dst = ins[2 + 2 * j + i], nat[2 * j + i]
            for c in range(dil):
                _put_rows(dst, c, dil, src[0, :, c * width:(c + 1) * width].astype(F32))
    os_ = [ins[0][...].astype(F32)] + [_get_tiles(nat[2 * j]) for j in range(len(dilations))]
    ls_ = [ins[1][...]] + [_get_tiles(nat[2 * j + 1]) for j in range(len(dilations))]
    mx = functools.reduce(jnp.maximum, ls_)
    es = [jnp.exp2(l - mx) for l in ls_]
    inv = 1.0 / sum(es)
    attn = None
    for e, o in zip(es, os_):
        w = e * inv
        hi = w.astype(BF16)
        lo = (w - hi.astype(F32)).astype(BF16)
        wide = (jnp.dot(hi, expand_ref[...], preferred_element_type=F32)
                + jnp.dot(lo, expand_ref[...], preferred_element_type=F32))
        attn = wide * o if attn is None else attn + wide * o
    mix_ref[:, 0:D_ATTN] = attn.astype(mix_ref.dtype)

    d_conv = u_ref.shape[-1]
    acc = None
    for j in range(CONV_WIDTH):
        off = CONV_PAD - CONV_LEFT + j
        s = off % SUBLANES
        term = ext_ref[s, pl.ds(off - s, tt), :] * w_ref[j:j + 1, :]
        acc = term if acc is None else acc + term
    c = _ln_swish(acc + b_ref[...], lg_ref[...], lb_ref[...])
    mix_ref[:, D_ATTN:D_ATTN + d_conv] = c.astype(mix_ref.dtype)


def _ln_swish(c, g, b):
    mu = jnp.mean(c, axis=-1, keepdims=True)
    cc = c - mu
    var = jnp.mean(cc * cc, axis=-1, keepdims=True)
    y = cc * lax.rsqrt(var + LN_EPS) * g + b
    return y * _sigmoid(y)


def _cast_blocking(rows, n_steps):
    for n_blocks in range(n_steps, 0, -1):
        if n_steps % n_blocks == 0 and rows % (n_blocks * 2 * SUBLANES) == 0:
            return rows // n_blocks, n_steps // n_blocks
    raise ValueError((rows, n_steps))


def _mix_prompt(u, conv_w, conv_b, ln_g, ln_b, attn, weights, batch, seq, tt):
    d_conv = u.shape[-1]
    u3 = u.reshape(batch, seq, d_conv)
    n_t = seq // tt
    per = tt // CONV_PAD
    dilations = tuple(dil for _, dil in PATTERNS if dil > 1)
    cur = pl.BlockSpec((1, tt, d_conv), lambda b, t: (b, t, 0))
    prev = pl.BlockSpec((1, CONV_PAD, d_conv), lambda b, t: (b, jnp.maximum(t * per - 1, 0), 0))
    flat = lambda width: pl.BlockSpec((tt, width), lambda b, t: (b * n_t + t, 0))
    view = lambda dil, width: pl.BlockSpec((1, tt // dil, dil * width), lambda b, t: (b, t, 0))
    ins = [attn[1][0].reshape(batch * seq, D_ATTN), attn[1][1].reshape(batch * seq, LANES)]
    specs = [flat(D_ATTN), flat(LANES)]
    scratch = [pltpu.VMEM((SUBLANES, CONV_PAD + tt, d_conv), F32)]
    for dil in dilations:
        ins += list(attn[dil])
        specs += [view(dil, D_ATTN), view(dil, LANES)]
        scratch += [_tiles_scratch(tt, D_ATTN), _tiles_scratch(tt, LANES)]
    expand = np.zeros((LANES, D_ATTN), np.float32)
    expand[:N_HEADS] = np.kron(np.eye(N_HEADS), np.ones((1, HEAD_DIM)))
    cast_specs = []
    for w in weights:
        rows, hold = _cast_blocking(w.shape[0], batch * n_t)
        cast_specs.append(pl.BlockSpec((rows, w.shape[1]), lambda b, t, hold=hold: ((b * n_t + t) // hold, 0)))
    res = pl.pallas_call(
        functools.partial(_mix_kernel, tt=tt, dilations=dilations, n_cast=len(weights)),
        grid=(batch, n_t),
        in_specs=[cur, prev, _const_spec(conv_w.shape), _const_spec((1, d_conv)),
                  _const_spec((1, d_conv)), _const_spec((1, d_conv)),
                  _const_spec((LANES, D_ATTN))] + specs + cast_specs,
        out_specs=[pl.BlockSpec((tt, D_ATTN + d_conv), lambda b, t: (b * n_t + t, 0))] + cast_specs,
        out_shape=[jax.ShapeDtypeStruct((batch * seq, D_ATTN + d_conv), BF16)]
        + [jax.ShapeDtypeStruct(w.shape, BF16) for w in weights],
        scratch_shapes=scratch,
        compiler_params=_params(2),
        name="mix_prompt",
    )(u3, u3, conv_w, conv_b, ln_g, ln_b, jnp.asarray(expand, BF16), *ins, *weights)
    return res[0], res[1:]


def _sample_conv_kernel(st_ref, u_ref, w_ref, b_ref, lg_ref, lb_ref, c_ref, ns_ref, us_ref, cs_ref,
                        *, nb, t_new):
    _put_tiles(us_ref, u_ref[...])
    new = [_get_rows(us_ref, t, nb, t_new) for t in range(t_new)]

    def ext(tau):
        return st_ref[tau] if tau < CONV_LEFT else new[tau - CONV_LEFT]

    for t in range(t_new):
        acc = None
        for j in range(CONV_WIDTH):
            term = ext(t + j) * w_ref[j:j + 1, :]
            acc = term if acc is None else acc + term
        c = _ln_swish(acc + b_ref[...], lg_ref[...], lb_ref[...])
        _put_rows(cs_ref, t, t_new, c)
    c_ref[...] = _get_tiles(cs_ref)
    for tau in range(CONV_LEFT):
        ns_ref[tau] = ext(tau + t_new)


def _sample_conv(state_t, u_s, conv_w, conv_b, ln_g, ln_b, layer, t_new, nb):
    _, n, d_conv = state_t.shape
    return pl.pallas_call(
        functools.partial(_sample_conv_kernel, nb=nb, t_new=t_new),
        grid=(n // nb,),
        in_specs=[pl.BlockSpec((CONV_LEFT, nb, d_conv), lambda i: (layer, i, 0)),
                  pl.BlockSpec((nb * t_new, d_conv), lambda i: (i, 0)),
                  _const_spec(conv_w.shape), _const_spec((1, d_conv)),
                  _const_spec((1, d_conv)), _const_spec((1, d_conv))],
        out_specs=[pl.BlockSpec((nb * t_new, d_conv), lambda i: (i, 0)),
                   pl.BlockSpec((CONV_LEFT, nb, d_conv), lambda i: (0, i, 0))],
        out_shape=[jax.ShapeDtypeStruct((n * t_new, d_conv), F32),
                   jax.ShapeDtypeStruct((CONV_LEFT, n, d_conv), F32)],
        scratch_shapes=[_tiles_scratch(nb * t_new, d_conv)] * 2,
        compiler_params=_params(1),
        name="sample_conv",
    )(state_t, u_s, conv_w, conv_b, ln_g, ln_b)


def _multiplicity(dist):
    dist = np.asarray(dist)
    c = np.zeros(dist.shape, np.float32)
    for window, dilation in PATTERNS:
        c += ((dist >= 0) & (dist <= window) & (dist % dilation == 0)).astype(np.float32)
    return c


def _shift_append(old, new_rows, out_ref, t_new):
    n_ch, l_buf = old.shape
    n_tiles = l_buf // LANES
    lane = lax.broadcasted_iota(jnp.int32, (n_ch, LANES), 1)
    keep = lane < LANES - t_new
    pad = jnp.concatenate([jnp.zeros((LANES - t_new, n_ch), F32), new_rows], axis=0)
    nxt = pad.T
    for j in reversed(range(n_tiles)):
        cur = pltpu.roll(old[:, j * LANES:(j + 1) * LANES], LANES - t_new, axis=1)
        out_ref[0, :, j * LANES:(j + 1) * LANES] = jnp.where(keep, cur, nxt)
        nxt = cur


CACHE_HEADS_PER_UNIT = 4


def _sample_attend(q_ref, kn_ref, vn_ref, ck, cv, cnt_ref, attn_ref, t_new):
    width = ck.shape[0]
    n_heads = width // HEAD_DIM
    rows = n_heads * t_new
    kn = kn_ref[0]
    vn = vn_ref[0]

    row = lax.broadcasted_iota(jnp.int32, (rows, width), 0)
    lane = lax.broadcasted_iota(jnp.int32, (rows, width), 1)
    own_head = (row // t_new) == (lane // HEAD_DIM)
    q_rep = jnp.concatenate([q_ref[0]] * n_heads, axis=0)
    q_bd = jnp.where(own_head, q_rep, 0.0)

    cnt = cnt_ref[...]
    s_c = jnp.dot(q_bd.astype(BF16), ck.astype(BF16), preferred_element_type=F32)
    s_c = jnp.where(cnt > 0.0, s_c, NEG_BIG)
    m = jnp.max(s_c, axis=-1, keepdims=True)

    t_of_row = lax.broadcasted_iota(jnp.int32, (rows, 1), 0) % t_new
    s_n, c_n = [], []
    for tp in range(t_new):
        d = t_of_row - tp
        c = jnp.zeros((rows, 1), F32)
        for _, dilation in PATTERNS:
            c = c + jnp.where((d >= 0) & (d % dilation == 0), 1.0, 0.0)
        s = jnp.sum(q_bd * kn[tp:tp + 1, :], axis=-1, keepdims=True)
        s = jnp.where(c > 0.0, s, NEG_BIG)
        m = jnp.maximum(m, s)
        s_n.append(s)
        c_n.append(c)

    p_c = cnt * jnp.exp2(s_c - m)
    l = jnp.sum(p_c, axis=-1, keepdims=True)
    acc = lax.dot_general(p_c.astype(BF16), cv.astype(BF16), (((1,), (1,)), ((), ())),
                          preferred_element_type=F32)
    for tp in range(t_new):
        p = c_n[tp] * jnp.exp2(s_n[tp] - m)
        l = l + p
        acc = acc + p * vn[tp:tp + 1, :]
    acc = jnp.where(own_head, acc * (1.0 / l), 0.0)
    out = acc[0:t_new, :]
    for h in range(1, n_heads):
        out = out + acc[h * t_new:(h + 1) * t_new, :]
    attn_ref[0] = out.astype(attn_ref.dtype)


CACHE_RING = 3


def _swiglu_chunk(h, wg_ref, wu_ref, wd_ref, j):
    cols = pl.ds(pl.multiple_of(j * MXU_COLS, MXU_COLS), MXU_COLS)
    gate = jnp.dot(h, wg_ref[:, cols], preferred_element_type=F32)
    up = jnp.dot(h, wu_ref[:, cols], preferred_element_type=F32)
    act = (gate * _sigmoid(gate) * up).astype(BF16)
    return jnp.dot(act, wd_ref[j], preferred_element_type=F32)


def _ffn_cache_kernel(x_ref, mix_ref, wo_ref, g_ref, wg_ref, wu_ref, wd_ref,
                      q_ref, kn_ref, vn_ref, ck_hbm, cv_hbm, cnt_ref,
                      y_ref, attn_ref, ok_ref, ov_ref, h_ref, kbuf, vbuf, sem,
                      *, steps_per_tile, t_new, first_row, units_per_row):
    u = pl.program_id(0)
    n_units = pl.num_programs(0)
    c = u % steps_per_tile
    base, extra = divmod(wd_ref.shape[0], steps_per_tile)
    uw = kbuf.shape[1]

    def fetch(unit, slot):
        row = first_row + unit // units_per_row
        h0 = pl.multiple_of((unit % units_per_row) * uw, uw)
        return (pltpu.make_async_copy(ck_hbm.at[row, pl.ds(h0, uw), :], kbuf.at[slot], sem.at[0, slot]),
                pltpu.make_async_copy(cv_hbm.at[row, pl.ds(h0, uw), :], vbuf.at[slot], sem.at[1, slot]))

    @pl.when(u == 0)
    def _():
        for unit in range(CACHE_RING - 1):
            for cp in fetch(unit, unit):
                cp.start()

    ahead = u + (CACHE_RING - 1)

    @pl.when(ahead < n_units)
    def _():
        for cp in fetch(ahead, ahead % CACHE_RING):
            cp.start()

    @pl.when(c == 0)
    def _():
        x1 = x_ref[...] + jnp.dot(mix_ref[...], wo_ref[...], preferred_element_type=F32)
        y_ref[...] = x1
        h_ref[...] = _rms_rows(x1, g_ref[...]).astype(BF16)

    slot = u % CACHE_RING
    for cp in fetch(u, slot):
        cp.wait()
    ck, cv = kbuf[slot], vbuf[slot]
    _shift_append(ck, kn_ref[0], ok_ref, t_new)
    _shift_append(cv, vn_ref[0], ov_ref, t_new)
    _sample_attend(q_ref, kn_ref, vn_ref, ck, cv, cnt_ref, attn_ref, t_new)
    for i in range(base):
        y_ref[...] += _swiglu_chunk(h_ref[...], wg_ref, wu_ref, wd_ref, c * base + i)
    if extra:
        first = steps_per_tile - extra

        @pl.when(c >= first)
        def _():
            y_ref[...] += _swiglu_chunk(h_ref[...], wg_ref, wu_ref, wd_ref, steps_per_tile * base + c - first)


def _ffn_with_cache(x2d, mix, wo_b, g, wg3, wu3, wd3, q_s, k_s, v_s, cache_kt, cache_vt, layer, tm):
    m, d = x2d.shape
    n, t_new, _ = q_s.shape
    l_buf = cache_kt.shape[2]
    assert l_buf == WIN_MAX and t_new % SUBLANES == 0
    uw = CACHE_HEADS_PER_UNIT * HEAD_DIM
    per_n = D_ATTN // uw
    n_tiles, n_units = m // tm, n * per_n
    assert n_units % n_tiles == 0
    steps_per_tile = n_units // n_tiles
    assert wd3.shape[0] >= steps_per_tile and n_units >= CACHE_RING
    dist = l_buf + np.arange(t_new)[:, None] - np.arange(l_buf)[None, :]
    cnt = np.tile(_multiplicity(dist), (CACHE_HEADS_PER_UNIT, 1))

    tile = lambda width: pl.BlockSpec((tm, width), lambda u: (u // steps_per_tile, 0))
    new = pl.BlockSpec((1, t_new, uw), lambda u: (u // per_n, 0, u % per_n))
    big = pl.BlockSpec((1, uw, l_buf), lambda u: (u // per_n, u % per_n, 0))
    in_hbm = pl.BlockSpec(memory_space=pl.ANY)
    return pl.pallas_call(
        functools.partial(_ffn_cache_kernel, steps_per_tile=steps_per_tile, t_new=t_new,
                          first_row=layer * n, units_per_row=per_n),
        grid=(n_units,),
        in_specs=[tile(d), tile(mix.shape[1]), _const_spec(wo_b.shape), _const_spec((1, d)),
                  _const_spec(wg3.shape), _const_spec(wu3.shape), _const_spec(wd3.shape),
                  new, new, new, in_hbm, in_hbm, _const_spec(cnt.shape)],
        out_specs=[tile(d), new, big, big],
        out_shape=[jax.ShapeDtypeStruct((m, d), F32),
                   jax.ShapeDtypeStruct((n, t_new, D_ATTN), F32),
                   jax.ShapeDtypeStruct((n, D_ATTN, l_buf), F32),
                   jax.ShapeDtypeStruct((n, D_ATTN, l_buf), F32)],
        scratch_shapes=[pltpu.VMEM((tm, d), BF16),
                        pltpu.VMEM((CACHE_RING, uw, l_buf), F32), pltpu.VMEM((CACHE_RING, uw, l_buf), F32),
                        pltpu.SemaphoreType.DMA((2, CACHE_RING))],
        compiler_params=_params(1),
        name="ffn_cache",
    )(x2d, mix, wo_b, g, wg3, wu3, wd3, q_s, k_s, v_s, cache_kt, cache_vt, jnp.asarray(cnt))


def _ffn_kernel(x_ref, *refs, n_mix):
    mix_refs = refs[0:n_mix]
    wo_ref, g_ref, wg_ref, wu_ref, wd_ref, y_ref = refs[n_mix:]
    x1 = x_ref[...]
    row = 0
    for mr in mix_refs:
        width = mr.shape[1]
        x1 = x1 + jnp.dot(mr[...].astype(BF16), wo_ref[row:row + width, :], preferred_element_type=F32)
        row += width
    h = _rms_rows(x1, g_ref[...]).astype(BF16)
    y_ref[...] = x1
    for j in range(wd_ref.shape[0]):
        y_ref[...] += _swiglu_chunk(h, wg_ref, wu_ref, wd_ref, j)


def _outproj_ffn(x2d, mix_parts, wo_b, g, wg3, wu3, wd3, tm):
    m, d = x2d.shape
    tm = min(tm, m)
    row = lambda i: (i, 0)
    return pl.pallas_call(
        functools.partial(_ffn_kernel, n_mix=len(mix_parts)),
        grid=(m // tm,),
        in_specs=[pl.BlockSpec((tm, d), row)] + [pl.BlockSpec((tm, p.shape[1]), row) for p in mix_parts]
        + [_const_spec(wo_b.shape), _const_spec((1, d)),
           _const_spec(wg3.shape), _const_spec(wu3.shape), _const_spec(wd3.shape)],
        out_specs=pl.BlockSpec((tm, d), row),
        out_shape=jax.ShapeDtypeStruct((m, d), F32),
        compiler_params=_params(1),
        name="outproj_ffn",
    )(x2d, *mix_parts, wo_b, g, wg3, wu3, wd3)


TOKEN_TILE = 512
INPROJ_TILE = 1024
MIX_TIME_TILE = 1024
SAMPLE_CONV_BATCH = 64
SAMPLE_TILE = 1024


def kernel(x_prompt, x_sample, cache_k, cache_v, state_conv, attn_norm_g, w_in, q_norm_g, k_norm_g,
           conv_w, conv_b, conv_ln_g, conv_ln_b, w_out, ffn_norm_g, w_gate, w_up, w_down):
    batch, seq, d_model = x_prompt.shape
    n_dec, t_new, _ = x_sample.shape
    depth = w_in.shape[0]
    d_conv = conv_w.shape[-1]
    l_buf = cache_k.shape[2]
    assert seq == WIN_MAX and l_buf == WIN_MAX and seq % (BLK * PATTERNS[-1][1]) == 0
    assert w_in.shape[-1] == 3 * D_ATTN + 2 * d_conv and d_conv == D_ATTN

    seg = jnp.asarray(np.kron(np.eye(N_HEADS), np.ones((HEAD_DIM, HEAD_DIM))), BF16)
    row = lambda v: v.reshape(1, -1)
    to_t = lambda c: jnp.transpose(c, (0, 1, 3, 4, 2)).reshape(depth * n_dec, D_ATTN, l_buf)
    from_t = lambda c, n: jnp.transpose(c.reshape(n, N_HEADS, HEAD_DIM, -1), (0, 3, 1, 2))
    cache_kt, cache_vt = to_t(cache_k), to_t(cache_v)
    state_t = jnp.transpose(state_conv, (0, 2, 1, 3)).reshape(depth * CONV_LEFT, n_dec, d_conv)

    y_p = x_prompt.reshape(batch * seq, d_model)
    y_s = x_sample.reshape(n_dec * t_new, d_model)
    outs = [[] for _ in range(6)]
    for l in range(depth):
        w_in_b = w_in[l].astype(BF16)
        g_attn, g_ffn = row(attn_norm_g[l]), row(ffn_norm_g[l])
        qg = row(jnp.tile(q_norm_g[l], N_HEADS))
        kg = row(jnp.tile(k_norm_g[l], N_HEADS))
        cb, lg, lb = row(conv_b[l]), row(conv_ln_g[l]), row(conv_ln_b[l])

        q, k, v, u_s = _inproj_sample(y_s, g_attn, w_in_b, qg, kg, seg, SAMPLE_TILE)
        c_s, ns = _sample_conv(state_t, u_s, conv_w[l], cb, lg, lb, l, t_new, min(SAMPLE_CONV_BATCH, n_dec))
        qkv, kt, vt, u = _inproj_prompt(y_p, g_attn, w_in_b, qg, kg, seg, batch, seq, INPROJ_TILE)
        attn = {dil: _band_attention(*qkv[dil], w, dil) for w, dil in PATTERNS}
        mix, (wo_b, wg3, wu3, wd_b) = _mix_prompt(u, conv_w[l], cb, lg, lb, attn,
                                                   (w_out[l], w_gate[l], w_up[l], w_down[l]),
                                                   batch, seq, MIX_TIME_TILE)
        wd3 = wd_b.reshape(wd_b.shape[0] // MXU_COLS, MXU_COLS, d_model)

        as3 = lambda t: t.reshape(n_dec, t_new, D_ATTN)
        y_p, attn_s, nk, nv = _ffn_with_cache(y_p, mix, wo_b, g_ffn, wg3, wu3, wd3,
                                              as3(q), as3(k), as3(v), cache_kt, cache_vt, l, TOKEN_TILE)
        y_s = _outproj_ffn(y_s, [attn_s.reshape(n_dec * t_new, D_ATTN), c_s],
                           wo_b, g_ffn, wg3, wu3, wd3, SAMPLE_TILE)
        outs[0].append(from_t(kt, batch))
        outs[1].append(from_t(vt, batch))
        outs[2].append(u.reshape(batch, seq, d_conv)[:, seq - CONV_LEFT:])
        outs[3].append(from_t(nk, n_dec))
        outs[4].append(from_t(nv, n_dec))
        outs[5].append(jnp.transpose(ns, (1, 0, 2)))

    stack = lambda xs: xs[0][None] if len(xs) == 1 else jnp.stack(xs)
    return (y_p.reshape(batch, seq, d_model), y_s.reshape(n_dec, t_new, d_model)) + tuple(
        stack(o) for o in outs)
```

```python
import functools

import numpy as np
import jax
import jax.numpy as jnp
from jax import lax
from jax.experimental import pallas as pl
from jax.experimental.pallas import tpu as pltpu

N_HEADS = 8
HEAD_DIM = 64
D_ATTN = N_HEADS * HEAD_DIM
PATTERNS = ((128, 1), (512, 4), (2048, 16))
WIN_MAX = max(w for w, _ in PATTERNS)
CONV_WIDTH = 31
CONV_LEFT = CONV_WIDTH - 1
BLK = 128
RMS_EPS = 1e-6
LN_EPS = 1e-5
ATTN_SCALE = HEAD_DIM ** -0.5
LOG2E = 1.4426950408889634
NEG_BIG = -1e30

LANES = 128
MXU_COLS = 256
HEADS_PER_LANE_TILE = LANES // HEAD_DIM
VMEM_LIMIT_BYTES = 56 * 1024 * 1024

F32 = jnp.float32
BF16 = jnp.bfloat16


def _params(n_axes):
    return pltpu.CompilerParams(dimension_semantics=("arbitrary",) * n_axes,
                                vmem_limit_bytes=VMEM_LIMIT_BYTES)


def _const_spec(shape):
    return pl.BlockSpec(shape, lambda *_: (0,) * len(shape), pipeline_mode=pl.Buffered(1))


def _sigmoid(x):
    return 1.0 / (1.0 + jnp.exp(-x))


def _rms_rows(x, g):
    ms = jnp.mean(x * x, axis=-1, keepdims=True)
    return x * lax.rsqrt(ms + RMS_EPS) * g


def _class_lanes(c):
    return slice(c * D_ATTN, (c + 1) * D_ATTN)


def _tiles_scratch(rows, width):
    return pltpu.VMEM((width // LANES, rows, LANES), F32)


def _put_tiles(scr, x):
    for lt in range(scr.shape[0]):
        scr[lt] = x[:, lt * LANES:(lt + 1) * LANES]


def _get_tiles(scr):
    return jnp.concatenate([scr[lt] for lt in range(scr.shape[0])], axis=1)


def _get_rows(scr, start, n, stride):
    return jnp.concatenate([scr[lt, pl.ds(start, n, stride=stride), :] for lt in range(scr.shape[0])],
                           axis=1)


def _put_rows(scr, start, stride, x):
    for lt in range(scr.shape[0]):
        scr[lt, pl.ds(start, x.shape[0], stride=stride), :] = x[:, lt * LANES:(lt + 1) * LANES]


def _inproj_kernel(x_ref, g_ref, w_ref, qg_ref, kg_ref, seg_ref, *refs, dilations):
    h = _rms_rows(x_ref[...], g_ref[...]).astype(BF16)

    def proj(col):
        return jnp.dot(h, w_ref[:, col * D_ATTN:(col + 1) * D_ATTN], preferred_element_type=F32)

    def head_rms(t, gain):
        ss = jnp.dot((t * t).astype(BF16), seg_ref[...], preferred_element_type=F32)
        return t * lax.rsqrt(ss * (1.0 / HEAD_DIM) + RMS_EPS) * gain

    make = (lambda: head_rms(proj(0), qg_ref[...]) * (ATTN_SCALE * LOG2E),
            lambda: head_rms(proj(1), kg_ref[...]),
            lambda: proj(2))
    if not dilations:
        for ref, f in zip(refs, make):
            ref[...] = f()
        refs[3][...] = proj(3) * _sigmoid(proj(4))
        return

    n_views = 3 * len(dilations)
    nat = refs[0:3]
    views = refs[3:3 + n_views]
    t_refs = (None,) + refs[3 + n_views:5 + n_views]
    u_ref = refs[5 + n_views]
    scratch = refs[6 + n_views:]
    tm = x_ref.shape[0]
    for i, f in enumerate(make):
        t = f()
        if t_refs[i] is not None:
            t_refs[i][0] = t.T
        nat[i][...] = t.astype(BF16)
        _put_tiles(scratch[i], t)
        for j, dil in enumerate(dilations):
            for c in range(dil):
                views[3 * j + i][0, :, _class_lanes(c)] = (
                    _get_rows(scratch[i], c, tm // dil, dil).astype(BF16))
    u_ref[...] = proj(3) * _sigmoid(proj(4))


def _inproj_sample(x2d, g, w_in_b, qg, kg, seg, tm):
    m, d = x2d.shape
    tm = min(tm, m)
    row = lambda i: (i, 0)
    out_f = jax.ShapeDtypeStruct((m, D_ATTN), F32)
    tile = pl.BlockSpec((tm, D_ATTN), row)
    return pl.pallas_call(
        functools.partial(_inproj_kernel, dilations=()),
        grid=(m // tm,),
        in_specs=[pl.BlockSpec((tm, d), row), _const_spec((1, d)), _const_spec(w_in_b.shape),
                  _const_spec((1, D_ATTN)), _const_spec((1, D_ATTN)), _const_spec((D_ATTN, D_ATTN))],
        out_specs=[tile] * 4,
        out_shape=[out_f] * 4,
        compiler_params=_params(1),
        name="inproj_sample",
    )(x2d, g, w_in_b, qg, kg, seg)


def _inproj_prompt(x2d, g, w_in_b, qg, kg, seg, batch, seq, tm):
    m, d = x2d.shape
    n_t = seq // tm
    dilations = tuple(dil for _, dil in PATTERNS if dil > 1)
    flat = lambda b, t: (b * n_t + t, 0)
    nat_spec = pl.BlockSpec((tm, D_ATTN), flat)
    nat_b = jax.ShapeDtypeStruct((m, D_ATTN), BF16)
    view_specs, view_shapes = [], []
    for dil in dilations:
        view_specs += [pl.BlockSpec((1, tm // dil, dil * D_ATTN), lambda b, t: (b, t, 0))] * 3
        view_shapes += [jax.ShapeDtypeStruct((batch, seq // dil, dil * D_ATTN), BF16)] * 3
    t_spec = pl.BlockSpec((1, D_ATTN, tm), lambda b, t: (b, 0, t))
    t_shape = jax.ShapeDtypeStruct((batch, D_ATTN, seq), F32)
    res = pl.pallas_call(
        functools.partial(_inproj_kernel, dilations=dilations),
        grid=(batch, n_t),
        in_specs=[pl.BlockSpec((tm, d), flat), _const_spec((1, d)), _const_spec(w_in_b.shape),
                  _const_spec((1, D_ATTN)), _const_spec((1, D_ATTN)), _const_spec((D_ATTN, D_ATTN))],
        out_specs=[nat_spec] * 3 + view_specs + [t_spec, t_spec, nat_spec],
        out_shape=[nat_b] * 3 + view_shapes + [t_shape, t_shape, jax.ShapeDtypeStruct((m, D_ATTN), F32)],
        scratch_shapes=[_tiles_scratch(tm, D_ATTN)] * 3,
        compiler_params=_params(2),
        name="inproj_prompt",
    )(x2d, g, w_in_b, qg, kg, seg)
    n_views = 3 * len(dilations)
    qkv = {1: tuple(t.reshape(batch, seq, D_ATTN) for t in res[0:3])}
    for j, dil in enumerate(dilations):
        qkv[dil] = tuple(res[3 + 3 * j:6 + 3 * j])
    kt, vt, u = res[3 + n_views:]
    return qkv, kt, vt, u


def _band_attn_kernel(q_ref, k_ref, v_ref, bias0_ref, bias_ref, o_ref, lse_ref, *, n_blocks, n_classes):
    lane = lax.broadcasted_iota(jnp.int32, (BLK, LANES), 1)
    upper = lane >= HEAD_DIM
    n_pairs = N_HEADS // HEADS_PER_LANE_TILE

    def block(cl, r0, k0, n_keys, bias):
        vps, scores = [], []
        for hp in range(n_pairs):
            lanes = slice(cl * D_ATTN + hp * LANES, cl * D_ATTN + (hp + 1) * LANES)
            qp = q_ref[0, pl.ds(r0, BLK), lanes]
            kp = k_ref[0, pl.ds(k0, n_keys), lanes]
            vps.append(v_ref[0, pl.ds(k0, n_keys), lanes])
            q2 = jnp.concatenate([jnp.where(~upper, qp, jnp.zeros_like(qp)),
                                  jnp.where(upper, qp, jnp.zeros_like(qp))], axis=0)
            s2 = lax.dot_general(q2, kp, (((1,), (1,)), ((), ())), preferred_element_type=F32)
            scores += [s2[hh * BLK:(hh + 1) * BLK] + bias for hh in range(HEADS_PER_LANE_TILE)]
        probs, inv_l = [], []
        lse_tile = jnp.zeros((BLK, LANES), F32)
        for h, s in enumerate(scores):
            m = jnp.max(s, axis=-1, keepdims=True)
            p = jnp.exp2(s - m)
            l = jnp.sum(p, axis=-1, keepdims=True)
            probs.append(p.astype(BF16))
            inv_l.append(1.0 / l)
            lse_tile = jnp.where(lane == h, m + jnp.log(l) * LOG2E, lse_tile)
        lse_ref[0, pl.ds(r0, BLK), cl * LANES:(cl + 1) * LANES] = lse_tile
        for hp in range(n_pairs):
            lanes = slice(cl * D_ATTN + hp * LANES, cl * D_ATTN + (hp + 1) * LANES)
            o2 = jnp.dot(jnp.concatenate(probs[2 * hp:2 * hp + 2], axis=0), vps[hp],
                         preferred_element_type=F32)
            o_lo, o_hi = (o2[hh * BLK:(hh + 1) * BLK] * inv_l[2 * hp + hh]
                          for hh in range(HEADS_PER_LANE_TILE))
            o_ref[0, pl.ds(r0, BLK), lanes] = jnp.where(upper, o_hi, o_lo).astype(o_ref.dtype)

    for cl in range(n_classes):
        block(cl, 0, 0, BLK, bias0_ref[...])
        if n_blocks > 1:
            def body(i, carry, cl=cl):
                r0 = pl.multiple_of(i * BLK, BLK)
                block(cl, r0, pl.multiple_of(r0 - BLK, BLK), 2 * BLK, bias_ref[...])
                return carry

            lax.fori_loop(1, n_blocks, body, 0)


def _band_biases(n_sub):
    qi = np.arange(BLK)[:, None]
    ki = np.arange(2 * BLK)[None, :]
    dist = BLK + qi - ki
    band = (dist >= 0) & (dist <= n_sub)
    d0 = qi - np.arange(BLK)[None, :]
    to_bias = lambda ok: np.where(ok, 0.0, NEG_BIG).astype(np.float32)
    return to_bias((d0 >= 0) & (d0 <= n_sub)), to_bias(band)


BAND_CLASSES_PER_STEP = 8


def _band_attention(q, k, v, window, dilation):
    batch, length, _ = q.shape
    cps = min(BAND_CLASSES_PER_STEP, dilation)
    bias0, bias = _band_biases(window // dilation)
    blk = pl.BlockSpec((1, length, cps * D_ATTN), lambda b, c: (b, 0, c))
    return pl.pallas_call(
        functools.partial(_band_attn_kernel, n_blocks=length // BLK, n_classes=cps),
        grid=(batch, dilation // cps),
        in_specs=[blk, blk, blk, _const_spec((BLK, BLK)), _const_spec((BLK, 2 * BLK))],
        out_specs=[blk, pl.BlockSpec((1, length, cps * LANES), lambda b, c: (b, 0, c))],
        out_shape=[jax.ShapeDtypeStruct(q.shape, BF16),
                   jax.ShapeDtypeStruct((batch, length, dilation * LANES), F32)],
        compiler_params=_params(2),
        name=f"band_attn_d{dilation}",
    )(q, k, v, jnp.asarray(bias0), jnp.asarray(bias))


CONV_PAD = 32
SUBLANES = 8


def _mix_kernel(u_ref, up_ref, w_ref, b_ref, lg_ref, lb_ref, expand_ref, *refs, tt, dilations, n_cast):
    n_pat = 1 + len(dilations)
    ins = refs[0:2 * n_pat]
    cast_in = refs[2 * n_pat:2 * n_pat + n_cast]
    mix_ref = refs[2 * n_pat + n_cast]
    cast_out = refs[2 * n_pat + n_cast + 1:2 * n_pat + 2 * n_cast + 1]
    ext_ref = refs[2 * n_pat + 2 * n_cast + 1]
    nat = refs[2 * n_pat + 2 * n_cast + 2:]
    for src, dst in zip(cast_in, cast_out):
        dst[...] = src[...].astype(dst.dtype)
    t = pl.program_id(1)
    rows = CONV_PAD + tt
    prev = up_ref[0]
    ext_ref[0, 0:CONV_PAD, :] = jnp.where(t > 0, prev, jnp.zeros_like(prev))
    ext_ref[0, CONV_PAD:rows, :] = u_ref[0]
    for s in range(1, SUBLANES):
        ext_ref[s, 0:rows - SUBLANES, :] = ext_ref[0, pl.ds(s, rows - SUBLANES), :]

    for j, dil in enumerate(dilations):
        for i, width in enumerate((D_ATTN, LANES)):
            src, dst = ins[2 + 2 * j + i], nat[2 * j + i]
            for c in range(dil):
                _put_rows(dst, c, dil, src[0, :, c * width:(c + 1) * width].astype(F32))
    os_ = [ins[0][...].astype(F32)] + [_get_tiles(nat[2 * j]) for j in range(len(dilations))]
    ls_ = [ins[1][...]] + [_get_tiles(nat[2 * j + 1]) for j in range(len(dilations))]
    mx = functools.reduce(jnp.maximum, ls_)
    es = [jnp.exp2(l - mx) for l in ls_]
    inv = 1.0 / sum(es)
    attn = None
    for e, o in zip(es, os_):
        w = e * inv
        hi = w.astype(BF16)
        lo = (w - hi.astype(F32)).astype(BF16)
        wide = (jnp.dot(hi, expand_ref[...], preferred_element_type=F32)
                + jnp.dot(lo, expand_ref[...], preferred_element_type=F32))
        attn = wide * o if attn is None else attn + wide * o
    mix_ref[:, 0:D_ATTN] = attn.astype(mix_ref.dtype)

    d_conv = u_ref.shape[-1]
    acc = None
    for j in range(CONV_WIDTH):
        off = CONV_PAD - CONV_LEFT + j
        s = off % SUBLANES
        term = ext_ref[s, pl.ds(off - s, tt), :] * w_ref[j:j + 1, :]
        acc = term if acc is None else acc + term
    c = _ln_swish(acc + b_ref[...], lg_ref[...], lb_ref[...])
    mix_ref[:, D_ATTN:D_ATTN + d_conv] = c.astype(mix_ref.dtype)


def _ln_swish(c, g, b):
    mu = jnp.mean(c, axis=-1, keepdims=True)
    cc = c - mu
    var = jnp.mean(cc * cc, axis=-1, keepdims=True)
    y = cc * lax.rsqrt(var + LN_EPS) * g + b
    return y * _sigmoid(y)


def _cast_blocking(rows, n_steps):
    for n_blocks in range(n_steps, 0, -1):
        if n_steps % n_blocks == 0 and rows % (n_blocks * 2 * SUBLANES) == 0:
            return rows // n_blocks, n_steps // n_blocks
    raise ValueError((rows, n_steps))


def _mix_prompt(u, conv_w, conv_b, ln_g, ln_b, attn, weights, batch, seq, tt):
    d_conv = u.shape[-1]
    u3 = u.reshape(batch, seq, d_conv)
    n_t = seq // tt
    per = tt // CONV_PAD
    dilations = tuple(dil for _, dil in PATTERNS if dil > 1)
    cur = pl.BlockSpec((1, tt, d_conv), lambda b, t: (b, t, 0))
    prev = pl.BlockSpec((1, CONV_PAD, d_conv), lambda b, t: (b, jnp.maximum(t * per - 1, 0), 0))
    flat = lambda width: pl.BlockSpec((tt, width), lambda b, t: (b * n_t + t, 0))
    view = lambda dil, width: pl.BlockSpec((1, tt // dil, dil * width), lambda b, t: (b, t, 0))
    ins = [attn[1][0].reshape(batch * seq, D_ATTN), attn[1][1].reshape(batch * seq, LANES)]
    specs = [flat(D_ATTN), flat(LANES)]
    scratch = [pltpu.VMEM((SUBLANES, CONV_PAD + tt, d_conv), F32)]
    for dil in dilations:
        ins += list(attn[dil])
        specs += [view(dil, D_ATTN), view(dil, LANES)]
        scratch += [_tiles_scratch(tt, D_ATTN), _tiles_scratch(tt, LANES)]
    expand = np.zeros((LANES, D_ATTN), np.float32)
    expand[:N_HEADS] = np.kron(np.eye(N_HEADS), np.ones((1, HEAD_DIM)))
    cast_specs = []
    for w in weights:
        rows, hold = _cast_blocking(w.shape[0], batch * n_t)
        cast_specs.append(pl.BlockSpec((rows, w.shape[1]), lambda b, t, hold=hold: ((b * n_t + t) // hold, 0)))
    res = pl.pallas_call(
        functools.partial(_mix_kernel, tt=tt, dilations=dilations, n_cast=len(weights)),
        grid=(batch, n_t),
        in_specs=[cur, prev, _const_spec(conv_w.shape), _const_spec((1, d_conv)),
                  _const_spec((1, d_conv)), _const_spec((1, d_conv)),
                  _const_spec((LANES, D_ATTN))] + specs + cast_specs,
        out_specs=[pl.BlockSpec((tt, D_ATTN + d_conv), lambda b, t: (b * n_t + t, 0))] + cast_specs,
        out_shape=[jax.ShapeDtypeStruct((batch * seq, D_ATTN + d_conv), BF16)]
        + [jax.ShapeDtypeStruct(w.shape, BF16) for w in weights],
        scratch_shapes=scratch,
        compiler_params=_params(2),
        name="mix_prompt",
    )(u3, u3, conv_w, conv_b, ln_g, ln_b, jnp.asarray(expand, BF16), *ins, *weights)
    return res[0], res[1:]


def _sample_conv_kernel(st_ref, u_ref, w_ref, b_ref, lg_ref, lb_ref, c_ref, ns_ref, us_ref, cs_ref,
                        *, nb, t_new):
    _put_tiles(us_ref, u_ref[...])
    new = [_get_rows(us_ref, t, nb, t_new) for t in range(t_new)]

    def ext(tau):
        return st_ref[tau] if tau < CONV_LEFT else new[tau - CONV_LEFT]

    for t in range(t_new):
        acc = None
        for j in range(CONV_WIDTH):
            term = ext(t + j) * w_ref[j:j + 1, :]
            acc = term if acc is None else acc + term
        c = _ln_swish(acc + b_ref[...], lg_ref[...], lb_ref[...])
        _put_rows(cs_ref, t, t_new, c)
    c_ref[...] = _get_tiles(cs_ref)
    for tau in range(CONV_LEFT):
        ns_ref[tau] = ext(tau + t_new)


def _sample_conv(state_t, u_s, conv_w, conv_b, ln_g, ln_b, layer, t_new, nb):
    _, n, d_conv = state_t.shape
    return pl.pallas_call(
        functools.partial(_sample_conv_kernel, nb=nb, t_new=t_new),
        grid=(n // nb,),
        in_specs=[pl.BlockSpec((CONV_LEFT, nb, d_conv), lambda i: (layer, i, 0)),
                  pl.BlockSpec((nb * t_new, d_conv), lambda i: (i, 0)),
                  _const_spec(conv_w.shape), _const_spec((1, d_conv)),
                  _const_spec((1, d_conv)), _const_spec((1, d_conv))],
        out_specs=[pl.BlockSpec((nb * t_new, d_conv), lambda i: (i, 0)),
                   pl.BlockSpec((CONV_LEFT, nb, d_conv), lambda i: (0, i, 0))],
        out_shape=[jax.ShapeDtypeStruct((n * t_new, d_conv), F32),
                   jax.ShapeDtypeStruct((CONV_LEFT, n, d_conv), F32)],
        scratch_shapes=[_tiles_scratch(nb * t_new, d_conv)] * 2,
        compiler_params=_params(1),
        name="sample_conv",
    )(state_t, u_s, conv_w, conv_b, ln_g, ln_b)


def _multiplicity(dist):
    dist = np.asarray(dist)
    c = np.zeros(dist.shape, np.float32)
    for window, dilation in PATTERNS:
        c += ((dist >= 0) & (dist <= window) & (dist % dilation == 0)).astype(np.float32)
    return c


def _shift_append(old, new_rows, out_ref, t_new):
    n_ch, l_buf = old.shape
    n_tiles = l_buf // LANES
    lane = lax.broadcasted_iota(jnp.int32, (n_ch, LANES), 1)
    keep = lane < LANES - t_new
    pad = jnp.concatenate([jnp.zeros((LANES - t_new, n_ch), F32), new_rows], axis=0)
    nxt = pad.T
    for j in reversed(range(n_tiles)):
        cur = pltpu.roll(old[:, j * LANES:(j + 1) * LANES], LANES - t_new, axis=1)
        out_ref[0, :, j * LANES:(j + 1) * LANES] = jnp.where(keep, cur, nxt)
        nxt = cur


CACHE_HEADS_PER_UNIT = 4


def _sample_attend(q_ref, kn_ref, vn_ref, ck, cv, cnt_ref, attn_ref, t_new):
    width = ck.shape[0]
    n_heads = width // HEAD_DIM
    rows = n_heads * t_new
    kn = kn_ref[0]
    vn = vn_ref[0]

    row = lax.broadcasted_iota(jnp.int32, (rows, width), 0)
    lane = lax.broadcasted_iota(jnp.int32, (rows, width), 1)
    own_head = (row // t_new) == (lane // HEAD_DIM)
    q_rep = jnp.concatenate([q_ref[0]] * n_heads, axis=0)
    q_bd = jnp.where(own_head, q_rep, 0.0)

    cnt = cnt_ref[...]
    s_c = jnp.dot(q_bd.astype(BF16), ck.astype(BF16), preferred_element_type=F32)
    s_c = jnp.where(cnt > 0.0, s_c, NEG_BIG)
    m = jnp.max(s_c, axis=-1, keepdims=True)

    t_of_row = lax.broadcasted_iota(jnp.int32, (rows, 1), 0) % t_new
    s_n, c_n = [], []
    for tp in range(t_new):
        d = t_of_row - tp
        c = jnp.zeros((rows, 1), F32)
        for _, dilation in PATTERNS:
            c = c + jnp.where((d >= 0) & (d % dilation == 0), 1.0, 0.0)
        s = jnp.sum(q_bd * kn[tp:tp + 1, :], axis=-1, keepdims=True)
        s = jnp.where(c > 0.0, s, NEG_BIG)
        m = jnp.maximum(m, s)
        s_n.append(s)
        c_n.append(c)

    p_c = cnt * jnp.exp2(s_c - m)
    l = jnp.sum(p_c, axis=-1, keepdims=True)
    acc = lax.dot_general(p_c.astype(BF16), cv.astype(BF16), (((1,), (1,)), ((), ())),
                          preferred_element_type=F32)
    for tp in range(t_new):
        p = c_n[tp] * jnp.exp2(s_n[tp] - m)
        l = l + p
        acc = acc + p * vn[tp:tp + 1, :]
    acc = jnp.where(own_head, acc * (1.0 / l), 0.0)
    out = acc[0:t_new, :]
    for h in range(1, n_heads):
        out = out + acc[h * t_new:(h + 1) * t_new, :]
    attn_ref[0] = out.astype(attn_ref.dtype)


CACHE_RING = 3


def _swiglu_chunk(h, wg_ref, wu_ref, wd_ref, j):
    cols = pl.ds(pl.multiple_of(j * MXU_COLS, MXU_COLS), MXU_COLS)
    gate = jnp.dot(h, wg_ref[:, cols], preferred_element_type=F32)
    up = jnp.dot(h, wu_ref[:, cols], preferred_element_type=F32)
    act = (gate * _sigmoid(gate) * up).astype(BF16)
    return jnp.dot(act, wd_ref[j], preferred_element_type=F32)


def _ffn_cache_kernel(x_ref, mix_ref, wo_ref, g_ref, wg_ref, wu_ref, wd_ref,
                      q_ref, kn_ref, vn_ref, ck_hbm, cv_hbm, cnt_ref,
                      y_ref, attn_ref, ok_ref, ov_ref, h_ref, kbuf, vbuf, sem,
                      *, steps_per_tile, t_new, first_row, units_per_row):
    u = pl.program_id(0)
    n_units = pl.num_programs(0)
    c = u % steps_per_tile
    base, extra = divmod(wd_ref.shape[0], steps_per_tile)
    uw = kbuf.shape[1]

    def fetch(unit, slot):
        row = first_row + unit // units_per_row
        h0 = pl.multiple_of((unit % units_per_row) * uw, uw)
        return (pltpu.make_async_copy(ck_hbm.at[row, pl.ds(h0, uw), :], kbuf.at[slot], sem.at[0, slot]),
                pltpu.make_async_copy(cv_hbm.at[row, pl.ds(h0, uw), :], vbuf.at[slot], sem.at[1, slot]))

    @pl.when(u == 0)
    def _():
        for unit in range(CACHE_RING - 1):
            for cp in fetch(unit, unit):
                cp.start()

    ahead = u + (CACHE_RING - 1)

    @pl.when(ahead < n_units)
    def _():
        for cp in fetch(ahead, ahead % CACHE_RING):
            cp.start()

    @pl.when(c == 0)
    def _():
        x1 = x_ref[...] + jnp.dot(mix_ref[...], wo_ref[...], preferred_element_type=F32)
        y_ref[...] = x1
        h_ref[...] = _rms_rows(x1, g_ref[...]).astype(BF16)

    slot = u % CACHE_RING
    for cp in fetch(u, slot):
        cp.wait()
    ck, cv = kbuf[slot], vbuf[slot]
    _shift_append(ck, kn_ref[0], ok_ref, t_new)
    _shift_append(cv, vn_ref[0], ov_ref, t_new)
    _sample_attend(q_ref, kn_ref, vn_ref, ck, cv, cnt_ref, attn_ref, t_new)
    for i in range(base):
        y_ref[...] += _swiglu_chunk(h_ref[...], wg_ref, wu_ref, wd_ref, c * base + i)
    if extra:
        first = steps_per_tile - extra

        @pl.when(c >= first)
        def _():
            y_ref[...] += _swiglu_chunk(h_ref[...], wg_ref, wu_ref, wd_ref, steps_per_tile * base + c - first)


def _ffn_with_cache(x2d, mix, wo_b, g, wg3, wu3, wd3, q_s, k_s, v_s, cache_kt, cache_vt, layer, tm):
    m, d = x2d.shape
    n, t_new, _ = q_s.shape
    l_buf = cache_kt.shape[2]
    assert l_buf == WIN_MAX and t_new % SUBLANES == 0
    uw = CACHE_HEADS_PER_UNIT * HEAD_DIM
    per_n = D_ATTN // uw
    n_tiles, n_units = m // tm, n * per_n
    assert n_units % n_tiles == 0
    steps_per_tile = n_units // n_tiles
    assert wd3.shape[0] >= steps_per_tile and n_units >= CACHE_RING
    dist = l_buf + np.arange(t_new)[:, None] - np.arange(l_buf)[None, :]
    cnt = np.tile(_multiplicity(dist), (CACHE_HEADS_PER_UNIT, 1))

    tile = lambda width: pl.BlockSpec((tm, width), lambda u: (u // steps_per_tile, 0))
    new = pl.BlockSpec((1, t_new, uw), lambda u: (u // per_n, 0, u % per_n))
    big = pl.BlockSpec((1, uw, l_buf), lambda u: (u // per_n, u % per_n, 0))
    in_hbm = pl.BlockSpec(memory_space=pl.ANY)
    return pl.pallas_call(
        functools.partial(_ffn_cache_kernel, steps_per_tile=steps_per_tile, t_new=t_new,
                          first_row=layer * n, units_per_row=per_n),
        grid=(n_units,),
        in_specs=[tile(d), tile(mix.shape[1]), _const_spec(wo_b.shape), _const_spec((1, d)),
                  _const_spec(wg3.shape), _const_spec(wu3.shape), _const_spec(wd3.shape),
                  new, new, new, in_hbm, in_hbm, _const_spec(cnt.shape)],
        out_specs=[tile(d), new, big, big],
        out_shape=[jax.ShapeDtypeStruct((m, d), F32),
                   jax.ShapeDtypeStruct((n, t_new, D_ATTN), F32),
                   jax.ShapeDtypeStruct((n, D_ATTN, l_buf), F32),
                   jax.ShapeDtypeStruct((n, D_ATTN, l_buf), F32)],
        scratch_shapes=[pltpu.VMEM((tm, d), BF16),
                        pltpu.VMEM((CACHE_RING, uw, l_buf), F32), pltpu.VMEM((CACHE_RING, uw, l_buf), F32),
                        pltpu.SemaphoreType.DMA((2, CACHE_RING))],
        compiler_params=_params(1),
        name="ffn_cache",
    )(x2d, mix, wo_b, g, wg3, wu3, wd3, q_s, k_s, v_s, cache_kt, cache_vt, jnp.asarray(cnt))


def _ffn_kernel(x_ref, *refs, n_mix):
    mix_refs = refs[0:n_mix]
    wo_ref, g_ref, wg_ref, wu_ref, wd_ref, y_ref = refs[n_mix:]
    x1 = x_ref[...]
    row = 0
    for mr in mix_refs:
        width = mr.shape[1]
        x1 = x1 + jnp.dot(mr[...].astype(BF16), wo_ref[row:row + width, :], preferred_element_type=F32)
        row += width
    h = _rms_rows(x1, g_ref[...]).astype(BF16)
    y_ref[...] = x1
    for j in range(wd_ref.shape[0]):
        y_ref[...] += _swiglu_chunk(h, wg_ref, wu_ref, wd_ref, j)


def _outproj_ffn(x2d, mix_parts, wo_b, g, wg3, wu3, wd3, tm):
    m, d = x2d.shape
    tm = min(tm, m)
    row = lambda i: (i, 0)
    return pl.pallas_call(
        functools.partial(_ffn_kernel, n_mix=len(mix_parts)),
        grid=(m // tm,),
        in_specs=[pl.BlockSpec((tm, d), row)] + [pl.BlockSpec((tm, p.shape[1]), row) for p in mix_parts]
        + [_const_spec(wo_b.shape), _const_spec((1, d)),
           _const_spec(wg3.shape), _const_spec(wu3.shape), _const_spec(wd3.shape)],
        out_specs=pl.BlockSpec((tm, d), row),
        out_shape=jax.ShapeDtypeStruct((m, d), F32),
        compiler_params=_params(1),
        name="outproj_ffn",
    )(x2d, *mix_parts, wo_b, g, wg3, wu3, wd3)


TOKEN_TILE = 512
INPROJ_TILE = 1024
MIX_TIME_TILE = 1024
SAMPLE_CONV_BATCH = 32


def kernel(x_prompt, x_sample, cache_k, cache_v, state_conv, attn_norm_g, w_in, q_norm_g, k_norm_g,
           conv_w, conv_b, conv_ln_g, conv_ln_b, w_out, ffn_norm_g, w_gate, w_up, w_down):
    batch, seq, d_model = x_prompt.shape
    n_dec, t_new, _ = x_sample.shape
    depth = w_in.shape[0]
    d_conv = conv_w.shape[-1]
    l_buf = cache_k.shape[2]
    assert seq == WIN_MAX and l_buf == WIN_MAX and seq % (BLK * PATTERNS[-1][1]) == 0
    assert w_in.shape[-1] == 3 * D_ATTN + 2 * d_conv and d_conv == D_ATTN

    seg = jnp.asarray(np.kron(np.eye(N_HEADS), np.ones((HEAD_DIM, HEAD_DIM))), BF16)
    row = lambda v: v.reshape(1, -1)
    to_t = lambda c: jnp.transpose(c, (0, 1, 3, 4, 2)).reshape(depth * n_dec, D_ATTN, l_buf)
    from_t = lambda c, n: jnp.transpose(c.reshape(n, N_HEADS, HEAD_DIM, -1), (0, 3, 1, 2))
    cache_kt, cache_vt = to_t(cache_k), to_t(cache_v)
    state_t = jnp.transpose(state_conv, (0, 2, 1, 3)).reshape(depth * CONV_LEFT, n_dec, d_conv)

    y_p = x_prompt.reshape(batch * seq, d_model)
    y_s = x_sample.reshape(n_dec * t_new, d_model)
    outs = [[] for _ in range(6)]
    for l in range(depth):
        w_in_b = w_in[l].astype(BF16)
        g_attn, g_ffn = row(attn_norm_g[l]), row(ffn_norm_g[l])
        qg = row(jnp.tile(q_norm_g[l], N_HEADS))
        kg = row(jnp.tile(k_norm_g[l], N_HEADS))
        cb, lg, lb = row(conv_b[l]), row(conv_ln_g[l]), row(conv_ln_b[l])

        q, k, v, u_s = _inproj_sample(y_s, g_attn, w_in_b, qg, kg, seg, TOKEN_TILE)
        c_s, ns = _sample_conv(state_t, u_s, conv_w[l], cb, lg, lb, l, t_new, min(SAMPLE_CONV_BATCH, n_dec))
        qkv, kt, vt, u = _inproj_prompt(y_p, g_attn, w_in_b, qg, kg, seg, batch, seq, INPROJ_TILE)
        attn = {dil: _band_attention(*qkv[dil], w, dil) for w, dil in PATTERNS}
        mix, (wo_b, wg3, wu3, wd_b) = _mix_prompt(u, conv_w[l], cb, lg, lb, attn,
                                                   (w_out[l], w_gate[l], w_up[l], w_down[l]),
                                                   batch, seq, MIX_TIME_TILE)
        wd3 = wd_b.reshape(wd_b.shape[0] // MXU_COLS, MXU_COLS, d_model)

        as3 = lambda t: t.reshape(n_dec, t_new, D_ATTN)
        y_p, attn_s, nk, nv = _ffn_with_cache(y_p, mix, wo_b, g_ffn, wg3, wu3, wd3,
                                              as3(q), as3(k), as3(v), cache_kt, cache_vt, l, TOKEN_TILE)
        y_s = _outproj_ffn(y_s, [attn_s.reshape(n_dec * t_new, D_ATTN), c_s],
                           wo_b, g_ffn, wg3, wu3, wd3, TOKEN_TILE)
        outs[0].append(from_t(kt, batch))
        outs[1].append(from_t(vt, batch))
        outs[2].append(u.reshape(batch, seq, d_conv)[:, seq - CONV_LEFT:])
        outs[3].append(from_t(nk, n_dec))
        outs[4].append(from_t(nv, n_dec))
        outs[5].append(jnp.transpose(ns, (1, 0, 2)))

    stack = lambda xs: xs[0][None] if len(xs) == 1 else jnp.stack(xs)
    return (y_p.reshape(batch, seq, d_model), y_s.reshape(n_dec, t_new, d_model)) + tuple(
        stack(o) for o in outs)
```

```python
import functools

import numpy as np
import jax
import jax.numpy as jnp
from jax import lax
from jax.experimental import pallas as pl
from jax.experimental.pallas import tpu as pltpu

N_HEADS = 8
HEAD_DIM = 64
D_ATTN = N_HEADS * HEAD_DIM
PATTERNS = ((128, 1), (512, 4), (2048, 16))
WIN_MAX = max(w for w, _ in PATTERNS)
CONV_WIDTH = 31
CONV_LEFT = CONV_WIDTH - 1
BLK = 128
RMS_EPS = 1e-6
LN_EPS = 1e-5
ATTN_SCALE = HEAD_DIM ** -0.5
LOG2E = 1.4426950408889634
NEG_BIG = -1e30

LANES = 128
MXU_COLS = 256
HEADS_PER_LANE_TILE = LANES // HEAD_DIM
VMEM_LIMIT_BYTES = 56 * 1024 * 1024

F32 = jnp.float32
BF16 = jnp.bfloat16


def _params(n_axes):
    return pltpu.CompilerParams(dimension_semantics=("arbitrary",) * n_axes,
                                vmem_limit_bytes=VMEM_LIMIT_BYTES)


def _const_spec(shape):
    return pl.BlockSpec(shape, lambda *_: (0,) * len(shape), pipeline_mode=pl.Buffered(1))


def _sigmoid(x):
    return 1.0 / (1.0 + jnp.exp(-x))


def _rms_rows(x, g):
    ms = jnp.mean(x * x, axis=-1, keepdims=True)
    return x * lax.rsqrt(ms + RMS_EPS) * g


def _class_lanes(c):
    return slice(c * D_ATTN, (c + 1) * D_ATTN)


def _tiles_scratch(rows, width):
    return pltpu.VMEM((width // LANES, rows, LANES), F32)


def _put_tiles(scr, x):
    for lt in range(scr.shape[0]):
        scr[lt] = x[:, lt * LANES:(lt + 1) * LANES]


def _get_tiles(scr):
    return jnp.concatenate([scr[lt] for lt in range(scr.shape[0])], axis=1)


def _get_rows(scr, start, n, stride):
    return jnp.concatenate([scr[lt, pl.ds(start, n, stride=stride), :] for lt in range(scr.shape[0])],
                           axis=1)


def _put_rows(scr, start, stride, x):
    for lt in range(scr.shape[0]):
        scr[lt, pl.ds(start, x.shape[0], stride=stride), :] = x[:, lt * LANES:(lt + 1) * LANES]


def _inproj_kernel(x_ref, g_ref, w_ref, qg_ref, kg_ref, seg_ref, *refs, dilations):
    h = _rms_rows(x_ref[...], g_ref[...]).astype(BF16)

    def proj(col):
        return jnp.dot(h, w_ref[:, col * D_ATTN:(col + 1) * D_ATTN], preferred_element_type=F32)

    def head_rms(t, gain):
        ss = jnp.dot((t * t).astype(BF16), seg_ref[...], preferred_element_type=F32)
        return t * lax.rsqrt(ss * (1.0 / HEAD_DIM) + RMS_EPS) * gain

    make = (lambda: head_rms(proj(0), qg_ref[...]) * (ATTN_SCALE * LOG2E),
            lambda: head_rms(proj(1), kg_ref[...]),
            lambda: proj(2))
    if not dilations:
        for ref, f in zip(refs, make):
            ref[...] = f()
        refs[3][...] = proj(3) * _sigmoid(proj(4))
        return

    n_views = 3 * len(dilations)
    nat = refs[0:3]
    views = refs[3:3 + n_views]
    t_refs = (None,) + refs[3 + n_views:5 + n_views]
    u_ref = refs[5 + n_views]
    scratch = refs[6 + n_views:]
    tm = x_ref.shape[0]
    for i, f in enumerate(make):
        t = f()
        if t_refs[i] is not None:
            t_refs[i][0] = t.T
        nat[i][...] = t.astype(BF16)
        _put_tiles(scratch[i], t)
        for j, dil in enumerate(dilations):
            for c in range(dil):
                views[3 * j + i][0, :, _class_lanes(c)] = (
                    _get_rows(scratch[i], c, tm // dil, dil).astype(BF16))
    u_ref[...] = proj(3) * _sigmoid(proj(4))


def _inproj_sample(x2d, g, w_in_b, qg, kg, seg, tm):
    m, d = x2d.shape
    tm = min(tm, m)
    row = lambda i: (i, 0)
    out_f = jax.ShapeDtypeStruct((m, D_ATTN), F32)
    tile = pl.BlockSpec((tm, D_ATTN), row)
    return pl.pallas_call(
        functools.partial(_inproj_kernel, dilations=()),
        grid=(m // tm,),
        in_specs=[pl.BlockSpec((tm, d), row), _const_spec((1, d)), _const_spec(w_in_b.shape),
                  _const_spec((1, D_ATTN)), _const_spec((1, D_ATTN)), _const_spec((D_ATTN, D_ATTN))],
        out_specs=[tile] * 4,
        out_shape=[out_f] * 4,
        compiler_params=_params(1),
        name="inproj_sample",
    )(x2d, g, w_in_b, qg, kg, seg)


def _inproj_prompt(x2d, g, w_in_b, qg, kg, seg, batch, seq, tm):
    m, d = x2d.shape
    n_t = seq // tm
    dilations = tuple(dil for _, dil in PATTERNS if dil > 1)
    flat = lambda b, t: (b * n_t + t, 0)
    nat_spec = pl.BlockSpec((tm, D_ATTN), flat)
    nat_b = jax.ShapeDtypeStruct((m, D_ATTN), BF16)
    view_specs, view_shapes = [], []
    for dil in dilations:
        view_specs += [pl.BlockSpec((1, tm // dil, dil * D_ATTN), lambda b, t: (b, t, 0))] * 3
        view_shapes += [jax.ShapeDtypeStruct((batch, seq // dil, dil * D_ATTN), BF16)] * 3
    t_spec = pl.BlockSpec((1, D_ATTN, tm), lambda b, t: (b, 0, t))
    t_shape = jax.ShapeDtypeStruct((batch, D_ATTN, seq), F32)
    res = pl.pallas_call(
        functools.partial(_inproj_kernel, dilations=dilations),
        grid=(batch, n_t),
        in_specs=[pl.BlockSpec((tm, d), flat), _const_spec((1, d)), _const_spec(w_in_b.shape),
                  _const_spec((1, D_ATTN)), _const_spec((1, D_ATTN)), _const_spec((D_ATTN, D_ATTN))],
        out_specs=[nat_spec] * 3 + view_specs + [t_spec, t_spec, nat_spec],
        out_shape=[nat_b] * 3 + view_shapes + [t_shape, t_shape, jax.ShapeDtypeStruct((m, D_ATTN), F32)],
        scratch_shapes=[_tiles_scratch(tm, D_ATTN)] * 3,
        compiler_params=_params(2),
        name="inproj_prompt",
    )(x2d, g, w_in_b, qg, kg, seg)
    n_views = 3 * len(dilations)
    qkv = {1: tuple(t.reshape(batch, seq, D_ATTN) for t in res[0:3])}
    for j, dil in enumerate(dilations):
        qkv[dil] = tuple(res[3 + 3 * j:6 + 3 * j])
    kt, vt, u = res[3 + n_views:]
    return qkv, kt, vt, u


def _band_attn_kernel(q_ref, k_ref, v_ref, bias0_ref, bias_ref, o_ref, lse_ref, *, n_blocks, n_classes):
    lane = lax.broadcasted_iota(jnp.int32, (BLK, LANES), 1)
    upper = lane >= HEAD_DIM
    n_pairs = N_HEADS // HEADS_PER_LANE_TILE

    def block(cl, r0, k0, n_keys, bias):
        vps, scores = [], []
        for hp in range(n_pairs):
            lanes = slice(cl * D_ATTN + hp * LANES, cl * D_ATTN + (hp + 1) * LANES)
            qp = q_ref[0, pl.ds(r0, BLK), lanes]
            kp = k_ref[0, pl.ds(k0, n_keys), lanes]
            vps.append(v_ref[0, pl.ds(k0, n_keys), lanes])
            q2 = jnp.concatenate([jnp.where(~upper, qp, jnp.zeros_like(qp)),
                                  jnp.where(upper, qp, jnp.zeros_like(qp))], axis=0)
            s2 = lax.dot_general(q2, kp, (((1,), (1,)), ((), ())), preferred_element_type=F32)
            scores += [s2[hh * BLK:(hh + 1) * BLK] + bias for hh in range(HEADS_PER_LANE_TILE)]
        probs, inv_l = [], []
        lse_tile = jnp.zeros((BLK, LANES), F32)
        for h, s in enumerate(scores):
            m = jnp.max(s, axis=-1, keepdims=True)
            p = jnp.exp2(s - m)
            l = jnp.sum(p, axis=-1, keepdims=True)
            probs.append(p.astype(BF16))
            inv_l.append(1.0 / l)
            lse_tile = jnp.where(lane == h, m + jnp.log(l) * LOG2E, lse_tile)
        lse_ref[0, pl.ds(r0, BLK), cl * LANES:(cl + 1) * LANES] = lse_tile
        for hp in range(n_pairs):
            lanes = slice(cl * D_ATTN + hp * LANES, cl * D_ATTN + (hp + 1) * LANES)
            o2 = jnp.dot(jnp.concatenate(probs[2 * hp:2 * hp + 2], axis=0), vps[hp],
                         preferred_element_type=F32)
            o_lo, o_hi = (o2[hh * BLK:(hh + 1) * BLK] * inv_l[2 * hp + hh]
                          for hh in range(HEADS_PER_LANE_TILE))
            o_ref[0, pl.ds(r0, BLK), lanes] = jnp.where(upper, o_hi, o_lo).astype(o_ref.dtype)

    for cl in range(n_classes):
        block(cl, 0, 0, BLK, bias0_ref[...])
        if n_blocks > 1:
            def body(i, carry, cl=cl):
                r0 = pl.multiple_of(i * BLK, BLK)
                block(cl, r0, pl.multiple_of(r0 - BLK, BLK), 2 * BLK, bias_ref[...])
                return carry

            lax.fori_loop(1, n_blocks, body, 0)


def _band_biases(n_sub):
    qi = np.arange(BLK)[:, None]
    ki = np.arange(2 * BLK)[None, :]
    dist = BLK + qi - ki
    band = (dist >= 0) & (dist <= n_sub)
    d0 = qi - np.arange(BLK)[None, :]
    to_bias = lambda ok: np.where(ok, 0.0, NEG_BIG).astype(np.float32)
    return to_bias((d0 >= 0) & (d0 <= n_sub)), to_bias(band)


BAND_CLASSES_PER_STEP = 8


def _band_attention(q, k, v, window, dilation):
    batch, length, _ = q.shape
    cps = min(BAND_CLASSES_PER_STEP, dilation)
    bias0, bias = _band_biases(window // dilation)
    blk = pl.BlockSpec((1, length, cps * D_ATTN), lambda b, c: (b, 0, c))
    return pl.pallas_call(
        functools.partial(_band_attn_kernel, n_blocks=length // BLK, n_classes=cps),
        grid=(batch, dilation // cps),
        in_specs=[blk, blk, blk, _const_spec((BLK, BLK)), _const_spec((BLK, 2 * BLK))],
        out_specs=[blk, pl.BlockSpec((1, length, cps * LANES), lambda b, c: (b, 0, c))],
        out_shape=[jax.ShapeDtypeStruct(q.shape, BF16),
                   jax.ShapeDtypeStruct((batch, length, dilation * LANES), F32)],
        compiler_params=_params(2),
        name=f"band_attn_d{dilation}",
    )(q, k, v, jnp.asarray(bias0), jnp.asarray(bias))


CONV_PAD = 32
SUBLANES = 8


def _mix_kernel(u_ref, up_ref, w_ref, b_ref, lg_ref, lb_ref, expand_ref, *refs, tt, dilations, n_cast):
    n_pat = 1 + len(dilations)
    ins = refs[0:2 * n_pat]
    cast_in = refs[2 * n_pat:2 * n_pat + n_cast]
    mix_ref = refs[2 * n_pat + n_cast]
    cast_out = refs[2 * n_pat + n_cast + 1:2 * n_pat + 2 * n_cast + 1]
    ext_ref = refs[2 * n_pat + 2 * n_cast + 1]
    nat = refs[2 * n_pat + 2 * n_cast + 2:]
    for src, dst in zip(cast_in, cast_out):
        dst[...] = src[...].astype(dst.dtype)
    t = pl.program_id(1)
    rows = CONV_PAD + tt
    prev = up_ref[0]
    ext_ref[0, 0:CONV_PAD, :] = jnp.where(t > 0, prev, jnp.zeros_like(prev))
    ext_ref[0, CONV_PAD:rows, :] = u_ref[0]
    for s in range(1, SUBLANES):
        ext_ref[s, 0:rows - SUBLANES, :] = ext_ref[0, pl.ds(s, rows - SUBLANES), :]

    for j, dil in enumerate(dilations):
        for i, width in enumerate((D_ATTN, LANES)):
            src, dst = ins[2 + 2 * j + i], nat[2 * j + i]
            for c in range(dil):
                _put_rows(dst, c, dil, src[0, :, c * width:(c + 1) * width].astype(F32))
    os_ = [ins[0][...].astype(F32)] + [_get_tiles(nat[2 * j]) for j in range(len(dilations))]
    ls_ = [ins[1][...]] + [_get_tiles(nat[2 * j + 1]) for j in range(len(dilations))]
    mx = functools.reduce(jnp.maximum, ls_)
    es = [jnp.exp2(l - mx) for l in ls_]
    inv = 1.0 / sum(es)
    attn = None
    for e, o in zip(es, os_):
        w = e * inv
        hi = w.astype(BF16)
        lo = (w - hi.astype(F32)).astype(BF16)
        wide = (jnp.dot(hi, expand_ref[...], preferred_element_type=F32)
                + jnp.dot(lo, expand_ref[...], preferred_element_type=F32))
        attn = wide * o if attn is None else attn + wide * o
    mix_ref[:, 0:D_ATTN] = attn.astype(mix_ref.dtype)

    d_conv = u_ref.shape[-1]
    acc = None
    for j in range(CONV_WIDTH):
        off = CONV_PAD - CONV_LEFT + j
        s = off % SUBLANES
        term = ext_ref[s, pl.ds(off - s, tt), :] * w_ref[j:j + 1, :]
        acc = term if acc is None else acc + term
    c = _ln_swish(acc + b_ref[...], lg_ref[...], lb_ref[...])
    mix_ref[:, D_ATTN:D_ATTN + d_conv] = c.astype(mix_ref.dtype)


def _ln_swish(c, g, b):
    mu = jnp.mean(c, axis=-1, keepdims=True)
    cc = c - mu
    var = jnp.mean(cc * cc, axis=-1, keepdims=True)
    y = cc * lax.rsqrt(var + LN_EPS) * g + b
    return y * _sigmoid(y)


def _cast_blocking(rows, n_steps):
    for n_blocks in range(n_steps, 0, -1):
        if n_steps % n_blocks == 0 and rows % (n_blocks * 2 * SUBLANES) == 0:
            return rows // n_blocks, n_steps // n_blocks
    raise ValueError((rows, n_steps))


def _mix_prompt(u, conv_w, conv_b, ln_g, ln_b, attn, weights, batch, seq, tt):
    d_conv = u.shape[-1]
    u3 = u.reshape(batch, seq, d_conv)
    n_t = seq // tt
    per = tt // CONV_PAD
    dilations = tuple(dil for _, dil in PATTERNS if dil > 1)
    cur = pl.BlockSpec((1, tt, d_conv), lambda b, t: (b, t, 0))
    prev = pl.BlockSpec((1, CONV_PAD, d_conv), lambda b, t: (b, jnp.maximum(t * per - 1, 0), 0))
    flat = lambda width: pl.BlockSpec((tt, width), lambda b, t: (b * n_t + t, 0))
    view = lambda dil, width: pl.BlockSpec((1, tt // dil, dil * width), lambda b, t: (b, t, 0))
    ins = [attn[1][0].reshape(batch * seq, D_ATTN), attn[1][1].reshape(batch * seq, LANES)]
    specs = [flat(D_ATTN), flat(LANES)]
    scratch = [pltpu.VMEM((SUBLANES, CONV_PAD + tt, d_conv), F32)]
    for dil in dilations:
        ins += list(attn[dil])
        specs += [view(dil, D_ATTN), view(dil, LANES)]
        scratch += [_tiles_scratch(tt, D_ATTN), _tiles_scratch(tt, LANES)]
    expand = np.zeros((LANES, D_ATTN), np.float32)
    expand[:N_HEADS] = np.kron(np.eye(N_HEADS), np.ones((1, HEAD_DIM)))
    cast_specs = []
    for w in weights:
        rows, hold = _cast_blocking(w.shape[0], batch * n_t)
        cast_specs.append(pl.BlockSpec((rows, w.shape[1]), lambda b, t, hold=hold: ((b * n_t + t) // hold, 0)))
    res = pl.pallas_call(
        functools.partial(_mix_kernel, tt=tt, dilations=dilations, n_cast=len(weights)),
        grid=(batch, n_t),
        in_specs=[cur, prev, _const_spec(conv_w.shape), _const_spec((1, d_conv)),
                  _const_spec((1, d_conv)), _const_spec((1, d_conv)),
                  _const_spec((LANES, D_ATTN))] + specs + cast_specs,
        out_specs=[pl.BlockSpec((tt, D_ATTN + d_conv), lambda b, t: (b * n_t + t, 0))] + cast_specs,
        out_shape=[jax.ShapeDtypeStruct((batch * seq, D_ATTN + d_conv), BF16)]
        + [jax.ShapeDtypeStruct(w.shape, BF16) for w in weights],
        scratch_shapes=scratch,
        compiler_params=_params(2),
        name="mix_prompt",
    )(u3, u3, conv_w, conv_b, ln_g, ln_b, jnp.asarray(expand, BF16), *ins, *weights)
    return res[0], res[1:]


def _sample_conv_kernel(st_ref, u_ref, w_ref, b_ref, lg_ref, lb_ref, c_ref, ns_ref, us_ref, cs_ref,
                        *, nb, t_new):
    _put_tiles(us_ref, u_ref[...])
    new = [_get_rows(us_ref, t, nb, t_new) for t in range(t_new)]

    def ext(tau):
        return st_ref[tau] if tau < CONV_LEFT else new[tau - CONV_LEFT]

    for t in range(t_new):
        acc = None
        for j in range(CONV_WIDTH):
            term = ext(t + j) * w_ref[j:j + 1, :]
            acc = term if acc is None else acc + term
        c = _ln_swish(acc + b_ref[...], lg_ref[...], lb_ref[...])
        _put_rows(cs_ref, t, t_new, c)
    c_ref[...] = _get_tiles(cs_ref)
    for tau in range(CONV_LEFT):
        ns_ref[tau] = ext(tau + t_new)


def _sample_conv(state_t, u_s, conv_w, conv_b, ln_g, ln_b, layer, t_new, nb):
    _, n, d_conv = state_t.shape
    return pl.pallas_call(
        functools.partial(_sample_conv_kernel, nb=nb, t_new=t_new),
        grid=(n // nb,),
        in_specs=[pl.BlockSpec((CONV_LEFT, nb, d_conv), lambda i: (layer, i, 0)),
                  pl.BlockSpec((nb * t_new, d_conv), lambda i: (i, 0)),
                  _const_spec(conv_w.shape), _const_spec((1, d_conv)),
                  _const_spec((1, d_conv)), _const_spec((1, d_conv))],
        out_specs=[pl.BlockSpec((nb * t_new, d_conv), lambda i: (i, 0)),
                   pl.BlockSpec((CONV_LEFT, nb, d_conv), lambda i: (0, i, 0))],
        out_shape=[jax.ShapeDtypeStruct((n * t_new, d_conv), F32),
                   jax.ShapeDtypeStruct((CONV_LEFT, n, d_conv), F32)],
        scratch_shapes=[_tiles_scratch(nb * t_new, d_conv)] * 2,
        compiler_params=_params(1),
        name="sample_conv",
    )(state_t, u_s, conv_w, conv_b, ln_g, ln_b)


def _multiplicity(dist):
    dist = np.asarray(dist)
    c = np.zeros(dist.shape, np.float32)
    for window, dilation in PATTERNS:
        c += ((dist >= 0) & (dist <= window) & (dist % dilation == 0)).astype(np.float32)
    return c


def _shift_append(old, new_rows, out_ref, t_new):
    n_ch, l_buf = old.shape
    n_tiles = l_buf // LANES
    lane = lax.broadcasted_iota(jnp.int32, (n_ch, LANES), 1)
    keep = lane < LANES - t_new
    pad = jnp.concatenate([jnp.zeros((LANES - t_new, n_ch), F32), new_rows], axis=0)
    nxt = pad.T
    for j in reversed(range(n_tiles)):
        cur = pltpu.roll(old[:, j * LANES:(j + 1) * LANES], LANES - t_new, axis=1)
        out_ref[0, :, j * LANES:(j + 1) * LANES] = jnp.where(keep, cur, nxt)
        nxt = cur


CACHE_HEADS_PER_UNIT = 4


def _sample_attend(q_ref, kn_ref, vn_ref, ck, cv, cnt_ref, attn_ref, t_new):
    width = ck.shape[0]
    n_heads = width // HEAD_DIM
    rows = n_heads * t_new
    kn = kn_ref[0]
    vn = vn_ref[0]

    row = lax.broadcasted_iota(jnp.int32, (rows, width), 0)
    lane = lax.broadcasted_iota(jnp.int32, (rows, width), 1)
    own_head = (row // t_new) == (lane // HEAD_DIM)
    q_rep = jnp.concatenate([q_ref[0]] * n_heads, axis=0)
    q_bd = jnp.where(own_head, q_rep, 0.0)

    cnt = cnt_ref[...]
    s_c = jnp.dot(q_bd.astype(BF16), ck.astype(BF16), preferred_element_type=F32)
    s_c = jnp.where(cnt > 0.0, s_c, NEG_BIG)
    m = jnp.max(s_c, axis=-1, keepdims=True)

    t_of_row = lax.broadcasted_iota(jnp.int32, (rows, 1), 0) % t_new
    s_n, c_n = [], []
    for tp in range(t_new):
        d = t_of_row - tp
        c = jnp.zeros((rows, 1), F32)
        for _, dilation in PATTERNS:
            c = c + jnp.where((d >= 0) & (d % dilation == 0), 1.0, 0.0)
        s = jnp.sum(q_bd * kn[tp:tp + 1, :], axis=-1, keepdims=True)
        s = jnp.where(c > 0.0, s, NEG_BIG)
        m = jnp.maximum(m, s)
        s_n.append(s)
        c_n.append(c)

    p_c = cnt * jnp.exp2(s_c - m)
    l = jnp.sum(p_c, axis=-1, keepdims=True)
    acc = lax.dot_general(p_c.astype(BF16), cv.astype(BF16), (((1,), (1,)), ((), ())),
                          preferred_element_type=F32)
    for tp in range(t_new):
        p = c_n[tp] * jnp.exp2(s_n[tp] - m)
        l = l + p
        acc = acc + p * vn[tp:tp + 1, :]
    acc = jnp.where(own_head, acc * (1.0 / l), 0.0)
    out = acc[0:t_new, :]
    for h in range(1, n_heads):
        out = out + acc[h * t_new:(h + 1) * t_new, :]
    attn_ref[0] = out.astype(attn_ref.dtype)


CACHE_RING = 3
RING_DMA_PRIORITY = 1


def _swiglu_chunk(h, wg_ref, wu_ref, wd_ref, j):
    cols = pl.ds(pl.multiple_of(j * MXU_COLS, MXU_COLS), MXU_COLS)
    gate = jnp.dot(h, wg_ref[:, cols], preferred_element_type=F32)
    up = jnp.dot(h, wu_ref[:, cols], preferred_element_type=F32)
    act = (gate * _sigmoid(gate) * up).astype(BF16)
    return jnp.dot(act, wd_ref[j], preferred_element_type=F32)


def _ffn_cache_kernel(x_ref, mix_ref, wo_ref, g_ref, wg_ref, wu_ref, wd_ref,
                      q_ref, kn_ref, vn_ref, ck_hbm, cv_hbm, cnt_ref,
                      y_ref, attn_ref, ok_ref, ov_ref, h_ref, kbuf, vbuf, sem,
                      *, steps_per_tile, t_new, first_row, units_per_row):
    u = pl.program_id(0)
    n_units = pl.num_programs(0)
    c = u % steps_per_tile
    base, extra = divmod(wd_ref.shape[0], steps_per_tile)
    uw = kbuf.shape[1]

    def fetch(unit, slot):
        row = first_row + unit // units_per_row
        h0 = pl.multiple_of((unit % units_per_row) * uw, uw)
        return (pltpu.make_async_copy(ck_hbm.at[row, pl.ds(h0, uw), :], kbuf.at[slot], sem.at[0, slot]),
                pltpu.make_async_copy(cv_hbm.at[row, pl.ds(h0, uw), :], vbuf.at[slot], sem.at[1, slot]))

    @pl.when(u == 0)
    def _():
        for unit in range(CACHE_RING - 1):
            for cp in fetch(unit, unit):
                cp.start(priority=RING_DMA_PRIORITY)

    ahead = u + (CACHE_RING - 1)

    @pl.when(ahead < n_units)
    def _():
        for cp in fetch(ahead, ahead % CACHE_RING):
            cp.start(priority=RING_DMA_PRIORITY)

    @pl.when(c == 0)
    def _():
        x1 = x_ref[...] + jnp.dot(mix_ref[...], wo_ref[...], preferred_element_type=F32)
        y_ref[...] = x1
        h_ref[...] = _rms_rows(x1, g_ref[...]).astype(BF16)

    slot = u % CACHE_RING
    for cp in fetch(u, slot):
        cp.wait()
    ck, cv = kbuf[slot], vbuf[slot]
    _shift_append(ck, kn_ref[0], ok_ref, t_new)
    _shift_append(cv, vn_ref[0], ov_ref, t_new)
    _sample_attend(q_ref, kn_ref, vn_ref, ck, cv, cnt_ref, attn_ref, t_new)
    for i in range(base):
        y_ref[...] += _swiglu_chunk(h_ref[...], wg_ref, wu_ref, wd_ref, c * base + i)
    if extra:
        first = steps_per_tile - extra

        @pl.when(c >= first)
        def _():
            y_ref[...] += _swiglu_chunk(h_ref[...], wg_ref, wu_ref, wd_ref, steps_per_tile * base + c - first)


def _ffn_with_cache(x2d, mix, wo_b, g, wg3, wu3, wd3, q_s, k_s, v_s, cache_kt, cache_vt, layer, tm):
    m, d = x2d.shape
    n, t_new, _ = q_s.shape
    l_buf = cache_kt.shape[2]
    assert l_buf == WIN_MAX and t_new % SUBLANES == 0
    uw = CACHE_HEADS_PER_UNIT * HEAD_DIM
    per_n = D_ATTN // uw
    n_tiles, n_units = m // tm, n * per_n
    assert n_units % n_tiles == 0
    steps_per_tile = n_units // n_tiles
    assert wd3.shape[0] >= steps_per_tile and n_units >= CACHE_RING
    dist = l_buf + np.arange(t_new)[:, None] - np.arange(l_buf)[None, :]
    cnt = np.tile(_multiplicity(dist), (CACHE_HEADS_PER_UNIT, 1))

    tile = lambda width: pl.BlockSpec((tm, width), lambda u: (u // steps_per_tile, 0))
    new = pl.BlockSpec((1, t_new, uw), lambda u: (u // per_n, 0, u % per_n))
    big = pl.BlockSpec((1, uw, l_buf), lambda u: (u // per_n, u % per_n, 0))
    in_hbm = pl.BlockSpec(memory_space=pl.ANY)
    return pl.pallas_call(
        functools.partial(_ffn_cache_kernel, steps_per_tile=steps_per_tile, t_new=t_new,
                          first_row=layer * n, units_per_row=per_n),
        grid=(n_units,),
        in_specs=[tile(d), tile(mix.shape[1]), _const_spec(wo_b.shape), _const_spec((1, d)),
                  _const_spec(wg3.shape), _const_spec(wu3.shape), _const_spec(wd3.shape),
                  new, new, new, in_hbm, in_hbm, _const_spec(cnt.shape)],
        out_specs=[tile(d), new, big, big],
        out_shape=[jax.ShapeDtypeStruct((m, d), F32),
                   jax.ShapeDtypeStruct((n, t_new, D_ATTN), F32),
                   jax.ShapeDtypeStruct((n, D_ATTN, l_buf), F32),
                   jax.ShapeDtypeStruct((n, D_ATTN, l_buf), F32)],
        scratch_shapes=[pltpu.VMEM((tm, d), BF16),
                        pltpu.VMEM((CACHE_RING, uw, l_buf), F32), pltpu.VMEM((CACHE_RING, uw, l_buf), F32),
                        pltpu.SemaphoreType.DMA((2, CACHE_RING))],
        compiler_params=_params(1),
        name="ffn_cache",
    )(x2d, mix, wo_b, g, wg3, wu3, wd3, q_s, k_s, v_s, cache_kt, cache_vt, jnp.asarray(cnt))


def _ffn_kernel(x_ref, *refs, n_mix):
    mix_refs = refs[0:n_mix]
    wo_ref, g_ref, wg_ref, wu_ref, wd_ref, y_ref = refs[n_mix:]
    x1 = x_ref[...]
    row = 0
    for mr in mix_refs:
        width = mr.shape[1]
        x1 = x1 + jnp.dot(mr[...].astype(BF16), wo_ref[row:row + width, :], preferred_element_type=F32)
        row += width
    h = _rms_rows(x1, g_ref[...]).astype(BF16)
    y_ref[...] = x1
    for j in range(wd_ref.shape[0]):
        y_ref[...] += _swiglu_chunk(h, wg_ref, wu_ref, wd_ref, j)


def _outproj_ffn(x2d, mix_parts, wo_b, g, wg3, wu3, wd3, tm):
    m, d = x2d.shape
    tm = min(tm, m)
    row = lambda i: (i, 0)
    return pl.pallas_call(
        functools.partial(_ffn_kernel, n_mix=len(mix_parts)),
        grid=(m // tm,),
        in_specs=[pl.BlockSpec((tm, d), row)] + [pl.BlockSpec((tm, p.shape[1]), row) for p in mix_parts]
        + [_const_spec(wo_b.shape), _const_spec((1, d)),
           _const_spec(wg3.shape), _const_spec(wu3.shape), _const_spec(wd3.shape)],
        out_specs=pl.BlockSpec((tm, d), row),
        out_shape=jax.ShapeDtypeStruct((m, d), F32),
        compiler_params=_params(1),
        name="outproj_ffn",
    )(x2d, *mix_parts, wo_b, g, wg3, wu3, wd3)


TOKEN_TILE = 512
INPROJ_TILE = 1024
MIX_TIME_TILE = 1024
SAMPLE_CONV_BATCH = 32


def kernel(x_prompt, x_sample, cache_k, cache_v, state_conv, attn_norm_g, w_in, q_norm_g, k_norm_g,
           conv_w, conv_b, conv_ln_g, conv_ln_b, w_out, ffn_norm_g, w_gate, w_up, w_down):
    batch, seq, d_model = x_prompt.shape
    n_dec, t_new, _ = x_sample.shape
    depth = w_in.shape[0]
    d_conv = conv_w.shape[-1]
    l_buf = cache_k.shape[2]
    assert seq == WIN_MAX and l_buf == WIN_MAX and seq % (BLK * PATTERNS[-1][1]) == 0
    assert w_in.shape[-1] == 3 * D_ATTN + 2 * d_conv and d_conv == D_ATTN

    seg = jnp.asarray(np.kron(np.eye(N_HEADS), np.ones((HEAD_DIM, HEAD_DIM))), BF16)
    row = lambda v: v.reshape(1, -1)
    to_t = lambda c: jnp.transpose(c, (0, 1, 3, 4, 2)).reshape(depth * n_dec, D_ATTN, l_buf)
    from_t = lambda c, n: jnp.transpose(c.reshape(n, N_HEADS, HEAD_DIM, -1), (0, 3, 1, 2))
    cache_kt, cache_vt = to_t(cache_k), to_t(cache_v)
    state_t = jnp.transpose(state_conv, (0, 2, 1, 3)).reshape(depth * CONV_LEFT, n_dec, d_conv)

    y_p = x_prompt.reshape(batch * seq, d_model)
    y_s = x_sample.reshape(n_dec * t_new, d_model)
    outs = [[] for _ in range(6)]
    for l in range(depth):
        w_in_b = w_in[l].astype(BF16)
        g_attn, g_ffn = row(attn_norm_g[l]), row(ffn_norm_g[l])
        qg = row(jnp.tile(q_norm_g[l], N_HEADS))
        kg = row(jnp.tile(k_norm_g[l], N_HEADS))
        cb, lg, lb = row(conv_b[l]), row(conv_ln_g[l]), row(conv_ln_b[l])

        q, k, v, u_s = _inproj_sample(y_s, g_attn, w_in_b, qg, kg, seg, TOKEN_TILE)
        c_s, ns = _sample_conv(state_t, u_s, conv_w[l], cb, lg, lb, l, t_new, min(SAMPLE_CONV_BATCH, n_dec))
        qkv, kt, vt, u = _inproj_prompt(y_p, g_attn, w_in_b, qg, kg, seg, batch, seq, INPROJ_TILE)
        attn = {dil: _band_attention(*qkv[dil], w, dil) for w, dil in PATTERNS}
        mix, (wo_b, wg3, wu3, wd_b) = _mix_prompt(u, conv_w[l], cb, lg, lb, attn,
                                                   (w_out[l], w_gate[l], w_up[l], w_down[l]),
                                                   batch, seq, MIX_TIME_TILE)
        wd3 = wd_b.reshape(wd_b.shape[0] // MXU_COLS, MXU_COLS, d_model)

        as3 = lambda t: t.reshape(n_dec, t_new, D_ATTN)
        y_p, attn_s, nk, nv = _ffn_with_cache(y_p, mix, wo_b, g_ffn, wg3, wu3, wd3,
                                              as3(q), as3(k), as3(v), cache_kt, cache_vt, l, TOKEN_TILE)
        y_s = _outproj_ffn(y_s, [attn_s.reshape(n_dec * t_new, D_ATTN), c_s],
                           wo_b, g_ffn, wg3, wu3, wd3, TOKEN_TILE)
        outs[0].append(from_t(kt, batch))
        outs[1].append(from_t(vt, batch))
        outs[2].append(u.reshape(batch, seq, d_conv)[:, seq - CONV_LEFT:])
        outs[3].append(from_t(nk, n_dec))
        outs[4].append(from_t(nv, n_dec))
        outs[5].append(jnp.transpose(ns, (1, 0, 2)))

    stack = lambda xs: xs[0][None] if len(xs) == 1 else jnp.stack(xs)
    return (y_p.reshape(batch, seq, d_model), y_s.reshape(n_dec, t_new, d_model)) + tuple(
        stack(o) for o in outs)
```
